```python
import math
import jax, jax.numpy as jnp
from jax import lax
import numpy as np

D_MODEL = 1024
BATCH = 2
SEQ = 16384
DEPTH = 4

CHUNK = 64
MEM_LEN = 256
D_MIX = D_MODEL
SSD_WIDTH = D_MIX // 2
SSD_HEAD_DIM = 64
SSD_HEADS = SSD_WIDTH // SSD_HEAD_DIM
SSD_GROUPS = 2
SSD_STATE = 128
SSD_CONV = 4
SSD_CONV_CH = SSD_WIDTH + 2 * SSD_GROUPS * SSD_STATE
FOX_WIDTH = D_MIX // 4
FOX_HEAD_DIM = 64
FOX_HEADS = FOX_WIDTH // FOX_HEAD_DIM
FOX_BLOCK = 128
POOL_WIDTH = D_MIX - SSD_WIDTH - FOX_WIDTH
POOL_WINDOWS = (2, 4, 8, 16)
POOL_GROUPS = len(POOL_WINDOWS)
POOL_GROUP_DIM = POOL_WIDTH // POOL_GROUPS
IN_SIZES = (SSD_WIDTH, SSD_CONV_CH, SSD_HEADS, FOX_WIDTH, FOX_WIDTH, FOX_WIDTH, FOX_HEADS, POOL_WIDTH)
D_IN = sum(IN_SIZES)
XATTN_HEADS = 4
XATTN_HEAD_DIM = D_MODEL // XATTN_HEADS
D_FF = 7 * D_MODEL // 2
N_EXPERTS = 8
TOP_K = 2
MOE_BLOCK = 256
N_DENSE = (DEPTH + 1) // 2
N_MOE = DEPTH // 2
DN_ALPHA = (2 * DEPTH) ** 0.25
DN_BETA = (8 * DEPTH) ** -0.25
LN_EPS = 1e-5
RMS_EPS = 1e-5

kernel_name = "hybrid_ssd_fox_pool_moe_deepnorm"


def layer_norm(x, g, b):
    xf = x.astype(jnp.float32)
    mu = jnp.mean(xf, axis=-1, keepdims=True)
    var = jnp.mean(jnp.square(xf - mu), axis=-1, keepdims=True)
    return ((xf - mu) * lax.rsqrt(var + LN_EPS) * g + b).astype(x.dtype)


def split_in(h):
    outs, off = [], 0
    for s in IN_SIZES:
        outs.append(h[..., off:off + s])
        off += s
    return outs


def causal_depthwise_conv(u, w, b):
    K, T = w.shape[0], u.shape[1]
    up = jnp.pad(u, ((0, 0), (K - 1, 0), (0, 0)))
    return sum(up[:, k:k + T] * w[k] for k in range(K)) + b


def ssd_mixer(z, xbc, dt_raw, conv_w, conv_b, dt_bias, a_log, d_skip, norm_w):
    Bsz, T, _ = z.shape
    G, R, P, N, L = SSD_GROUPS, SSD_HEADS // SSD_GROUPS, SSD_HEAD_DIM, SSD_STATE, CHUNK
    nc = T // L
    xbc = jax.nn.silu(causal_depthwise_conv(xbc, conv_w, conv_b))
    xs = xbc[..., :SSD_WIDTH]
    bm = xbc[..., SSD_WIDTH:SSD_WIDTH + G * N].reshape(Bsz, nc, L, G, N)
    cm = xbc[..., SSD_WIDTH + G * N:].reshape(Bsz, nc, L, G, N)
    dt = jax.nn.softplus(dt_raw.astype(jnp.float32) + dt_bias)
    a = -jnp.exp(a_log.astype(jnp.float32))
    X = xs.reshape(Bsz, nc, L, G, R, P) * dt.reshape(Bsz, nc, L, G, R)[..., None]
    A = (dt * a).reshape(Bsz, nc, L, G, R).transpose(0, 3, 4, 1, 2)
    a_cs = jnp.cumsum(A, axis=-1)
    seg = a_cs[..., :, None] - a_cs[..., None, :]
    causal = jnp.tril(jnp.ones((L, L), dtype=bool))
    Lmat = jnp.exp(jnp.where(causal, seg, -jnp.inf))
    cb = jnp.einsum('bclgn,bcsgn->bcgls', cm, bm)
    y_diag = jnp.einsum('bcgls,bgrcls,bcsgrp->bclgrp', cb, Lmat, X)
    decay_states = jnp.exp(a_cs[..., -1:] - a_cs)
    states = jnp.einsum('bclgn,bgrcl,bclgrp->bcgrpn', bm, decay_states, X)
    chunk_decay = jnp.exp(a_cs[..., -1])

    def step(h, inp):
        s_c, d_c = inp
        return h * d_c[..., None, None] + s_c, h

    h0 = jnp.zeros((Bsz, G, R, P, N), states.dtype)
    _, prev = lax.scan(step, h0, (jnp.moveaxis(states, 1, 0), jnp.moveaxis(chunk_decay, -1, 0)))
    prev = jnp.moveaxis(prev, 0, 1)
    y_off = jnp.einsum('bclgn,bcgrpn,bgrcl->bclgrp', cm, prev, jnp.exp(a_cs))
    y = (y_diag + y_off).reshape(Bsz, T, SSD_HEADS, P) + xs.reshape(Bsz, T, SSD_HEADS, P) * d_skip[:, None]
    y = y.reshape(Bsz, T, SSD_WIDTH) * jax.nn.silu(z)
    yg = y.reshape(Bsz, T, G, SSD_WIDTH // G).astype(jnp.float32)
    yg = yg * lax.rsqrt(jnp.mean(jnp.square(yg), axis=-1, keepdims=True) + RMS_EPS)
    return (yg.reshape(Bsz, T, SSD_WIDTH) * norm_w).astype(z.dtype)


def fox_mixer(q, k, v, f_logit, f_bias):
    Bsz, T, _ = q.shape
    H, Dh = FOX_HEADS, FOX_HEAD_DIM
    q = q.reshape(Bsz, T, H, Dh).transpose(0, 2, 1, 3)
    k = k.reshape(Bsz, T, H, Dh).transpose(0, 2, 1, 3)
    v = v.reshape(Bsz, T, H, Dh).transpose(0, 2, 1, 3)
    log_f = jax.nn.log_sigmoid(f_logit.astype(jnp.float32) + f_bias)
    c = jnp.cumsum(log_f, axis=1).transpose(0, 2, 1)
    nq = T // FOX_BLOCK
    q_blocks = q.reshape(Bsz, H, nq, FOX_BLOCK, Dh).transpose(2, 0, 1, 3, 4)
    c_blocks = c.reshape(Bsz, H, nq, FOX_BLOCK).transpose(2, 0, 1, 3)
    key_pos = jnp.arange(T)
    scale = Dh ** -0.5

    def block(args):
        qb, cq, i = args
        q_pos = i * FOX_BLOCK + jnp.arange(FOX_BLOCK)
        s = jnp.einsum('bhqd,bhkd->bhqk', qb, k).astype(jnp.float32) * scale
        s = s + cq[..., :, None] - c[:, :, None, :]
        s = jnp.where(key_pos[None, :] <= q_pos[:, None], s, -jnp.inf)
        p = jax.nn.softmax(s, axis=-1)
        return jnp.einsum('bhqk,bhkd->bhqd', p.astype(v.dtype), v)

    out = lax.map(block, (q_blocks, c_blocks, jnp.arange(nq)))
    return out.transpose(1, 0, 3, 2, 4).reshape(Bsz, T, FOX_WIDTH)


def pool_mixer(u, w, b, scale):
    Bsz, T, _ = u.shape
    ug = u.reshape(Bsz, T, POOL_GROUPS, POOL_GROUP_DIM).astype(jnp.float32)
    cs = jnp.cumsum(ug, axis=1)
    means = []
    for g, win in enumerate(POOL_WINDOWS):
        c_g = cs[:, :, g]
        lagged = jnp.pad(c_g, ((0, 0), (win, 0), (0, 0)))[:, :T]
        count = jnp.minimum(jnp.arange(1, T + 1), win).astype(jnp.float32)[None, :, None]
        means.append((c_g - lagged) / count)
    pooled = jnp.stack(means, axis=2) - ug
    y = jnp.einsum('btgc,gcd->btgd', pooled, w) + b
    return (y.reshape(Bsz, T, POOL_WIDTH) * scale).astype(u.dtype)


def cross_attention(x, mem, wq, wk, wv, wo):
    Bsz, T, D = x.shape
    M = mem.shape[1]
    q = (x @ wq).reshape(Bsz, T, XATTN_HEADS, XATTN_HEAD_DIM)
    k = (mem @ wk).reshape(Bsz, M, XATTN_HEADS, XATTN_HEAD_DIM)
    v = (mem @ wv).reshape(Bsz, M, XATTN_HEADS, XATTN_HEAD_DIM)
    s = jnp.einsum('bqhd,bkhd->bhqk', q, k).astype(jnp.float32) * XATTN_HEAD_DIM ** -0.5
    p = jax.nn.softmax(s, axis=-1)
    o = jnp.einsum('bhqk,bkhd->bqhd', p.astype(v.dtype), v).reshape(Bsz, T, D)
    return o @ wo


def swiglu(x, w1, w3, w2):
    return (jax.nn.silu(x @ w1) * (x @ w3)) @ w2


def moe_swiglu(x, router_w, w1, w3, w2):
    Bsz, T, D = x.shape
    xf = x.reshape(-1, D)
    N = xf.shape[0]
    NK = N * TOP_K
    logits = (xf @ router_w).astype(jnp.float32)
    top_val, top_idx = lax.top_k(logits, TOP_K)
    gates = jax.nn.softmax(top_val, axis=-1)
    flat_e = top_idx.reshape(-1)
    flat_tok = jnp.arange(NK) // TOP_K
    flat_g = gates.reshape(-1)
    order = jnp.argsort(flat_e)
    sorted_e = flat_e[order]
    counts = jnp.zeros((N_EXPERTS,), jnp.int32).at[flat_e].add(1)
    padded = (counts + MOE_BLOCK - 1) // MOE_BLOCK * MOE_BLOCK
    start = jnp.cumsum(counts) - counts
    pend = jnp.cumsum(padded)
    pstart = pend - padded
    dest = pstart[sorted_e] + (jnp.arange(NK) - start[sorted_e])
    cap = -(-NK // MOE_BLOCK) * MOE_BLOCK + N_EXPERTS * MOE_BLOCK
    nblk = cap // MOE_BLOCK
    slot_tok = jnp.zeros((cap,), jnp.int32).at[dest].set(flat_tok[order])
    slot_gate = jnp.zeros((cap,), jnp.float32).at[dest].set(flat_g[order])
    block_exp = jnp.minimum(jnp.searchsorted(pend, jnp.arange(nblk) * MOE_BLOCK, side='right'), N_EXPERTS - 1)
    xs = xf[slot_tok].reshape(nblk, MOE_BLOCK, D)

    def run(args):
        xb, e = args
        return swiglu(xb, w1[e], w3[e], w2[e])

    ys = lax.map(run, (xs, block_exp)).reshape(cap, D)
    contrib = (ys * slot_gate[:, None]).astype(x.dtype)
    out = jnp.zeros((N, D), x.dtype).at[slot_tok].add(contrib)
    return out.reshape(Bsz, T, D)


def setup_inputs(seed: int = 0) -> dict:
    key = jax.random.key(seed)
    ks = iter(jax.random.split(key, 40))
    nrm = lambda shape, s: jax.random.normal(next(ks), shape, jnp.float32) * s
    x = nrm((BATCH, SEQ, D_MODEL), 1.0)
    mem = nrm((BATCH, MEM_LEN, D_MODEL), 1.0)
    w_in = nrm((DEPTH, D_MODEL, D_IN), D_MODEL ** -0.5)
    ssm_conv_w = nrm((DEPTH, SSD_CONV, SSD_CONV_CH), SSD_CONV ** -0.5)
    ssm_conv_b = nrm((DEPTH, SSD_CONV_CH), 0.02)
    u = jax.random.uniform(next(ks), (DEPTH, SSD_HEADS), jnp.float32)
    dt0 = jnp.exp(u * (math.log(0.1) - math.log(0.001)) + math.log(0.001))
    ssm_dt_bias = dt0 + jnp.log(-jnp.expm1(-dt0))
    ssm_a_log = jnp.log(jax.random.uniform(next(ks), (DEPTH, SSD_HEADS), jnp.float32, 1.0, 16.0))
    ssm_d = 1.0 + nrm((DEPTH, SSD_HEADS), 0.1)
    ssm_norm_w = 1.0 + nrm((DEPTH, SSD_WIDTH), 0.02)
    fox_f_bias = jax.random.uniform(next(ks), (DEPTH, FOX_HEADS), jnp.float32, 2.0, 7.0)
    pool_w = nrm((DEPTH, POOL_GROUPS, POOL_GROUP_DIM, POOL_GROUP_DIM), POOL_GROUP_DIM ** -0.5)
    pool_b = nrm((DEPTH, POOL_GROUPS, POOL_GROUP_DIM), 0.02)
    pool_scale = 1.0 + nrm((DEPTH, POOL_WIDTH), 0.1)
    w_out = nrm((DEPTH, D_MIX, D_MODEL), D_MIX ** -0.5 * DN_BETA)
    ln1_g = 1.0 + nrm((DEPTH, D_MODEL), 0.02)
    ln1_b = nrm((DEPTH, D_MODEL), 0.02)
    xa_wq = nrm((DEPTH, D_MODEL, D_MODEL), D_MODEL ** -0.5)
    xa_wk = nrm((DEPTH, D_MODEL, D_MODEL), D_MODEL ** -0.5)
    xa_wv = nrm((DEPTH, D_MODEL, D_MODEL), D_MODEL ** -0.5)
    xa_wo = nrm((DEPTH, D_MODEL, D_MODEL), D_MODEL ** -0.5 * DN_BETA)
    ln2_g = 1.0 + nrm((DEPTH, D_MODEL), 0.02)
    ln2_b = nrm((DEPTH, D_MODEL), 0.02)
    ffn_w1 = nrm((N_DENSE, D_MODEL, D_FF), D_MODEL ** -0.5)
    ffn_w3 = nrm((N_DENSE, D_MODEL, D_FF), D_MODEL ** -0.5)
    ffn_w2 = nrm((N_DENSE, D_FF, D_MODEL), D_FF ** -0.5 * DN_BETA)
    router_w = nrm((N_MOE, D_MODEL, N_EXPERTS), D_MODEL ** -0.5)
    moe_w1 = nrm((N_MOE, N_EXPERTS, D_MODEL, D_FF), D_MODEL ** -0.5)
    moe_w3 = nrm((N_MOE, N_EXPERTS, D_MODEL, D_FF), D_MODEL ** -0.5)
    moe_w2 = nrm((N_MOE, N_EXPERTS, D_FF, D_MODEL), D_FF ** -0.5 * DN_BETA)
    ln3_g = 1.0 + nrm((DEPTH, D_MODEL), 0.02)
    ln3_b = nrm((DEPTH, D_MODEL), 0.02)
    return {"x": x, "mem": mem, "w_in": w_in, "ssm_conv_w": ssm_conv_w, "ssm_conv_b": ssm_conv_b,
            "ssm_dt_bias": ssm_dt_bias, "ssm_a_log": ssm_a_log, "ssm_d": ssm_d, "ssm_norm_w": ssm_norm_w,
            "fox_f_bias": fox_f_bias, "pool_w": pool_w, "pool_b": pool_b, "pool_scale": pool_scale,
            "w_out": w_out, "ln1_g": ln1_g, "ln1_b": ln1_b, "xa_wq": xa_wq, "xa_wk": xa_wk,
            "xa_wv": xa_wv, "xa_wo": xa_wo, "ln2_g": ln2_g, "ln2_b": ln2_b, "ffn_w1": ffn_w1,
            "ffn_w3": ffn_w3, "ffn_w2": ffn_w2, "router_w": router_w, "moe_w1": moe_w1,
            "moe_w3": moe_w3, "moe_w2": moe_w2, "ln3_g": ln3_g, "ln3_b": ln3_b}


def reference(x, mem, w_in, ssm_conv_w, ssm_conv_b, ssm_dt_bias, ssm_a_log, ssm_d, ssm_norm_w,
              fox_f_bias, pool_w, pool_b, pool_scale, w_out, ln1_g, ln1_b, xa_wq, xa_wk, xa_wv,
              xa_wo, ln2_g, ln2_b, ffn_w1, ffn_w3, ffn_w2, router_w, moe_w1, moe_w3, moe_w2,
              ln3_g, ln3_b):
    for layer in range(DEPTH):
        h = x @ w_in[layer]
        z, xbc, dt_raw, q, k, v, f_logit, pool_in = split_in(h)
        y_ssd = ssd_mixer(z, xbc, dt_raw, ssm_conv_w[layer], ssm_conv_b[layer], ssm_dt_bias[layer],
                          ssm_a_log[layer], ssm_d[layer], ssm_norm_w[layer])
        y_fox = fox_mixer(q, k, v, f_logit, fox_f_bias[layer])
        y_pool = pool_mixer(pool_in, pool_w[layer], pool_b[layer], pool_scale[layer])
        mix = jnp.concatenate([y_ssd, y_fox, y_pool], axis=-1) @ w_out[layer]
        x = layer_norm(DN_ALPHA * x + mix, ln1_g[layer], ln1_b[layer])
        xa = cross_attention(x, mem, xa_wq[layer], xa_wk[layer], xa_wv[layer], xa_wo[layer])
        x = layer_norm(DN_ALPHA * x + xa, ln2_g[layer], ln2_b[layer])
        j = layer // 2
        if layer % 2 == 0:
            ff = swiglu(x, ffn_w1[j], ffn_w3[j], ffn_w2[j])
        else:
            ff = moe_swiglu(x, router_w[j], moe_w1[j], moe_w3[j], moe_w2[j])
        x = layer_norm(DN_ALPHA * x + ff, ln3_g[layer], ln3_b[layer])
    return x
```

```python
import functools

import jax
import jax.numpy as jnp
from jax import lax
from jax.experimental import pallas as pl
from jax.experimental.pallas import tpu as pltpu

F32 = jnp.float32
BF16 = jnp.bfloat16

D_MODEL = 1024
DEPTH = 4
SSD_WIDTH = 512
SSD_HEAD_DIM = 64
SSD_HEADS = 8
SSD_GROUPS = 2
SSD_STATE = 128
SSD_CONV = 4
SSD_CONV_CH = 1024
FOX_WIDTH = 256
FOX_HEAD_DIM = 64
FOX_HEADS = 4
POOL_WIDTH = 256
POOL_WINDOWS = (2, 4, 8, 16)
POOL_GROUP_DIM = 64
IN_SIZES = (512, 1024, 8, 256, 256, 256, 4, 256)
XATTN_HEADS = 4
XATTN_HEAD_DIM = 256
D_FF = 3584
N_EXPERTS = 8
TOP_K = 2
DN_ALPHA = (2 * DEPTH) ** 0.25
LN_EPS = 1e-5
RMS_EPS = 1e-5

LANES = 128
MIB = 1024 * 1024

IN_TM = 512
SSD_TB = 512
SSD_L = 128
FOX_TQ = 512
FOX_TK = 512
POOL_TB = 512
POOL_PIECE = 128
OUT_TM = 512
XA_TM = 512
FFN_TM = 1024
FFN_TF = 512
MOE_TM = 512
ROUTE_TM = 512
ROW_TM = 512


def _cparams(sem, vmem_mib=48):
    return pltpu.CompilerParams(dimension_semantics=sem, vmem_limit_bytes=vmem_mib * MIB)


def _dot(a, b):
    return jnp.dot(a, b, preferred_element_type=F32)


def _dot_nt(a, b):
    return lax.dot_general(a, b, (((1,), (1,)), ((), ())), preferred_element_type=F32)


def _dot_tn(a, b):
    return lax.dot_general(a, b, (((0,), (0,)), ((), ())), preferred_element_type=F32)


def _split3(x):
    hi = x.astype(BF16)
    r1 = x - hi.astype(F32)
    mid = r1.astype(BF16)
    lo = (r1 - mid.astype(F32)).astype(BF16)
    return hi, mid, lo


def _silu(x):
    return x / (1.0 + jnp.exp(-x))


def _softplus(x):
    return jnp.maximum(x, 0.0) + jnp.log1p(jnp.exp(-jnp.abs(x)))


def _log_sigmoid(x):
    return jnp.minimum(x, 0.0) - jnp.log1p(jnp.exp(-jnp.abs(x)))


def _layer_norm(v, g, b):
    mu = jnp.mean(v, axis=-1, keepdims=True)
    d = v - mu
    var = jnp.mean(d * d, axis=-1, keepdims=True)
    return d * lax.rsqrt(var + LN_EPS) * g + b


def _tri_ones(n, lower):
    r = lax.broadcasted_iota(jnp.int32, (n, n), 0)
    c = lax.broadcasted_iota(jnp.int32, (n, n), 1)
    m = (c <= r) if lower else (r <= c)
    return jnp.where(m, 1.0, 0.0).astype(BF16)


IN_MAIN_COLS = 1024 + 512 + 256 + 256 + 256 + 512


def _in_proj_kernel(x_ref, wm_ref, wkt_ref, ws_ref, wst_ref, fbc_ref, fbr_ref,
                    xbc_ref, z_ref, q_ref, v_ref, pool_ref, dte_ref, kt_ref, ccol_ref, rows_ref,
                    carry_c, carry_r, *, tiles_per_batch):
    i = pl.program_id(0)
    tm = x_ref.shape[0]
    xb = x_ref[...].astype(BF16)

    xbc_ref[...] = _dot(xb, wm_ref[:, 0:1024])
    z_ref[...] = _dot(xb, wm_ref[:, 1024:1536])
    q_ref[...] = _dot(xb, wm_ref[:, 1536:1792]).astype(BF16)
    v_ref[...] = _dot(xb, wm_ref[:, 1792:2048]).astype(BF16)
    pool_ref[...] = _dot(xb, wm_ref[:, 2048:2304])
    dte_ref[...] = _dot(xb, wm_ref[:, 2304:2816])
    kt_ref[...] = _dot_nt(wkt_ref[...], xb).astype(BF16)

    small_c = _dot(xb, ws_ref[...])
    small_r = _dot_nt(wst_ref[...], xb)

    @pl.when(i % tiles_per_batch == 0)
    def _():
        carry_c[...] = jnp.zeros_like(carry_c)
        carry_r[...] = jnp.zeros_like(carry_r)

    lf_c = _log_sigmoid(small_c + fbc_ref[...])
    lf_r = _log_sigmoid(small_r + fbr_ref[...])
    tri = _tri_ones(tm, lower=True)
    upp = _tri_ones(tm, lower=False)
    c3 = _split3(lf_c)
    r3 = _split3(lf_r)
    cs_c = _dot(tri, c3[0]) + _dot(tri, c3[1]) + _dot(tri, c3[2]) + carry_c[...]
    cs_r = _dot(r3[0], upp) + _dot(r3[1], upp) + _dot(r3[2], upp) + carry_r[:, 0:1]
    carry_c[...] = cs_c[tm - 1:tm, :]
    carry_r[...] = jnp.broadcast_to(cs_r[:, tm - 1:tm], carry_r.shape)
    ccol_ref[...] = cs_c
    rowid = lax.broadcasted_iota(jnp.int32, small_r.shape, 0)
    rows_ref[...] = jnp.where(rowid < SSD_HEADS, small_r, cs_r)


def _in_proj(x2, wm, wkt, ws, wst, fbc, fbr, *, seq_len, tm):
    n = x2.shape[0]
    grid = (n // tm,)
    full = lambda a: pl.BlockSpec(a.shape, lambda i: (0,) * a.ndim)
    rowblk = lambda w: pl.BlockSpec((tm, w), lambda i: (i, 0))
    out_shape = (
        jax.ShapeDtypeStruct((n, 1024), F32),
        jax.ShapeDtypeStruct((n, 512), F32),
        jax.ShapeDtypeStruct((n, 256), BF16),
        jax.ShapeDtypeStruct((n, 256), BF16),
        jax.ShapeDtypeStruct((n, 256), F32),
        jax.ShapeDtypeStruct((n, 512), F32),
        jax.ShapeDtypeStruct((256, n), BF16),
        jax.ShapeDtypeStruct((n, LANES), F32),
        jax.ShapeDtypeStruct((16, n), F32),
    )
    out_specs = (rowblk(1024), rowblk(512), rowblk(256), rowblk(256), rowblk(256), rowblk(512),
                 pl.BlockSpec((256, tm), lambda i: (0, i)),
                 rowblk(LANES),
                 pl.BlockSpec((16, tm), lambda i: (0, i)))
    return pl.pallas_call(
        functools.partial(_in_proj_kernel, tiles_per_batch=seq_len // tm),
        out_shape=out_shape,
        grid=grid,
        in_specs=[rowblk(D_MODEL), full(wm), full(wkt), full(ws), full(wst), full(fbc), full(fbr)],
        out_specs=out_specs,
        scratch_shapes=[pltpu.VMEM((1, LANES), F32), pltpu.VMEM((16, LANES), F32)],
        compiler_params=_cparams(("arbitrary",)),
        name="in_proj",
    )(x2, wm, wkt, ws, wst, fbc, fbr)


def _ssd_kernel(xbc_ref, z_ref, dte_ref, rows_ref, cw_ref, cb_ref, dtb_ref, alog_ref, dsk_ref, nw_ref,
                dtbr_ref, alogr_ref, y_ref, tail, ubuf, xc, state, *, chunk):
    t = pl.program_id(1)
    tb = xbc_ref.shape[0]
    L = chunk
    CONV_PIECE = 64
    HALO = 8

    @pl.when(t == 0)
    def _():
        tail[...] = jnp.zeros_like(tail)
        state[...] = jnp.zeros_like(state)

    ubuf[0:HALO, :] = tail[...]
    ubuf[HALO:HALO + tb, :] = xbc_ref[...]
    tail[...] = xbc_ref[tb - HALO:tb, :]
    for r0 in range(0, tb, CONV_PIECE):
        acc = jnp.broadcast_to(cb_ref[...], (CONV_PIECE, SSD_CONV_CH))
        for k in range(SSD_CONV):
            off = r0 + HALO - (SSD_CONV - 1) + k
            acc = acc + cw_ref[k:k + 1, :] * ubuf[off:off + CONV_PIECE, :]
        xc[r0:r0 + CONV_PIECE, :] = _silu(acc)

    a_e = -jnp.exp(alog_ref[...])
    a_r = -jnp.exp(alogr_ref[...])
    tri = _tri_ones(L, lower=True)
    upp = _tri_ones(L, lower=False)
    rr = lax.broadcasted_iota(jnp.int32, (L, L), 0)
    cc = lax.broadcasted_iota(jnp.int32, (L, L), 1)
    causal = cc <= rr
    lane = lax.broadcasted_iota(jnp.int32, (L, LANES), 1)
    left = lane < SSD_HEAD_DIM
    HG = SSD_HEADS // SSD_GROUPS
    GW = SSD_WIDTH // SSD_GROUPS

    def body(c, carry):
        r0 = pl.multiple_of(c * L, L)
        rows = pl.ds(r0, L)
        xs = xc[rows, 0:SSD_WIDTH]
        dt_e = _softplus(dte_ref[rows, :] + dtb_ref[...])
        a3 = _split3(dt_e * a_e)
        acs_e = _dot(tri, a3[0]) + _dot(tri, a3[1]) + _dot(tri, a3[2])
        dt_r = _softplus(rows_ref[:, rows] + dtbr_ref[...])
        ar3 = _split3(dt_r * a_r)
        acs_r = _dot(ar3[0], upp) + _dot(ar3[1], upp) + _dot(ar3[2], upp)
        total = acs_e[L - 1:L, :]
        x_dt = xs * dt_e
        xb = x_dt.astype(BF16)
        xd = (x_dt * jnp.exp(total - acs_e)).astype(BF16)
        eacs = jnp.exp(acs_e)
        y_parts = []
        for g in range(SSD_GROUPS):
            bm = xc[rows, SSD_WIDTH + g * SSD_STATE:SSD_WIDTH + (g + 1) * SSD_STATE].astype(BF16)
            cm = xc[rows, SSD_WIDTH + (SSD_GROUPS + g) * SSD_STATE:
                    SSD_WIDTH + (SSD_GROUPS + g + 1) * SSD_STATE].astype(BF16)
            cb = _dot_nt(cm, bm)
            for pair in range(HG // 2):
                lo = g * GW + pair * LANES
                x_pair = xb[:, lo:lo + LANES]
                outs = []
                for sub in range(2):
                    h = g * HG + pair * 2 + sub
                    seg = acs_e[:, h * SSD_HEAD_DIM:h * SSD_HEAD_DIM + 1] - acs_r[h:h + 1, :]
                    decay = jnp.exp(jnp.where(causal, seg, -jnp.inf))
                    outs.append(_dot((cb * decay).astype(BF16), x_pair))
                y_parts.append(jnp.where(left, outs[0], outs[1]))
            s_prev = state[g]
            y_off = _dot(cm, s_prev.astype(BF16)) * eacs[:, g * GW:(g + 1) * GW]
            y_parts[-2] = y_parts[-2] + y_off[:, 0:LANES]
            y_parts[-1] = y_parts[-1] + y_off[:, LANES:2 * LANES]
            state[g] = s_prev * jnp.exp(total[:, g * GW:(g + 1) * GW]) + _dot_tn(bm, xd[:, g * GW:(g + 1) * GW])
        y = jnp.concatenate(y_parts, axis=1) + xs * dsk_ref[...]
        y = y * _silu(z_ref[rows, :])
        outs = []
        for g in range(SSD_GROUPS):
            yg = y[:, g * GW:(g + 1) * GW]
            ms = jnp.mean(yg * yg, axis=-1, keepdims=True)
            outs.append(yg * lax.rsqrt(ms + RMS_EPS))
        y_ref[rows, :] = (jnp.concatenate(outs, axis=1) * nw_ref[...]).astype(y_ref.dtype)
        return carry

    lax.fori_loop(0, tb // L, body, 0)


def _ssd(xbc, z, dte, rowsp, cw, cb, dtb_e, alog_e, dsk_e, nw, dtb_r, alog_r, *, batch, seq_len, tb, chunk):
    n = xbc.shape[0]
    tpb = seq_len // tb
    full = lambda a: pl.BlockSpec(a.shape, lambda b, t: (0,) * a.ndim)
    rowblk = lambda w: pl.BlockSpec((tb, w), lambda b, t: (b * tpb + t, 0))
    return pl.pallas_call(
        functools.partial(_ssd_kernel, chunk=chunk),
        out_shape=jax.ShapeDtypeStruct((n, SSD_WIDTH), BF16),
        grid=(batch, tpb),
        in_specs=[rowblk(SSD_CONV_CH), rowblk(SSD_WIDTH), rowblk(SSD_WIDTH),
                  pl.BlockSpec((16, tb), lambda b, t: (0, b * tpb + t)),
                  full(cw), full(cb), full(dtb_e), full(alog_e), full(dsk_e), full(nw),
                  full(dtb_r), full(alog_r)],
        out_specs=rowblk(SSD_WIDTH),
        scratch_shapes=[pltpu.VMEM((8, SSD_CONV_CH), F32),
                        pltpu.VMEM((tb + 8, SSD_CONV_CH), F32),
                        pltpu.VMEM((tb, SSD_CONV_CH), F32),
                        pltpu.VMEM((SSD_GROUPS, SSD_STATE, SSD_WIDTH // SSD_GROUPS), F32)],
        compiler_params=_cparams(("parallel", "arbitrary")),
        name="ssd_mixer",
    )(xbc, z, dte, rowsp, cw, cb, dtb_e, alog_e, dsk_e, nw, dtb_r, alog_r)


def _fox_kernel(q_ref, kt_ref, v_ref, cq_ref, ck_ref, o_ref, m_sc, l_sc, acc_sc):
    i = pl.program_id(1)
    j = pl.program_id(2)
    tq = q_ref.shape[0]
    tk = v_ref.shape[0]
    FROW = SSD_HEADS

    @pl.when(j == 0)
    def _():
        m_sc[...] = jnp.full_like(m_sc, -jnp.inf)
        l_sc[...] = jnp.zeros_like(l_sc)
        acc_sc[...] = jnp.zeros_like(acc_sc)

    def step(masked):
        if masked:
            rr = lax.broadcasted_iota(jnp.int32, (tq, tk), 0)
            cc = lax.broadcasted_iota(jnp.int32, (tq, tk), 1)
            keep = cc <= rr
        for h in range(FOX_HEADS):
            q_h = q_ref[:, h * FOX_HEAD_DIM:(h + 1) * FOX_HEAD_DIM]
            kt_h = kt_ref[h * FOX_HEAD_DIM:(h + 1) * FOX_HEAD_DIM, :]
            s = _dot(q_h, kt_h)
            s = s + (cq_ref[:, FROW + h:FROW + h + 1] - ck_ref[FROW + h:FROW + h + 1, :])
            if masked:
                s = jnp.where(keep, s, -jnp.inf)
            m_prev = m_sc[h]
            m_new = jnp.maximum(m_prev, jnp.max(s, axis=-1, keepdims=True))
            alpha = jnp.exp(m_prev - m_new)
            p = jnp.exp(s - m_new)
            l_sc[h] = alpha * l_sc[h] + jnp.sum(p, axis=-1, keepdims=True)
            v_pair = v_ref[:, (h // 2) * LANES:(h // 2 + 1) * LANES]
            acc_sc[h] = alpha * acc_sc[h] + _dot(p.astype(BF16), v_pair)
            m_sc[h] = m_new

    @pl.when(j < i)
    def _():
        step(False)

    @pl.when(j == i)
    def _():
        step(True)
        lane = lax.broadcasted_iota(jnp.int32, (tq, LANES), 1)
        left = lane < FOX_HEAD_DIM
        for pair in range(FOX_HEADS // 2):
            even = acc_sc[2 * pair] / l_sc[2 * pair]
            odd = acc_sc[2 * pair + 1] / l_sc[2 * pair + 1]
            o_ref[:, pair * LANES:(pair + 1) * LANES] = jnp.where(left, even, odd).astype(o_ref.dtype)


def _fox(q, kt, v, ccol, rowsp, *, batch, seq_len, tq, tk):
    assert tq == tk
    n = q.shape[0]
    nq = seq_len // tq
    qrow = lambda w: pl.BlockSpec((tq, w), lambda b, i, j: (b * nq + i, 0))
    return pl.pallas_call(
        _fox_kernel,
        out_shape=jax.ShapeDtypeStruct((n, FOX_WIDTH), BF16),
        grid=(batch, nq, nq),
        in_specs=[qrow(FOX_WIDTH),
                  pl.BlockSpec((FOX_WIDTH, tk), lambda b, i, j: (0, b * nq + jnp.minimum(j, i))),
                  pl.BlockSpec((tk, FOX_WIDTH), lambda b, i, j: (b * nq + jnp.minimum(j, i), 0)),
                  qrow(LANES),
                  pl.BlockSpec((16, tk), lambda b, i, j: (0, b * nq + jnp.minimum(j, i)))],
        out_specs=qrow(FOX_WIDTH),
        scratch_shapes=[pltpu.VMEM((FOX_HEADS, tq, 1), F32),
                        pltpu.VMEM((FOX_HEADS, tq, 1), F32),
                        pltpu.VMEM((FOX_HEADS, tq, LANES), F32)],
        compiler_params=_cparams(("parallel", "parallel", "arbitrary")),
        name="fox_attention",
    )(q, kt, v, ccol, rowsp)


POOL_HALO = 16


def _pool_kernel(u_ref, w_ref, b_ref, sc_ref, y_ref, tail, ubuf):
    t = pl.program_id(1)
    tb = u_ref.shape[0]
    P = POOL_PIECE

    @pl.when(t == 0)
    def _():
        tail[...] = jnp.zeros_like(tail)

    ubuf[0:POOL_HALO, :] = tail[...]
    ubuf[POOL_HALO:POOL_HALO + tb, :] = u_ref[...]
    tail[...] = u_ref[tb - POOL_HALO:tb, :]
    group = lax.broadcasted_iota(jnp.int32, (P, POOL_WIDTH), 1) // POOL_GROUP_DIM
    for p0 in range(0, tb, P):
        w1 = ubuf[p0:p0 + P + POOL_HALO, :]
        n1 = P + POOL_HALO
        a2 = w1[1:n1] + w1[0:n1 - 1]
        a4 = a2[2:n1 - 1] + a2[0:n1 - 3]
        a8 = a4[4:n1 - 3] + a4[0:n1 - 7]
        a16 = a8[8:n1 - 7] + a8[0:n1 - 15]
        sums = (a2[15:15 + P], a4[13:13 + P], a8[9:9 + P], a16[1:1 + P])
        tpos = t * tb + p0 + lax.broadcasted_iota(jnp.int32, (P, 1), 0) + 1
        pooled = jnp.zeros((P, POOL_WIDTH), F32)
        for g, win in enumerate(POOL_WINDOWS):
            cnt = jnp.minimum(tpos, win).astype(F32)
            pooled = jnp.where(group == g, sums[g] / cnt, pooled)
        pooled = pooled - w1[POOL_HALO:POOL_HALO + P]
        y = _dot(pooled.astype(BF16), w_ref[...]) + b_ref[...]
        y_ref[p0:p0 + P, :] = (y * sc_ref[...]).astype(y_ref.dtype)


def _pool(u, wbd, b, sc, *, batch, seq_len, tb):
    n = u.shape[0]
    tpb = seq_len // tb
    full = lambda a: pl.BlockSpec(a.shape, lambda bb, t: (0,) * a.ndim)
    rowblk = pl.BlockSpec((tb, POOL_WIDTH), lambda bb, t: (bb * tpb + t, 0))
    return pl.pallas_call(
        _pool_kernel,
        out_shape=jax.ShapeDtypeStruct((n, POOL_WIDTH), BF16),
        grid=(batch, tpb),
        in_specs=[rowblk, full(wbd), full(b), full(sc)],
        out_specs=rowblk,
        scratch_shapes=[pltpu.VMEM((POOL_HALO, POOL_WIDTH), F32),
                        pltpu.VMEM((tb + POOL_HALO, POOL_WIDTH), F32)],
        compiler_params=_cparams(("parallel", "arbitrary")),
        name="pool_mixer",
    )(u, wbd, b, sc)


def _out_proj_kernel(x_ref, ys_ref, yf_ref, yp_ref, w_ref, g_ref, b_ref, o_ref):
    mix = (_dot(ys_ref[...], w_ref[0:512, :]) + _dot(yf_ref[...], w_ref[512:768, :])
           + _dot(yp_ref[...], w_ref[768:1024, :]))
    o_ref[...] = _layer_norm(DN_ALPHA * x_ref[...] + mix, g_ref[...], b_ref[...])


def _out_proj(x2, ys, yf, yp, w, g, b, *, tm):
    n = x2.shape[0]
    full = lambda a: pl.BlockSpec(a.shape, lambda i: (0,) * a.ndim)
    rowblk = lambda wd: pl.BlockSpec((tm, wd), lambda i: (i, 0))
    return pl.pallas_call(
        _out_proj_kernel,
        out_shape=jax.ShapeDtypeStruct((n, D_MODEL), F32),
        grid=(n // tm,),
        in_specs=[rowblk(D_MODEL), rowblk(512), rowblk(256), rowblk(256), full(w), full(g), full(b)],
        out_specs=rowblk(D_MODEL),
        compiler_params=_cparams(("parallel",)),
        name="out_proj_ln",
    )(x2, ys, yf, yp, w, g, b)


def _kv_proj_kernel(m_ref, w_ref, o_ref):
    o_ref[...] = _dot(m_ref[...].astype(BF16), w_ref[...]).astype(o_ref.dtype)


def _kv_proj(mem2, wkv):
    m = mem2.shape[0]
    return pl.pallas_call(
        _kv_proj_kernel,
        out_shape=jax.ShapeDtypeStruct((m, wkv.shape[1]), BF16),
        grid=(1,),
        in_specs=[pl.BlockSpec(mem2.shape, lambda i: (0, 0)), pl.BlockSpec(wkv.shape, lambda i: (0, 0))],
        out_specs=pl.BlockSpec((m, wkv.shape[1]), lambda i: (0, 0)),
        compiler_params=_cparams(("arbitrary",)),
        name="xattn_kv_proj",
    )(mem2, wkv)


def _xattn_kernel(x_ref, kv_ref, wq_ref, wo_ref, g_ref, b_ref, o_ref):
    x = x_ref[...]
    q = _dot(x.astype(BF16), wq_ref[...]).astype(BF16)
    heads = []
    for h in range(XATTN_HEADS):
        lo = h * XATTN_HEAD_DIM
        k_h = kv_ref[:, lo:lo + XATTN_HEAD_DIM]
        v_h = kv_ref[:, D_MODEL + lo:D_MODEL + lo + XATTN_HEAD_DIM]
        s = _dot_nt(q[:, lo:lo + XATTN_HEAD_DIM], k_h)
        p = jnp.exp(s - jnp.max(s, axis=-1, keepdims=True))
        l = jnp.sum(p, axis=-1, keepdims=True)
        heads.append((_dot(p.astype(BF16), v_h) / l).astype(BF16))
    o = jnp.concatenate(heads, axis=1)
    xa = _dot(o, wo_ref[...])
    o_ref[...] = _layer_norm(DN_ALPHA * x + xa, g_ref[...], b_ref[...])


def _xattn(x2, kv, wq, wo, g, b, *, batch, seq_len, mem_len, tm):
    n = x2.shape[0]
    tpb = seq_len // tm
    full = lambda a: pl.BlockSpec(a.shape, lambda bb, t: (0,) * a.ndim)
    rowblk = pl.BlockSpec((tm, D_MODEL), lambda bb, t: (bb * tpb + t, 0))
    return pl.pallas_call(
        _xattn_kernel,
        out_shape=jax.ShapeDtypeStruct((n, D_MODEL), F32),
        grid=(batch, tpb),
        in_specs=[rowblk, pl.BlockSpec((mem_len, 2 * D_MODEL), lambda bb, t: (bb, 0)),
                  full(wq), full(wo), full(g), full(b)],
        out_specs=rowblk,
        compiler_params=_cparams(("parallel", "parallel")),
        name="xattn_ln",
    )(x2, kv, wq, wo, g, b)


ROW_SLABS = D_MODEL // LANES


def _ffn_kernel(texp_ref, nvalid_ref, x_ref, w1_ref, w3_ref, w2_ref, g_ref, b_ref, o_ref, xb_sc, acc_sc,
                *, grouped):
    t = pl.program_id(0)
    f = pl.program_id(1)
    nf = pl.num_programs(1)

    @pl.when(t < nvalid_ref[0])
    def _():
        @pl.when(f == 0)
        def _():
            if grouped:
                for s in range(ROW_SLABS):
                    xb_sc[:, s * LANES:(s + 1) * LANES] = x_ref[:, s, :].astype(BF16)
            else:
                xb_sc[...] = x_ref[...].astype(BF16)
            acc_sc[...] = jnp.zeros_like(acc_sc)

        xb = xb_sc[...]
        h1 = _dot(xb, w1_ref[0])
        h3 = _dot(xb, w3_ref[0])
        acc_sc[...] += _dot((_silu(h1) * h3).astype(BF16), w2_ref[0])

        @pl.when(f == nf - 1)
        def _():
            if grouped:
                for s in range(ROW_SLABS):
                    o_ref[:, s, :] = acc_sc[:, s * LANES:(s + 1) * LANES]
            else:
                o_ref[...] = _layer_norm(DN_ALPHA * x_ref[...] + acc_sc[...], g_ref[...], b_ref[...])

    @pl.when((t >= nvalid_ref[0]) & (f == 0))
    def _():
        o_ref[...] = jnp.zeros_like(o_ref)


def _ffn(tile_expert, nvalid, x, w1, w3, w2, g, b, *, tm, tf, grouped):
    n = x.shape[0]
    nt = n // tm
    nf = D_FF // tf

    def tile_of(t, nv):
        return jnp.minimum(t, nv[0] - 1)

    def f_of(t, f, nv):
        return jnp.where(t < nv[0], f, nf - 1)

    if grouped:
        rowblk = pl.BlockSpec((tm, ROW_SLABS, LANES), lambda t, f, te, nv: (tile_of(t, nv), 0, 0))
        outblk = pl.BlockSpec((tm, ROW_SLABS, LANES), lambda t, f, te, nv: (t, 0, 0))
        out_shape = jax.ShapeDtypeStruct((n, ROW_SLABS, LANES), F32)
    else:
        rowblk = pl.BlockSpec((tm, D_MODEL), lambda t, f, te, nv: (tile_of(t, nv), 0))
        outblk = pl.BlockSpec((tm, D_MODEL), lambda t, f, te, nv: (t, 0))
        out_shape = jax.ShapeDtypeStruct((n, D_MODEL), F32)
    w13 = pl.BlockSpec((1, D_MODEL, tf), lambda t, f, te, nv: (te[tile_of(t, nv)], 0, f_of(t, f, nv)))
    w2s = pl.BlockSpec((1, tf, D_MODEL), lambda t, f, te, nv: (te[tile_of(t, nv)], f_of(t, f, nv), 0))
    vec = pl.BlockSpec((1, D_MODEL), lambda t, f, te, nv: (0, 0))
    return pl.pallas_call(
        functools.partial(_ffn_kernel, grouped=grouped),
        out_shape=out_shape,
        grid_spec=pltpu.PrefetchScalarGridSpec(
            num_scalar_prefetch=2,
            grid=(nt, nf),
            in_specs=[rowblk, w13, w13, w2s, vec, vec],
            out_specs=outblk,
            scratch_shapes=[pltpu.VMEM((tm, D_MODEL), BF16), pltpu.VMEM((tm, D_MODEL), F32)]),
        compiler_params=_cparams(("arbitrary", "arbitrary"), vmem_mib=56),
        name="swiglu_grouped" if grouped else "swiglu_ln",
    )(tile_expert, nvalid, x, w1, w3, w2, g, b)


def _router_kernel(x_ref, w_ref, route_ref, counts_ref, carry):
    i = pl.program_id(0)
    tm = x_ref.shape[0]

    @pl.when(i == 0)
    def _():
        carry[...] = jnp.zeros_like(carry)

    xh, xm, _ = _split3(x_ref[...])
    wh, wm, _ = _split3(w_ref[...])
    logits = _dot(xh, wh) + (_dot(xh, wm) + _dot(xm, wh))
    lane = lax.broadcasted_iota(jnp.int32, (tm, LANES), 1)
    logits = jnp.where(lane < N_EXPERTS, logits, -jnp.inf)
    m1 = jnp.max(logits, axis=-1, keepdims=True)
    i1 = jnp.min(jnp.where(logits == m1, lane, LANES), axis=-1, keepdims=True)
    rest = jnp.where(lane == i1, -jnp.inf, logits)
    m2 = jnp.max(rest, axis=-1, keepdims=True)
    i2 = jnp.min(jnp.where(rest == m2, lane, LANES), axis=-1, keepdims=True)
    e21 = jnp.exp(m2 - m1)
    g1 = 1.0 / (1.0 + e21)
    g2 = e21 / (1.0 + e21)
    hit1 = lane == i1
    hit2 = lane == i2
    onehot = jnp.where(hit1 | hit2, 1.0, 0.0).astype(BF16)
    r = lax.broadcasted_iota(jnp.int32, (tm, tm), 0)
    c = lax.broadcasted_iota(jnp.int32, (tm, tm), 1)
    strict = jnp.where(c < r, 1.0, 0.0).astype(BF16)
    before = _dot(strict, onehot) + carry[...]
    rank1 = jnp.sum(jnp.where(hit1, before, 0.0), axis=-1, keepdims=True)
    rank2 = jnp.sum(jnp.where(hit2, before, 0.0), axis=-1, keepdims=True)
    carry[...] = carry[...] + jnp.sum(onehot.astype(F32), axis=0, keepdims=True)
    out = jnp.where(lane == 0, i1.astype(F32), 0.0)
    out = jnp.where(lane == 1, i2.astype(F32), out)
    out = jnp.where(lane == 2, g1, out)
    out = jnp.where(lane == 3, g2, out)
    out = jnp.where(lane == 4, rank1, out)
    out = jnp.where(lane == 5, rank2, out)
    route_ref[...] = out
    counts_ref[...] = carry[...]


def _router(x2, wr, *, tm):
    n = x2.shape[0]
    return pl.pallas_call(
        _router_kernel,
        out_shape=(jax.ShapeDtypeStruct((n, LANES), F32), jax.ShapeDtypeStruct((1, LANES), F32)),
        grid=(n // tm,),
        in_specs=[pl.BlockSpec((tm, D_MODEL), lambda i: (i, 0)), pl.BlockSpec(wr.shape, lambda i: (0, 0))],
        out_specs=(pl.BlockSpec((tm, LANES), lambda i: (i, 0)), pl.BlockSpec((1, LANES), lambda i: (0, 0))),
        scratch_shapes=[pltpu.VMEM((1, LANES), F32)],
        compiler_params=_cparams(("arbitrary",)),
        name="moe_router",
    )(x2, wr)


POS_ROWS = 8


def _pos_entry(pos_smem, r, k):
    idx = TOP_K * r + k
    return pos_smem[idx // LANES, idx % LANES]


def _dispatch_kernel(pend_ref, padded_ref, pos_hbm, x_ref, xs_hbm, pos_smem, stage, zeros_vmem, sem_pos, sem_rows,
                     *, row_tm, pad_tm):
    i = pl.program_id(0)

    def zero_copy(start):
        return pltpu.make_async_copy(zeros_vmem, xs_hbm.at[pl.ds(start, pad_tm)], sem_rows)

    @pl.when(i == 0)
    def _():
        zeros_vmem[...] = jnp.zeros_like(zeros_vmem)
        used = pend_ref[N_EXPERTS - 1]
        fills = [(padded_ref[e] > 0, pend_ref[e] - pad_tm) for e in range(N_EXPERTS)]
        fills += [(used + e * pad_tm < xs_hbm.shape[0], used + e * pad_tm) for e in range(N_EXPERTS)]
        for cond, start in fills:
            @pl.when(cond)
            def _():
                zero_copy(start).start()
        for cond, start in fills:
            @pl.when(cond)
            def _():
                zero_copy(start).wait()

    pos_copy = pltpu.make_async_copy(pos_hbm.at[i], pos_smem, sem_pos)
    pos_copy.start()
    for s in range(ROW_SLABS):
        stage[:, s, :] = x_ref[:, s * LANES:(s + 1) * LANES]
    pos_copy.wait()

    def row_copy(r, k):
        return pltpu.make_async_copy(stage.at[r], xs_hbm.at[_pos_entry(pos_smem, r, k)], sem_rows)

    def start(r, carry):
        for k in range(TOP_K):
            row_copy(r, k).start()
        return carry

    def wait(r, carry):
        for k in range(TOP_K):
            row_copy(r, k).wait()
        return carry

    lax.fori_loop(0, row_tm, start, 0)
    lax.fori_loop(0, row_tm, wait, 0)


def _dispatch(pend, padded, pos3, x2, *, cap, row_tm, pad_tm):
    n = x2.shape[0]
    assert TOP_K * row_tm == POS_ROWS * LANES
    return pl.pallas_call(
        functools.partial(_dispatch_kernel, row_tm=row_tm, pad_tm=pad_tm),
        out_shape=jax.ShapeDtypeStruct((cap, ROW_SLABS, LANES), F32),
        grid_spec=pltpu.PrefetchScalarGridSpec(
            num_scalar_prefetch=2,
            grid=(n // row_tm,),
            in_specs=[pl.BlockSpec(memory_space=pl.ANY),
                      pl.BlockSpec((row_tm, D_MODEL), lambda i, pe, pa: (i, 0))],
            out_specs=pl.BlockSpec(memory_space=pl.ANY),
            scratch_shapes=[pltpu.SMEM((POS_ROWS, LANES), jnp.int32),
                            pltpu.VMEM((row_tm, ROW_SLABS, LANES), F32),
                            pltpu.VMEM((pad_tm, ROW_SLABS, LANES), F32),
                            pltpu.SemaphoreType.DMA(()),
                            pltpu.SemaphoreType.DMA(())]),
        compiler_params=_cparams(("arbitrary",)),
        name="moe_dispatch",
    )(pend, padded, pos3, x2)


def _combine_kernel(pos_hbm, x_ref, route_ref, ys_hbm, g_ref, b_ref, o_ref, pos_smem, rows0, rows1, ff_sc,
                    sem_pos, sem_rows, *, row_tm):
    i = pl.program_id(0)
    pos_copy = pltpu.make_async_copy(pos_hbm.at[i], pos_smem, sem_pos)
    pos_copy.start()
    pos_copy.wait()
    rows = (rows0, rows1)

    def row_copy(r, k):
        return pltpu.make_async_copy(ys_hbm.at[_pos_entry(pos_smem, r, k)], rows[k].at[r], sem_rows)

    def start(r, carry):
        for k in range(TOP_K):
            row_copy(r, k).start()
        return carry

    def wait(r, carry):
        for k in range(TOP_K):
            row_copy(r, k).wait()
        return carry

    lax.fori_loop(0, row_tm, start, 0)
    lax.fori_loop(0, row_tm, wait, 0)
    g1 = route_ref[:, 2:3]
    g2 = route_ref[:, 3:4]
    for s in range(ROW_SLABS):
        ff_sc[:, s * LANES:(s + 1) * LANES] = rows0[:, s, :] * g1 + rows1[:, s, :] * g2
    o_ref[...] = _layer_norm(DN_ALPHA * x_ref[...] + ff_sc[...], g_ref[...], b_ref[...])


def _combine(pos3, x2, route, ys, g, b, *, row_tm):
    n = x2.shape[0]
    assert TOP_K * row_tm == POS_ROWS * LANES
    rowblk = lambda w: pl.BlockSpec((row_tm, w), lambda i: (i, 0))
    vec = pl.BlockSpec((1, D_MODEL), lambda i: (0, 0))
    return pl.pallas_call(
        functools.partial(_combine_kernel, row_tm=row_tm),
        out_shape=jax.ShapeDtypeStruct((n, D_MODEL), F32),
        grid=(n // row_tm,),
        in_specs=[pl.BlockSpec(memory_space=pl.ANY), rowblk(D_MODEL), rowblk(LANES),
                  pl.BlockSpec(memory_space=pl.ANY), vec, vec],
        out_specs=rowblk(D_MODEL),
        scratch_shapes=[pltpu.SMEM((POS_ROWS, LANES), jnp.int32),
                        pltpu.VMEM((row_tm, ROW_SLABS, LANES), F32),
                        pltpu.VMEM((row_tm, ROW_SLABS, LANES), F32),
                        pltpu.VMEM((row_tm, D_MODEL), F32),
                        pltpu.SemaphoreType.DMA(()),
                        pltpu.SemaphoreType.DMA(())],
        compiler_params=_cparams(("arbitrary",)),
        name="moe_combine_ln",
    )(pos3, x2, route, ys, g, b)


def _row(v):
    return v.reshape(1, -1).astype(F32)


def _repeat_heads(v, width):
    return jnp.repeat(v.astype(F32), width).reshape(1, -1)


def _pad_rows16(v8, offset):
    out = jnp.zeros((16,), F32).at[offset:offset + v8.shape[0]].set(v8.astype(F32))
    return out.reshape(16, 1)


def _mixer_weights(w_in_l):
    offs = [0]
    for s in IN_SIZES:
        offs.append(offs[-1] + s)
    seg = lambda k: w_in_l[:, offs[k]:offs[k + 1]]
    wz, wxbc, wdt, wq, wk, wv, wf, wp = (seg(k) for k in range(8))
    wq = wq * (FOX_HEAD_DIM ** -0.5)
    wm = jnp.concatenate([wxbc, wz, wq, wv, wp, jnp.repeat(wdt, SSD_HEAD_DIM, axis=1)], axis=1).astype(BF16)
    ws = jnp.concatenate([wdt, wf, jnp.zeros((D_MODEL, LANES - 12), F32)], axis=1).astype(BF16)
    wst = ws[:, 0:16].T
    return wm, wk.T.astype(BF16), ws, wst


def _token_mixing(x2, p, layer, *, batch, seq_len):
    wm, wkt, ws, wst = _mixer_weights(p["w_in"][layer])
    fb = p["fox_f_bias"][layer].astype(F32)
    fbc = jnp.zeros((LANES,), F32).at[SSD_HEADS:SSD_HEADS + FOX_HEADS].set(fb).reshape(1, LANES)
    fbr = _pad_rows16(fb, SSD_HEADS)
    xbc, z, q, v, pool_in, dte, kt, ccol, rowsp = _in_proj(
        x2, wm, wkt, ws, wst, fbc, fbr, seq_len=seq_len, tm=min(IN_TM, seq_len))

    y_ssd = _ssd(xbc, z, dte, rowsp,
                 p["ssm_conv_w"][layer].astype(F32), _row(p["ssm_conv_b"][layer]),
                 _repeat_heads(p["ssm_dt_bias"][layer], SSD_HEAD_DIM),
                 _repeat_heads(p["ssm_a_log"][layer], SSD_HEAD_DIM),
                 _repeat_heads(p["ssm_d"][layer], SSD_HEAD_DIM),
                 _row(p["ssm_norm_w"][layer]),
                 _pad_rows16(p["ssm_dt_bias"][layer], 0), _pad_rows16(p["ssm_a_log"][layer], 0),
                 batch=batch, seq_len=seq_len, tb=min(SSD_TB, seq_len), chunk=SSD_L)

    tq = min(FOX_TQ, seq_len)
    y_fox = _fox(q, kt, v, ccol, rowsp, batch=batch, seq_len=seq_len, tq=tq, tk=tq)

    wbd = jax.scipy.linalg.block_diag(*[p["pool_w"][layer][g] for g in range(len(POOL_WINDOWS))]).astype(BF16)
    y_pool = _pool(pool_in, wbd, _row(p["pool_b"][layer]), _row(p["pool_scale"][layer]),
                   batch=batch, seq_len=seq_len, tb=min(POOL_TB, seq_len))

    return _out_proj(x2, y_ssd, y_fox, y_pool, p["w_out"][layer].astype(BF16),
                     _row(p["ln1_g"][layer]), _row(p["ln1_b"][layer]), tm=min(OUT_TM, seq_len))


def _cross_attention(x2, mem2, p, layer, *, batch, seq_len, mem_len):
    wkv = jnp.concatenate([p["xa_wk"][layer], p["xa_wv"][layer]], axis=1).astype(BF16)
    kv = _kv_proj(mem2, wkv)
    wq = (p["xa_wq"][layer] * (XATTN_HEAD_DIM ** -0.5)).astype(BF16)
    return _xattn(x2, kv, wq, p["xa_wo"][layer].astype(BF16), _row(p["ln2_g"][layer]), _row(p["ln2_b"][layer]),
                  batch=batch, seq_len=seq_len, mem_len=mem_len, tm=min(XA_TM, seq_len))


def _dense_ffn(x2, p, layer):
    j = layer // 2
    n = x2.shape[0]
    tm = min(FFN_TM, n)
    tile_expert = jnp.zeros((n // tm,), jnp.int32)
    nvalid = jnp.full((1,), n // tm, jnp.int32)
    return _ffn(tile_expert, nvalid, x2, p["ffn_w1"][j][None].astype(BF16), p["ffn_w3"][j][None].astype(BF16),
                p["ffn_w2"][j][None].astype(BF16), _row(p["ln3_g"][layer]), _row(p["ln3_b"][layer]),
                tm=tm, tf=FFN_TF, grouped=False)


def _moe_ffn(x2, p, layer):
    j = layer // 2
    n = x2.shape[0]
    tm = min(MOE_TM, n)
    row_tm = min(ROW_TM, n)
    wr = jnp.concatenate([p["router_w"][j].astype(F32), jnp.zeros((D_MODEL, LANES - N_EXPERTS), F32)], axis=1)
    route, counts = _router(x2, wr, tm=min(ROUTE_TM, n))

    counts = counts[0, :N_EXPERTS].astype(jnp.int32)
    padded = (counts + tm - 1) // tm * tm
    pend = jnp.cumsum(padded)
    pstart = pend - padded
    experts = route[:, 0:TOP_K].astype(jnp.int32)
    ranks = route[:, 4:4 + TOP_K].astype(jnp.int32)
    pos = (pstart[experts] + ranks).astype(jnp.int32)
    pos3 = pos.reshape(n // row_tm, POS_ROWS, LANES)
    cap = n * TOP_K + N_EXPERTS * tm
    ntiles = cap // tm
    tile_expert = jnp.minimum(jnp.searchsorted(pend, jnp.arange(ntiles) * tm, side="right"),
                              N_EXPERTS - 1).astype(jnp.int32)
    nvalid = (pend[-1:] // tm).astype(jnp.int32)

    xs = _dispatch(pend.astype(jnp.int32), padded.astype(jnp.int32), pos3, x2, cap=cap, row_tm=row_tm, pad_tm=tm)
    ys = _ffn(tile_expert, nvalid, xs, p["moe_w1"][j].astype(BF16), p["moe_w3"][j].astype(BF16),
              p["moe_w2"][j].astype(BF16), _row(p["ln3_g"][layer]), _row(p["ln3_b"][layer]),
              tm=tm, tf=FFN_TF, grouped=True)
    return _combine(pos3, x2, route, ys, _row(p["ln3_g"][layer]), _row(p["ln3_b"][layer]), row_tm=row_tm)


def _forward(x, mem, p):
    batch, seq_len, _ = x.shape
    mem_len = mem.shape[1]
    x2 = x.reshape(batch * seq_len, D_MODEL).astype(F32)
    mem2 = mem.reshape(batch * mem_len, D_MODEL).astype(F32)
    for layer in range(DEPTH):
        x2 = _token_mixing(x2, p, layer, batch=batch, seq_len=seq_len)
        x2 = _cross_attention(x2, mem2, p, layer, batch=batch, seq_len=seq_len, mem_len=mem_len)
        x2 = _dense_ffn(x2, p, layer) if layer % 2 == 0 else _moe_ffn(x2, p, layer)
    return x2.reshape(batch, seq_len, D_MODEL)


def kernel(x, mem, w_in, ssm_conv_w, ssm_conv_b, ssm_dt_bias, ssm_a_log, ssm_d, ssm_norm_w, fox_f_bias, pool_w, pool_b, pool_scale, w_out, ln1_g, ln1_b, xa_wq, xa_wk, xa_wv, xa_wo, ln2_g, ln2_b, ffn_w1, ffn_w3, ffn_w2, router_w, moe_w1, moe_w3, moe_w2, ln3_g, ln3_b):
    p = dict(w_in=w_in, ssm_conv_w=ssm_conv_w, ssm_conv_b=ssm_conv_b, ssm_dt_bias=ssm_dt_bias,
             ssm_a_log=ssm_a_log, ssm_d=ssm_d, ssm_norm_w=ssm_norm_w, fox_f_bias=fox_f_bias, pool_w=pool_w,
             pool_b=pool_b, pool_scale=pool_scale, w_out=w_out, ln1_g=ln1_g, ln1_b=ln1_b, xa_wq=xa_wq,
             xa_wk=xa_wk, xa_wv=xa_wv, xa_wo=xa_wo, ln2_g=ln2_g, ln2_b=ln2_b, ffn_w1=ffn_w1, ffn_w3=ffn_w3,
             ffn_w2=ffn_w2, router_w=router_w, moe_w1=moe_w1, moe_w3=moe_w3, moe_w2=moe_w2, ln3_g=ln3_g,
             ln3_b=ln3_b)
    return _forward(x, mem, p)
```

```python
import functools

import jax
import jax.numpy as jnp
from jax import lax
from jax.experimental import pallas as pl
from jax.experimental.pallas import tpu as pltpu

F32 = jnp.float32
BF16 = jnp.bfloat16

D_MODEL = 1024
DEPTH = 4
SSD_WIDTH = 512
SSD_HEAD_DIM = 64
SSD_HEADS = 8
SSD_GROUPS = 2
SSD_STATE = 128
SSD_CONV = 4
SSD_CONV_CH = 1024
FOX_WIDTH = 256
FOX_HEAD_DIM = 64
FOX_HEADS = 4
POOL_WIDTH = 256
POOL_WINDOWS = (2, 4, 8, 16)
POOL_GROUP_DIM = 64
IN_SIZES = (512, 1024, 8, 256, 256, 256, 4, 256)
XATTN_HEADS = 4
XATTN_HEAD_DIM = 256
D_FF = 3584
N_EXPERTS = 8
TOP_K = 2
DN_ALPHA = (2 * DEPTH) ** 0.25
LN_EPS = 1e-5
RMS_EPS = 1e-5

LANES = 128
MIB = 1024 * 1024

IN_TM = 512
SSD_TB = 512
SSD_L = 128
FOX_TQ = 512
FOX_TK = 512
POOL_TB = 512
POOL_PIECE = 128
OUT_TM = 512
XA_TM = 512
FFN_TM = 1024
FFN_TF = 512
MOE_TM = 512
ROUTE_TM = 512
ROW_TM = 512


def _cparams(sem, vmem_mib=48):
    return pltpu.CompilerParams(dimension_semantics=sem, vmem_limit_bytes=vmem_mib * MIB)


def _dot(a, b):
    return jnp.dot(a, b, preferred_element_type=F32)


def _dot_nt(a, b):
    return lax.dot_general(a, b, (((1,), (1,)), ((), ())), preferred_element_type=F32)


def _dot_tn(a, b):
    return lax.dot_general(a, b, (((0,), (0,)), ((), ())), preferred_element_type=F32)


def _split3(x):
    hi = x.astype(BF16)
    r1 = x - hi.astype(F32)
    mid = r1.astype(BF16)
    lo = (r1 - mid.astype(F32)).astype(BF16)
    return hi, mid, lo


def _silu(x):
    return x / (1.0 + jnp.exp(-x))


def _softplus(x):
    return jnp.maximum(x, 0.0) + jnp.log1p(jnp.exp(-jnp.abs(x)))


def _log_sigmoid(x):
    return jnp.minimum(x, 0.0) - jnp.log1p(jnp.exp(-jnp.abs(x)))


def _layer_norm(v, g, b):
    mu = jnp.mean(v, axis=-1, keepdims=True)
    d = v - mu
    var = jnp.mean(d * d, axis=-1, keepdims=True)
    return d * lax.rsqrt(var + LN_EPS) * g + b


def _tri_ones(n, lower):
    r = lax.broadcasted_iota(jnp.int32, (n, n), 0)
    c = lax.broadcasted_iota(jnp.int32, (n, n), 1)
    m = (c <= r) if lower else (r <= c)
    return jnp.where(m, 1.0, 0.0).astype(BF16)


FOX_SLOT = LANES
FOX_EXT = FOX_HEADS * FOX_SLOT
FOX_BIAS_TERMS = 3
FOX_ONES_ROWS = 16
LOG2E = 1.4426950408889634


def _in_proj_kernel(x_ref, wm_ref, wkx_ref, wqt_ref, wvt_ref, ws_ref, wst_ref, fbc_ref, place_ref, qones_ref,
                    vones_ref, xbc_ref, z_ref, pool_ref, dte_ref, kx_ref, qt_ref, vt_ref, rows_ref,
                    carry_c, *, tiles_per_batch):
    i = pl.program_id(0)
    tm = x_ref.shape[0]
    xb = x_ref[...].astype(BF16)

    xbc_ref[...] = _dot(xb, wm_ref[:, 0:1024])
    z_ref[...] = _dot(xb, wm_ref[:, 1024:1536])
    pool_ref[...] = _dot(xb, wm_ref[:, 1536:1792])
    dte_ref[...] = _dot(xb, wm_ref[:, 1792:2304])
    qt_ref[...] = (_dot_nt(wqt_ref[...], xb) + qones_ref[...]).astype(BF16)
    vt_ref[...] = (_dot_nt(wvt_ref[...], xb) + vones_ref[...]).astype(BF16)
    rows_ref[...] = _dot_nt(wst_ref[...], xb)

    @pl.when(i % tiles_per_batch == 0)
    def _():
        carry_c[...] = jnp.zeros_like(carry_c)

    small_c = _dot(xb, ws_ref[...])
    c3 = _split3(_log_sigmoid(small_c + fbc_ref[...]))
    tri = _tri_ones(tm, lower=True)
    cs_c = _dot(tri, c3[0]) + _dot(tri, c3[1]) + _dot(tri, c3[2]) + carry_c[...]
    carry_c[...] = cs_c[tm - 1:tm, :]
    n3 = _split3(cs_c * (-LOG2E))
    kx = _dot(xb, wkx_ref[...])
    for term in range(FOX_BIAS_TERMS):
        kx = kx + _dot(n3[term], place_ref[term])
    kx_ref[...] = kx.astype(BF16)


def _in_proj(x2, wm, wkx, wqt, wvt, ws, wst, fbc, place, qones, vones, *, seq_len, tm):
    n = x2.shape[0]
    grid = (n // tm,)
    full = lambda a: pl.BlockSpec(a.shape, lambda i: (0,) * a.ndim)
    rowblk = lambda w: pl.BlockSpec((tm, w), lambda i: (i, 0))
    colblk = lambda h: pl.BlockSpec((h, tm), lambda i: (0, i))
    out_shape = (
        jax.ShapeDtypeStruct((n, 1024), F32),
        jax.ShapeDtypeStruct((n, 512), F32),
        jax.ShapeDtypeStruct((n, 256), F32),
        jax.ShapeDtypeStruct((n, 512), F32),
        jax.ShapeDtypeStruct((n, FOX_EXT), BF16),
        jax.ShapeDtypeStruct((FOX_EXT, n), BF16),
        jax.ShapeDtypeStruct((FOX_EXT, n), BF16),
        jax.ShapeDtypeStruct((16, n), F32),
    )
    out_specs = (rowblk(1024), rowblk(512), rowblk(256), rowblk(512), rowblk(FOX_EXT),
                 colblk(FOX_EXT), colblk(FOX_EXT), colblk(16))
    args = (x2, wm, wkx, wqt, wvt, ws, wst, fbc, place, qones, vones)
    return pl.pallas_call(
        functools.partial(_in_proj_kernel, tiles_per_batch=seq_len // tm),
        out_shape=out_shape,
        grid=grid,
        in_specs=[rowblk(D_MODEL)] + [full(a) for a in args[1:]],
        out_specs=out_specs,
        scratch_shapes=[pltpu.VMEM((1, LANES), F32)],
        compiler_params=_cparams(("arbitrary",)),
        name="in_proj",
    )(*args)


def _ssd_kernel(xbc_ref, z_ref, dte_ref, rows_ref, cw_ref, cb_ref, dtb_ref, alog_ref, dsk_ref, nw_ref,
                dtbr_ref, alogr_ref, y_ref, tail, ubuf, xc, state, *, chunk):
    t = pl.program_id(1)
    tb = xbc_ref.shape[0]
    L = chunk
    CONV_PIECE = 64
    HALO = 8

    @pl.when(t == 0)
    def _():
        tail[...] = jnp.zeros_like(tail)
        state[...] = jnp.zeros_like(state)

    ubuf[0:HALO, :] = tail[...]
    ubuf[HALO:HALO + tb, :] = xbc_ref[...]
    tail[...] = xbc_ref[tb - HALO:tb, :]
    for r0 in range(0, tb, CONV_PIECE):
        acc = jnp.broadcast_to(cb_ref[...], (CONV_PIECE, SSD_CONV_CH))
        for k in range(SSD_CONV):
            off = r0 + HALO - (SSD_CONV - 1) + k
            acc = acc + cw_ref[k:k + 1, :] * ubuf[off:off + CONV_PIECE, :]
        xc[r0:r0 + CONV_PIECE, :] = _silu(acc)

    a_e = -jnp.exp(alog_ref[...])
    a_r = -jnp.exp(alogr_ref[...])
    tri = _tri_ones(L, lower=True)
    upp = _tri_ones(L, lower=False)
    rr = lax.broadcasted_iota(jnp.int32, (L, L), 0)
    cc = lax.broadcasted_iota(jnp.int32, (L, L), 1)
    causal = cc <= rr
    lane = lax.broadcasted_iota(jnp.int32, (L, LANES), 1)
    left = lane < SSD_HEAD_DIM
    HG = SSD_HEADS // SSD_GROUPS
    GW = SSD_WIDTH // SSD_GROUPS

    def body(c, carry):
        r0 = pl.multiple_of(c * L, L)
        rows = pl.ds(r0, L)
        xs = xc[rows, 0:SSD_WIDTH]
        dt_e = _softplus(dte_ref[rows, :] + dtb_ref[...])
        a3 = _split3(dt_e * a_e)
        acs_e = _dot(tri, a3[0]) + _dot(tri, a3[1]) + _dot(tri, a3[2])
        dt_r = _softplus(rows_ref[:, rows] + dtbr_ref[...])
        ar3 = _split3(dt_r * a_r)
        acs_r = _dot(ar3[0], upp) + _dot(ar3[1], upp) + _dot(ar3[2], upp)
        total = acs_e[L - 1:L, :]
        x_dt = xs * dt_e
        xb = x_dt.astype(BF16)
        xd = (x_dt * jnp.exp(total - acs_e)).astype(BF16)
        eacs = jnp.exp(acs_e)
        y_parts = []
        for g in range(SSD_GROUPS):
            bm = xc[rows, SSD_WIDTH + g * SSD_STATE:SSD_WIDTH + (g + 1) * SSD_STATE].astype(BF16)
            cm = xc[rows, SSD_WIDTH + (SSD_GROUPS + g) * SSD_STATE:
                    SSD_WIDTH + (SSD_GROUPS + g + 1) * SSD_STATE].astype(BF16)
            cb = _dot_nt(cm, bm)
            for pair in range(HG // 2):
                lo = g * GW + pair * LANES
                x_pair = xb[:, lo:lo + LANES]
                outs = []
                for sub in range(2):
                    h = g * HG + pair * 2 + sub
                    seg = acs_e[:, h * SSD_HEAD_DIM:h * SSD_HEAD_DIM + 1] - acs_r[h:h + 1, :]
                    decay = jnp.exp(jnp.where(causal, seg, -jnp.inf))
                    outs.append(_dot((cb * decay).astype(BF16), x_pair))
                y_parts.append(jnp.where(left, outs[0], outs[1]))
            s_prev = state[g]
            y_off = _dot(cm, s_prev.astype(BF16)) * eacs[:, g * GW:(g + 1) * GW]
            y_parts[-2] = y_parts[-2] + y_off[:, 0:LANES]
            y_parts[-1] = y_parts[-1] + y_off[:, LANES:2 * LANES]
            state[g] = s_prev * jnp.exp(total[:, g * GW:(g + 1) * GW]) + _dot_tn(bm, xd[:, g * GW:(g + 1) * GW])
        y = jnp.concatenate(y_parts, axis=1) + xs * dsk_ref[...]
        y = y * _silu(z_ref[rows, :])
        outs = []
        for g in range(SSD_GROUPS):
            yg = y[:, g * GW:(g + 1) * GW]
            ms = jnp.mean(yg * yg, axis=-1, keepdims=True)
            outs.append(yg * lax.rsqrt(ms + RMS_EPS))
        y_ref[rows, :] = (jnp.concatenate(outs, axis=1) * nw_ref[...]).astype(y_ref.dtype)
        return carry

    lax.fori_loop(0, tb // L, body, 0)


def _ssd(xbc, z, dte, rowsp, cw, cb, dtb_e, alog_e, dsk_e, nw, dtb_r, alog_r, *, batch, seq_len, tb, chunk):
    n = xbc.shape[0]
    tpb = seq_len // tb
    full = lambda a: pl.BlockSpec(a.shape, lambda b, t: (0,) * a.ndim)
    rowblk = lambda w: pl.BlockSpec((tb, w), lambda b, t: (b * tpb + t, 0))
    return pl.pallas_call(
        functools.partial(_ssd_kernel, chunk=chunk),
        out_shape=jax.ShapeDtypeStruct((n, SSD_WIDTH), BF16),
        grid=(batch, tpb),
        in_specs=[rowblk(SSD_CONV_CH), rowblk(SSD_WIDTH), rowblk(SSD_WIDTH),
                  pl.BlockSpec((16, tb), lambda b, t: (0, b * tpb + t)),
                  full(cw), full(cb), full(dtb_e), full(alog_e), full(dsk_e), full(nw),
                  full(dtb_r), full(alog_r)],
        out_specs=rowblk(SSD_WIDTH),
        scratch_shapes=[pltpu.VMEM((8, SSD_CONV_CH), F32),
                        pltpu.VMEM((tb + 8, SSD_CONV_CH), F32),
                        pltpu.VMEM((tb, SSD_CONV_CH), F32),
                        pltpu.VMEM((SSD_GROUPS, SSD_STATE, SSD_WIDTH // SSD_GROUPS), F32)],
        compiler_params=_cparams(("parallel", "arbitrary")),
        name="ssd_mixer",
    )(xbc, z, dte, rowsp, cw, cb, dtb_e, alog_e, dsk_e, nw, dtb_r, alog_r)


def _fox_kernel(qi_ref, kj_ref, kx_ref, qt_ref, vt_ref, o_ref, m_sc, acc_sc, *, ratio):
    step_id = pl.program_id(1)
    i = qi_ref[step_id]
    j = kj_ref[step_id]
    tk = kx_ref.shape[0]
    tq = qt_ref.shape[1]
    vrows = FOX_HEAD_DIM + FOX_ONES_ROWS

    @pl.when(j == 0)
    def _():
        m_sc[...] = jnp.full_like(m_sc, -jnp.inf)
        acc_sc[...] = jnp.zeros_like(acc_sc)

    def step(masked):
        if masked:
            key = j * tk + lax.broadcasted_iota(jnp.int32, (tk, tq), 0)
            qry = i * tq + lax.broadcasted_iota(jnp.int32, (tk, tq), 1)
            keep = key <= qry
        for h in range(FOX_HEADS):
            lo = h * FOX_SLOT
            st = _dot(kx_ref[:, lo:lo + FOX_SLOT], qt_ref[lo:lo + FOX_SLOT, :])
            if masked:
                st = jnp.where(keep, st, -jnp.inf)
            m_prev = m_sc[h]
            m_new = jnp.maximum(m_prev, jnp.max(st, axis=0, keepdims=True))
            alpha = jnp.exp2(m_prev - m_new)
            p = jnp.exp2(st - m_new).astype(BF16)
            acc_sc[h] = alpha * acc_sc[h] + _dot(vt_ref[lo:lo + vrows, :], p)
            m_sc[h] = m_new

    @pl.when(j < i * ratio)
    def _():
        step(False)

    @pl.when(j >= i * ratio)
    def _():
        step(True)

    @pl.when(j == (i + 1) * ratio - 1)
    def _():
        parts = [acc_sc[h, 0:FOX_HEAD_DIM, :] / acc_sc[h, FOX_HEAD_DIM:FOX_HEAD_DIM + 1, :]
                 for h in range(FOX_HEADS)]
        o_ref[...] = jnp.concatenate(parts, axis=0).T.astype(o_ref.dtype)


def _fox(kx, qt, vt, *, batch, seq_len, tq, tk):
    n = kx.shape[0]
    assert tq % tk == 0
    ratio = tq // tk
    nq = seq_len // tq
    nk = seq_len // tk
    pairs = [(i, j) for i in range(nq) for j in range((i + 1) * ratio)]
    qi = jnp.asarray([pq for pq, _ in pairs], jnp.int32)
    kj = jnp.asarray([pk for _, pk in pairs], jnp.int32)
    return pl.pallas_call(
        functools.partial(_fox_kernel, ratio=ratio),
        out_shape=jax.ShapeDtypeStruct((n, FOX_WIDTH), BF16),
        grid_spec=pltpu.PrefetchScalarGridSpec(
            num_scalar_prefetch=2,
            grid=(batch, len(pairs)),
            in_specs=[pl.BlockSpec((tk, FOX_EXT), lambda b, s, qi, kj: (b * nk + kj[s], 0)),
                      pl.BlockSpec((FOX_EXT, tq), lambda b, s, qi, kj: (0, b * nq + qi[s])),
                      pl.BlockSpec((FOX_EXT, tk), lambda b, s, qi, kj: (0, b * nk + kj[s]))],
            out_specs=pl.BlockSpec((tq, FOX_WIDTH), lambda b, s, qi, kj: (b * nq + qi[s], 0)),
            scratch_shapes=[pltpu.VMEM((FOX_HEADS, 1, tq), F32),
                            pltpu.VMEM((FOX_HEADS, FOX_HEAD_DIM + FOX_ONES_ROWS, tq), F32)]),
        compiler_params=_cparams(("parallel", "arbitrary")),
        name="fox_attention",
    )(qi, kj, kx, qt, vt)


POOL_HALO = 16


def _pool_kernel(u_ref, w_ref, b_ref, sc_ref, y_ref, tail, ubuf):
    t = pl.program_id(1)
    tb = u_ref.shape[0]
    P = POOL_PIECE

    @pl.when(t == 0)
    def _():
        tail[...] = jnp.zeros_like(tail)

    ubuf[0:POOL_HALO, :] = tail[...]
    ubuf[POOL_HALO:POOL_HALO + tb, :] = u_ref[...]
    tail[...] = u_ref[tb - POOL_HALO:tb, :]
    group = lax.broadcasted_iota(jnp.int32, (P, POOL_WIDTH), 1) // POOL_GROUP_DIM
    for p0 in range(0, tb, P):
        w1 = ubuf[p0:p0 + P + POOL_HALO, :]
        n1 = P + POOL_HALO
        a2 = w1[1:n1] + w1[0:n1 - 1]
        a4 = a2[2:n1 - 1] + a2[0:n1 - 3]
        a8 = a4[4:n1 - 3] + a4[0:n1 - 7]
        a16 = a8[8:n1 - 7] + a8[0:n1 - 15]
        sums = (a2[15:15 + P], a4[13:13 + P], a8[9:9 + P], a16[1:1 + P])
        tpos = t * tb + p0 + lax.broadcasted_iota(jnp.int32, (P, 1), 0) + 1
        pooled = jnp.zeros((P, POOL_WIDTH), F32)
        for g, win in enumerate(POOL_WINDOWS):
            cnt = jnp.minimum(tpos, win).astype(F32)
            pooled = jnp.where(group == g, sums[g] / cnt, pooled)
        pooled = pooled - w1[POOL_HALO:POOL_HALO + P]
        y = _dot(pooled.astype(BF16), w_ref[...]) + b_ref[...]
        y_ref[p0:p0 + P, :] = (y * sc_ref[...]).astype(y_ref.dtype)


def _pool(u, wbd, b, sc, *, batch, seq_len, tb):
    n = u.shape[0]
    tpb = seq_len // tb
    full = lambda a: pl.BlockSpec(a.shape, lambda bb, t: (0,) * a.ndim)
    rowblk = pl.BlockSpec((tb, POOL_WIDTH), lambda bb, t: (bb * tpb + t, 0))
    return pl.pallas_call(
        _pool_kernel,
        out_shape=jax.ShapeDtypeStruct((n, POOL_WIDTH), BF16),
        grid=(batch, tpb),
        in_specs=[rowblk, full(wbd), full(b), full(sc)],
        out_specs=rowblk,
        scratch_shapes=[pltpu.VMEM((POOL_HALO, POOL_WIDTH), F32),
                        pltpu.VMEM((tb + POOL_HALO, POOL_WIDTH), F32)],
        compiler_params=_cparams(("parallel", "arbitrary")),
        name="pool_mixer",
    )(u, wbd, b, sc)


def _out_proj_kernel(x_ref, ys_ref, yf_ref, yp_ref, w_ref, g_ref, b_ref, o_ref):
    mix = (_dot(ys_ref[...], w_ref[0:512, :]) + _dot(yf_ref[...], w_ref[512:768, :])
           + _dot(yp_ref[...], w_ref[768:1024, :]))
    o_ref[...] = _layer_norm(DN_ALPHA * x_ref[...] + mix, g_ref[...], b_ref[...])


def _out_proj(x2, ys, yf, yp, w, g, b, *, tm):
    n = x2.shape[0]
    full = lambda a: pl.BlockSpec(a.shape, lambda i: (0,) * a.ndim)
    rowblk = lambda wd: pl.BlockSpec((tm, wd), lambda i: (i, 0))
    return pl.pallas_call(
        _out_proj_kernel,
        out_shape=jax.ShapeDtypeStruct((n, D_MODEL), F32),
        grid=(n // tm,),
        in_specs=[rowblk(D_MODEL), rowblk(512), rowblk(256), rowblk(256), full(w), full(g), full(b)],
        out_specs=rowblk(D_MODEL),
        compiler_params=_cparams(("parallel",)),
        name="out_proj_ln",
    )(x2, ys, yf, yp, w, g, b)


def _kv_proj_kernel(m_ref, w_ref, o_ref):
    o_ref[...] = _dot(m_ref[...].astype(BF16), w_ref[...]).astype(o_ref.dtype)


def _kv_proj(mem2, wkv):
    m = mem2.shape[0]
    return pl.pallas_call(
        _kv_proj_kernel,
        out_shape=jax.ShapeDtypeStruct((m, wkv.shape[1]), BF16),
        grid=(1,),
        in_specs=[pl.BlockSpec(mem2.shape, lambda i: (0, 0)), pl.BlockSpec(wkv.shape, lambda i: (0, 0))],
        out_specs=pl.BlockSpec((m, wkv.shape[1]), lambda i: (0, 0)),
        compiler_params=_cparams(("arbitrary",)),
        name="xattn_kv_proj",
    )(mem2, wkv)


def _xattn_kernel(x_ref, kv_ref, wq_ref, wo_ref, g_ref, b_ref, o_ref):
    x = x_ref[...]
    q = _dot(x.astype(BF16), wq_ref[...]).astype(BF16)
    heads = []
    for h in range(XATTN_HEADS):
        lo = h * XATTN_HEAD_DIM
        k_h = kv_ref[:, lo:lo + XATTN_HEAD_DIM]
        v_h = kv_ref[:, D_MODEL + lo:D_MODEL + lo + XATTN_HEAD_DIM]
        s = _dot_nt(q[:, lo:lo + XATTN_HEAD_DIM], k_h)
        p = jnp.exp(s - jnp.max(s, axis=-1, keepdims=True))
        l = jnp.sum(p, axis=-1, keepdims=True)
        heads.append((_dot(p.astype(BF16), v_h) / l).astype(BF16))
    o = jnp.concatenate(heads, axis=1)
    xa = _dot(o, wo_ref[...])
    o_ref[...] = _layer_norm(DN_ALPHA * x + xa, g_ref[...], b_ref[...])


def _xattn(x2, kv, wq, wo, g, b, *, batch, seq_len, mem_len, tm):
    n = x2.shape[0]
    tpb = seq_len // tm
    full = lambda a: pl.BlockSpec(a.shape, lambda bb, t: (0,) * a.ndim)
    rowblk = pl.BlockSpec((tm, D_MODEL), lambda bb, t: (bb * tpb + t, 0))
    return pl.pallas_call(
        _xattn_kernel,
        out_shape=jax.ShapeDtypeStruct((n, D_MODEL), F32),
        grid=(batch, tpb),
        in_specs=[rowblk, pl.BlockSpec((mem_len, 2 * D_MODEL), lambda bb, t: (bb, 0)),
                  full(wq), full(wo), full(g), full(b)],
        out_specs=rowblk,
        compiler_params=_cparams(("parallel", "parallel")),
        name="xattn_ln",
    )(x2, kv, wq, wo, g, b)


ROW_SLABS = D_MODEL // LANES


def _ffn_kernel(texp_ref, nvalid_ref, x_ref, w1_ref, w3_ref, w2_ref, g_ref, b_ref, o_ref, xb_sc, acc_sc,
                *, grouped):
    t = pl.program_id(0)
    f = pl.program_id(1)
    nf = pl.num_programs(1)

    @pl.when(t < nvalid_ref[0])
    def _():
        @pl.when(f == 0)
        def _():
            if grouped:
                for s in range(ROW_SLABS):
                    xb_sc[:, s * LANES:(s + 1) * LANES] = x_ref[:, s, :].astype(BF16)
            else:
                xb_sc[...] = x_ref[...].astype(BF16)
            acc_sc[...] = jnp.zeros_like(acc_sc)

        xb = xb_sc[...]
        h1 = _dot(xb, w1_ref[0])
        h3 = _dot(xb, w3_ref[0])
        acc_sc[...] += _dot((_silu(h1) * h3).astype(BF16), w2_ref[0])

        @pl.when(f == nf - 1)
        def _():
            if grouped:
                for s in range(ROW_SLABS):
                    o_ref[:, s, :] = acc_sc[:, s * LANES:(s + 1) * LANES]
            else:
                o_ref[...] = _layer_norm(DN_ALPHA * x_ref[...] + acc_sc[...], g_ref[...], b_ref[...])

    @pl.when((t >= nvalid_ref[0]) & (f == 0))
    def _():
        o_ref[...] = jnp.zeros_like(o_ref)


def _ffn(tile_expert, nvalid, x, w1, w3, w2, g, b, *, tm, tf, grouped):
    n = x.shape[0]
    nt = n // tm
    nf = D_FF // tf

    def tile_of(t, nv):
        return jnp.minimum(t, nv[0] - 1)

    def f_of(t, f, nv):
        return jnp.where(t < nv[0], f, nf - 1)

    if grouped:
        rowblk = pl.BlockSpec((tm, ROW_SLABS, LANES), lambda t, f, te, nv: (tile_of(t, nv), 0, 0))
        outblk = pl.BlockSpec((tm, ROW_SLABS, LANES), lambda t, f, te, nv: (t, 0, 0))
        out_shape = jax.ShapeDtypeStruct((n, ROW_SLABS, LANES), F32)
    else:
        rowblk = pl.BlockSpec((tm, D_MODEL), lambda t, f, te, nv: (tile_of(t, nv), 0))
        outblk = pl.BlockSpec((tm, D_MODEL), lambda t, f, te, nv: (t, 0))
        out_shape = jax.ShapeDtypeStruct((n, D_MODEL), F32)
    w13 = pl.BlockSpec((1, D_MODEL, tf), lambda t, f, te, nv: (te[tile_of(t, nv)], 0, f_of(t, f, nv)))
    w2s = pl.BlockSpec((1, tf, D_MODEL), lambda t, f, te, nv: (te[tile_of(t, nv)], f_of(t, f, nv), 0))
    vec = pl.BlockSpec((1, D_MODEL), lambda t, f, te, nv: (0, 0))
    return pl.pallas_call(
        functools.partial(_ffn_kernel, grouped=grouped),
        out_shape=out_shape,
        grid_spec=pltpu.PrefetchScalarGridSpec(
            num_scalar_prefetch=2,
            grid=(nt, nf),
            in_specs=[rowblk, w13, w13, w2s, vec, vec],
            out_specs=outblk,
            scratch_shapes=[pltpu.VMEM((tm, D_MODEL), BF16), pltpu.VMEM((tm, D_MODEL), F32)]),
        compiler_params=_cparams(("arbitrary", "arbitrary"), vmem_mib=56),
        name="swiglu_grouped" if grouped else "swiglu_ln",
    )(tile_expert, nvalid, x, w1, w3, w2, g, b)


def _router_kernel(x_ref, w_ref, route_ref, counts_ref, carry):
    i = pl.program_id(0)
    tm = x_ref.shape[0]

    @pl.when(i == 0)
    def _():
        carry[...] = jnp.zeros_like(carry)

    xh, xm, _ = _split3(x_ref[...])
    wh, wm, _ = _split3(w_ref[...])
    logits = _dot(xh, wh) + (_dot(xh, wm) + _dot(xm, wh))
    lane = lax.broadcasted_iota(jnp.int32, (tm, LANES), 1)
    logits = jnp.where(lane < N_EXPERTS, logits, -jnp.inf)
    m1 = jnp.max(logits, axis=-1, keepdims=True)
    i1 = jnp.min(jnp.where(logits == m1, lane, LANES), axis=-1, keepdims=True)
    rest = jnp.where(lane == i1, -jnp.inf, logits)
    m2 = jnp.max(rest, axis=-1, keepdims=True)
    i2 = jnp.min(jnp.where(rest == m2, lane, LANES), axis=-1, keepdims=True)
    e21 = jnp.exp(m2 - m1)
    g1 = 1.0 / (1.0 + e21)
    g2 = e21 / (1.0 + e21)
    hit1 = lane == i1
    hit2 = lane == i2
    onehot = jnp.where(hit1 | hit2, 1.0, 0.0).astype(BF16)
    r = lax.broadcasted_iota(jnp.int32, (tm, tm), 0)
    c = lax.broadcasted_iota(jnp.int32, (tm, tm), 1)
    strict = jnp.where(c < r, 1.0, 0.0).astype(BF16)
    before = _dot(strict, onehot) + carry[...]
    rank1 = jnp.sum(jnp.where(hit1, before, 0.0), axis=-1, keepdims=True)
    rank2 = jnp.sum(jnp.where(hit2, before, 0.0), axis=-1, keepdims=True)
    carry[...] = carry[...] + jnp.sum(onehot.astype(F32), axis=0, keepdims=True)
    out = jnp.where(lane == 0, i1.astype(F32), 0.0)
    out = jnp.where(lane == 1, i2.astype(F32), out)
    out = jnp.where(lane == 2, g1, out)
    out = jnp.where(lane == 3, g2, out)
    out = jnp.where(lane == 4, rank1, out)
    out = jnp.where(lane == 5, rank2, out)
    route_ref[...] = out
    counts_ref[...] = carry[...]


def _router(x2, wr, *, tm):
    n = x2.shape[0]
    return pl.pallas_call(
        _router_kernel,
        out_shape=(jax.ShapeDtypeStruct((n, LANES), F32), jax.ShapeDtypeStruct((1, LANES), F32)),
        grid=(n // tm,),
        in_specs=[pl.BlockSpec((tm, D_MODEL), lambda i: (i, 0)), pl.BlockSpec(wr.shape, lambda i: (0, 0))],
        out_specs=(pl.BlockSpec((tm, LANES), lambda i: (i, 0)), pl.BlockSpec((1, LANES), lambda i: (0, 0))),
        scratch_shapes=[pltpu.VMEM((1, LANES), F32)],
        compiler_params=_cparams(("arbitrary",)),
        name="moe_router",
    )(x2, wr)


POS_ROWS = 8


ROW_UNROLL = 8


def _for_each_row_copy(row_copy, action):
    tokens_per_pos_row = LANES // TOP_K
    for prow in range(POS_ROWS):
        def body(c, carry):
            for k in range(TOP_K):
                copy = row_copy(prow * tokens_per_pos_row + c, (prow, TOP_K * c + k), k)
                getattr(copy, action)()
            return carry

        lax.fori_loop(0, tokens_per_pos_row, body, 0, unroll=ROW_UNROLL)


def _dispatch_kernel(pend_ref, padded_ref, pos_hbm, x_ref, xs_hbm, pos_smem, stage, zeros_vmem, sem_pos, sem_rows,
                     *, row_tm, pad_tm):
    i = pl.program_id(0)

    def zero_copy(start):
        return pltpu.make_async_copy(zeros_vmem, xs_hbm.at[pl.ds(start, pad_tm)], sem_rows)

    @pl.when(i == 0)
    def _():
        zeros_vmem[...] = jnp.zeros_like(zeros_vmem)
        used = pend_ref[N_EXPERTS - 1]
        fills = [(padded_ref[e] > 0, pend_ref[e] - pad_tm) for e in range(N_EXPERTS)]
        fills += [(used + e * pad_tm < xs_hbm.shape[0], used + e * pad_tm) for e in range(N_EXPERTS)]
        for cond, start in fills:
            @pl.when(cond)
            def _():
                zero_copy(start).start()
        for cond, start in fills:
            @pl.when(cond)
            def _():
                zero_copy(start).wait()

    pos_copy = pltpu.make_async_copy(pos_hbm.at[i], pos_smem, sem_pos)
    pos_copy.start()
    for s in range(ROW_SLABS):
        stage[:, s, :] = x_ref[:, s * LANES:(s + 1) * LANES]
    pos_copy.wait()

    def row_copy(r, entry, k):
        return pltpu.make_async_copy(stage.at[r], xs_hbm.at[pos_smem[entry]], sem_rows)

    _for_each_row_copy(row_copy, "start")
    _for_each_row_copy(row_copy, "wait")


def _dispatch(pend, padded, pos3, x2, *, cap, row_tm, pad_tm):
    n = x2.shape[0]
    assert TOP_K * row_tm == POS_ROWS * LANES
    return pl.pallas_call(
        functools.partial(_dispatch_kernel, row_tm=row_tm, pad_tm=pad_tm),
        out_shape=jax.ShapeDtypeStruct((cap, ROW_SLABS, LANES), F32),
        grid_spec=pltpu.PrefetchScalarGridSpec(
            num_scalar_prefetch=2,
            grid=(n // row_tm,),
            in_specs=[pl.BlockSpec(memory_space=pl.ANY),
                      pl.BlockSpec((row_tm, D_MODEL), lambda i, pe, pa: (i, 0))],
            out_specs=pl.BlockSpec(memory_space=pl.ANY),
            scratch_shapes=[pltpu.SMEM((POS_ROWS, LANES), jnp.int32),
                            pltpu.VMEM((row_tm, ROW_SLABS, LANES), F32),
                            pltpu.VMEM((pad_tm, ROW_SLABS, LANES), F32),
                            pltpu.SemaphoreType.DMA(()),
                            pltpu.SemaphoreType.DMA(())]),
        compiler_params=_cparams(("arbitrary",)),
        name="moe_dispatch",
    )(pend, padded, pos3, x2)


def _combine_kernel(pos_hbm, x_ref, route_ref, ys_hbm, g_ref, b_ref, o_ref, pos_smem, rows0, rows1, ff_sc,
                    sem_pos, sem_rows, *, row_tm):
    i = pl.program_id(0)
    pos_copy = pltpu.make_async_copy(pos_hbm.at[i], pos_smem, sem_pos)
    pos_copy.start()
    pos_copy.wait()
    rows = (rows0, rows1)

    def row_copy(r, entry, k):
        return pltpu.make_async_copy(ys_hbm.at[pos_smem[entry]], rows[k].at[r], sem_rows)

    _for_each_row_copy(row_copy, "start")
    _for_each_row_copy(row_copy, "wait")
    g1 = jnp.broadcast_to(route_ref[:, 2:3], (row_tm, LANES))
    g2 = jnp.broadcast_to(route_ref[:, 3:4], (row_tm, LANES))
    for s in range(ROW_SLABS):
        ff_sc[:, s * LANES:(s + 1) * LANES] = rows0[:, s, :] * g1 + rows1[:, s, :] * g2
    o_ref[...] = _layer_norm(DN_ALPHA * x_ref[...] + ff_sc[...], g_ref[...], b_ref[...])


def _combine(pos3, x2, route, ys, g, b, *, row_tm):
    n = x2.shape[0]
    assert TOP_K * row_tm == POS_ROWS * LANES
    rowblk = lambda w: pl.BlockSpec((row_tm, w), lambda i: (i, 0))
    vec = pl.BlockSpec((1, D_MODEL), lambda i: (0, 0))
    return pl.pallas_call(
        functools.partial(_combine_kernel, row_tm=row_tm),
        out_shape=jax.ShapeDtypeStruct((n, D_MODEL), F32),
        grid=(n // row_tm,),
        in_specs=[pl.BlockSpec(memory_space=pl.ANY), rowblk(D_MODEL), rowblk(LANES),
                  pl.BlockSpec(memory_space=pl.ANY), vec, vec],
        out_specs=rowblk(D_MODEL),
        scratch_shapes=[pltpu.SMEM((POS_ROWS, LANES), jnp.int32),
                        pltpu.VMEM((row_tm, ROW_SLABS, LANES), F32),
                        pltpu.VMEM((row_tm, ROW_SLABS, LANES), F32),
                        pltpu.VMEM((row_tm, D_MODEL), F32),
                        pltpu.SemaphoreType.DMA(()),
                        pltpu.SemaphoreType.DMA(())],
        compiler_params=_cparams(("arbitrary",)),
        name="moe_combine_ln",
    )(pos3, x2, route, ys, g, b)


def _row(v):
    return v.reshape(1, -1).astype(F32)


def _repeat_heads(v, width):
    return jnp.repeat(v.astype(F32), width).reshape(1, -1)


def _pad_rows16(v8, offset):
    out = jnp.zeros((16,), F32).at[offset:offset + v8.shape[0]].set(v8.astype(F32))
    return out.reshape(16, 1)


def _mixer_weights(w_in_l):
    offs = [0]
    for s in IN_SIZES:
        offs.append(offs[-1] + s)
    seg = lambda k: w_in_l[:, offs[k]:offs[k + 1]]
    wz, wxbc, wdt, wq, wk, wv, wf, wp = (seg(k) for k in range(8))
    wm = jnp.concatenate([wxbc, wz, wp, jnp.repeat(wdt, SSD_HEAD_DIM, axis=1)], axis=1).astype(BF16)
    ws = jnp.concatenate([wdt, wf, jnp.zeros((D_MODEL, LANES - 12), F32)], axis=1).astype(BF16)
    wst = ws[:, 0:16].T

    def slots(w):
        w = w.reshape(D_MODEL, FOX_HEADS, FOX_HEAD_DIM)
        w = jnp.pad(w, ((0, 0), (0, 0), (0, FOX_SLOT - FOX_HEAD_DIM)))
        return w.reshape(D_MODEL, FOX_EXT)

    wq = wq * (FOX_HEAD_DIM ** -0.5 * LOG2E)
    return wm, slots(wk).astype(BF16), slots(wq).T.astype(BF16), slots(wv).T.astype(BF16), ws, wst


def _fox_constants():
    place = jnp.zeros((FOX_BIAS_TERMS, LANES, FOX_EXT), F32)
    qones = jnp.zeros((FOX_EXT, 1), F32)
    vones = jnp.zeros((FOX_EXT, 1), F32)
    for h in range(FOX_HEADS):
        base = h * FOX_SLOT + FOX_HEAD_DIM
        for term in range(FOX_BIAS_TERMS):
            place = place.at[term, SSD_HEADS + h, base + term].set(1.0)
        qones = qones.at[base:base + FOX_BIAS_TERMS].set(1.0)
        vones = vones.at[base:base + FOX_ONES_ROWS].set(1.0)
    return place.astype(BF16), qones, vones


def _token_mixing(x2, p, layer, *, batch, seq_len):
    wm, wkx, wqt, wvt, ws, wst = _mixer_weights(p["w_in"][layer])
    fb = p["fox_f_bias"][layer].astype(F32)
    fbc = jnp.zeros((LANES,), F32).at[SSD_HEADS:SSD_HEADS + FOX_HEADS].set(fb).reshape(1, LANES)
    place, qones, vones = _fox_constants()
    xbc, z, pool_in, dte, kx, qt, vt, rowsp = _in_proj(
        x2, wm, wkx, wqt, wvt, ws, wst, fbc, place, qones, vones, seq_len=seq_len, tm=min(IN_TM, seq_len))

    y_ssd = _ssd(xbc, z, dte, rowsp,
                 p["ssm_conv_w"][layer].astype(F32), _row(p["ssm_conv_b"][layer]),
                 _repeat_heads(p["ssm_dt_bias"][layer], SSD_HEAD_DIM),
                 _repeat_heads(p["ssm_a_log"][layer], SSD_HEAD_DIM),
                 _repeat_heads(p["ssm_d"][layer], SSD_HEAD_DIM),
                 _row(p["ssm_norm_w"][layer]),
                 _pad_rows16(p["ssm_dt_bias"][layer], 0), _pad_rows16(p["ssm_a_log"][layer], 0),
                 batch=batch, seq_len=seq_len, tb=min(SSD_TB, seq_len), chunk=SSD_L)

    y_fox = _fox(kx, qt, vt, batch=batch, seq_len=seq_len, tq=min(FOX_TQ, seq_len), tk=min(FOX_TK, seq_len))

    wbd = jax.scipy.linalg.block_diag(*[p["pool_w"][layer][g] for g in range(len(POOL_WINDOWS))]).astype(BF16)
    y_pool = _pool(pool_in, wbd, _row(p["pool_b"][layer]), _row(p["pool_scale"][layer]),
                   batch=batch, seq_len=seq_len, tb=min(POOL_TB, seq_len))

    return _out_proj(x2, y_ssd, y_fox, y_pool, p["w_out"][layer].astype(BF16),
                     _row(p["ln1_g"][layer]), _row(p["ln1_b"][layer]), tm=min(OUT_TM, seq_len))


def _cross_attention(x2, mem2, p, layer, *, batch, seq_len, mem_len):
    wkv = jnp.concatenate([p["xa_wk"][layer], p["xa_wv"][layer]], axis=1).astype(BF16)
    kv = _kv_proj(mem2, wkv)
    wq = (p["xa_wq"][layer] * (XATTN_HEAD_DIM ** -0.5)).astype(BF16)
    return _xattn(x2, kv, wq, p["xa_wo"][layer].astype(BF16), _row(p["ln2_g"][layer]), _row(p["ln2_b"][layer]),
                  batch=batch, seq_len=seq_len, mem_len=mem_len, tm=min(XA_TM, seq_len))


def _dense_ffn(x2, p, layer):
    j = layer // 2
    n = x2.shape[0]
    tm = min(FFN_TM, n)
    tile_expert = jnp.zeros((n // tm,), jnp.int32)
    nvalid = jnp.full((1,), n // tm, jnp.int32)
    return _ffn(tile_expert, nvalid, x2, p["ffn_w1"][j][None].astype(BF16), p["ffn_w3"][j][None].astype(BF16),
                p["ffn_w2"][j][None].astype(BF16), _row(p["ln3_g"][layer]), _row(p["ln3_b"][layer]),
                tm=tm, tf=FFN_TF, grouped=False)


def _moe_ffn(x2, p, layer):
    j = layer // 2
    n = x2.shape[0]
    tm = min(MOE_TM, n)
    row_tm = min(ROW_TM, n)
    wr = jnp.concatenate([p["router_w"][j].astype(F32), jnp.zeros((D_MODEL, LANES - N_EXPERTS), F32)], axis=1)
    route, counts = _router(x2, wr, tm=min(ROUTE_TM, n))

    counts = counts[0, :N_EXPERTS].astype(jnp.int32)
    padded = (counts + tm - 1) // tm * tm
    pend = jnp.cumsum(padded)
    pstart = pend - padded
    experts = route[:, 0:TOP_K].astype(jnp.int32)
    ranks = route[:, 4:4 + TOP_K].astype(jnp.int32)
    pos = (pstart[experts] + ranks).astype(jnp.int32)
    pos3 = pos.reshape(n // row_tm, POS_ROWS, LANES)
    cap = n * TOP_K + N_EXPERTS * tm
    ntiles = cap // tm
    tile_start = jnp.arange(ntiles, dtype=jnp.int32) * tm
    tile_expert = jnp.minimum(jnp.sum((pend[None, :] <= tile_start[:, None]).astype(jnp.int32), axis=1),
                              N_EXPERTS - 1).astype(jnp.int32)
    nvalid = (pend[-1:] // tm).astype(jnp.int32)

    xs = _dispatch(pend.astype(jnp.int32), padded.astype(jnp.int32), pos3, x2, cap=cap, row_tm=row_tm, pad_tm=tm)
    ys = _ffn(tile_expert, nvalid, xs, p["moe_w1"][j].astype(BF16), p["moe_w3"][j].astype(BF16),
              p["moe_w2"][j].astype(BF16), _row(p["ln3_g"][layer]), _row(p["ln3_b"][layer]),
              tm=tm, tf=FFN_TF, grouped=True)
    return _combine(pos3, x2, route, ys, _row(p["ln3_g"][layer]), _row(p["ln3_b"][layer]), row_tm=row_tm)


def _forward(x, mem, p):
    batch, seq_len, _ = x.shape
    mem_len = mem.shape[1]
    x2 = x.reshape(batch * seq_len, D_MODEL).astype(F32)
    mem2 = mem.reshape(batch * mem_len, D_MODEL).astype(F32)
    for layer in range(DEPTH):
        x2 = _token_mixing(x2, p, layer, batch=batch, seq_len=seq_len)
        x2 = _cross_attention(x2, mem2, p, layer, batch=batch, seq_len=seq_len, mem_len=mem_len)
        x2 = _dense_ffn(x2, p, layer) if layer % 2 == 0 else _moe_ffn(x2, p, layer)
    return x2.reshape(batch, seq_len, D_MODEL)


def kernel(x, mem, w_in, ssm_conv_w, ssm_conv_b, ssm_dt_bias, ssm_a_log, ssm_d, ssm_norm_w, fox_f_bias, pool_w, pool_b, pool_scale, w_out, ln1_g, ln1_b, xa_wq, xa_wk, xa_wv, xa_wo, ln2_g, ln2_b, ffn_w1, ffn_w3, ffn_w2, router_w, moe_w1, moe_w3, moe_w2, ln3_g, ln3_b):
    p = dict(w_in=w_in, ssm_conv_w=ssm_conv_w, ssm_conv_b=ssm_conv_b, ssm_dt_bias=ssm_dt_bias,
             ssm_a_log=ssm_a_log, ssm_d=ssm_d, ssm_norm_w=ssm_norm_w, fox_f_bias=fox_f_bias, pool_w=pool_w,
             pool_b=pool_b, pool_scale=pool_scale, w_out=w_out, ln1_g=ln1_g, ln1_b=ln1_b, xa_wq=xa_wq,
             xa_wk=xa_wk, xa_wv=xa_wv, xa_wo=xa_wo, ln2_g=ln2_g, ln2_b=ln2_b, ffn_w1=ffn_w1, ffn_w3=ffn_w3,
             ffn_w2=ffn_w2, router_w=router_w, moe_w1=moe_w1, moe_w3=moe_w3, moe_w2=moe_w2, ln3_g=ln3_g,
             ln3_b=ln3_b)
    return _forward(x, mem, p)
```

```python
import functools

import jax
import jax.numpy as jnp
from jax import lax
from jax.experimental import pallas as pl
from jax.experimental.pallas import tpu as pltpu

F32 = jnp.float32
BF16 = jnp.bfloat16

D_MODEL = 1024
DEPTH = 4
SSD_WIDTH = 512
SSD_HEAD_DIM = 64
SSD_HEADS = 8
SSD_GROUPS = 2
SSD_STATE = 128
SSD_CONV = 4
SSD_CONV_CH = 1024
FOX_WIDTH = 256
FOX_HEAD_DIM = 64
FOX_HEADS = 4
POOL_WIDTH = 256
POOL_WINDOWS = (2, 4, 8, 16)
POOL_GROUP_DIM = 64
IN_SIZES = (512, 1024, 8, 256, 256, 256, 4, 256)
XATTN_HEADS = 4
XATTN_HEAD_DIM = 256
D_FF = 3584
N_EXPERTS = 8
TOP_K = 2
DN_ALPHA = (2 * DEPTH) ** 0.25
LN_EPS = 1e-5
RMS_EPS = 1e-5

LANES = 128
MIB = 1024 * 1024

IN_TM = 512
SSD_TB = 512
SSD_L = 128
FOX_TQ = 1024
FOX_TK = 512
FOX_QW = 128
FOX_LOOKAHEAD = 4
POOL_TB = 512
POOL_PIECE = 128
OUT_TM = 512
XA_TM = 512
FFN_TM = 1024
FFN_TF = 512
MOE_TM = 1024
ROUTE_TM = 512
ROW_TM = 512


def _cparams(sem, vmem_mib=48):
    return pltpu.CompilerParams(dimension_semantics=sem, vmem_limit_bytes=vmem_mib * MIB)


def _dot(a, b):
    return jnp.dot(a, b, preferred_element_type=F32)


def _dot_nt(a, b):
    return lax.dot_general(a, b, (((1,), (1,)), ((), ())), preferred_element_type=F32)


def _dot_tn(a, b):
    return lax.dot_general(a, b, (((0,), (0,)), ((), ())), preferred_element_type=F32)


def _split3(x):
    hi = x.astype(BF16)
    r1 = x - hi.astype(F32)
    mid = r1.astype(BF16)
    lo = (r1 - mid.astype(F32)).astype(BF16)
    return hi, mid, lo


def _silu(x):
    return x / (1.0 + jnp.exp(-x))


def _softplus(x):
    return jnp.maximum(x, 0.0) + jnp.log1p(jnp.exp(-jnp.abs(x)))


def _log_sigmoid(x):
    return jnp.minimum(x, 0.0) - jnp.log1p(jnp.exp(-jnp.abs(x)))


def _layer_norm(v, g, b):
    mu = jnp.mean(v, axis=-1, keepdims=True)
    d = v - mu
    var = jnp.mean(d * d, axis=-1, keepdims=True)
    return d * lax.rsqrt(var + LN_EPS) * g + b


def _tri_ones(n, lower):
    r = lax.broadcasted_iota(jnp.int32, (n, n), 0)
    c = lax.broadcasted_iota(jnp.int32, (n, n), 1)
    m = (c <= r) if lower else (r <= c)
    return jnp.where(m, 1.0, 0.0).astype(BF16)


FOX_SLOT = LANES
FOX_EXT = FOX_HEADS * FOX_SLOT
FOX_BIAS_TERMS = 3
FOX_ONES_ROWS = 16
LOG2E = 1.4426950408889634


def _in_proj_kernel(x_ref, wm_ref, wkx_ref, wqt_ref, wvt_ref, ws_ref, wst_ref, fbc_ref, place_ref, qones_ref,
                    vones_ref, xbc_ref, z_ref, pool_ref, dte_ref, kx_ref, qt_ref, vt_ref, rows_ref,
                    carry_c, *, tiles_per_batch):
    i = pl.program_id(0)
    tm = x_ref.shape[0]
    xb = x_ref[...].astype(BF16)

    xbc_ref[...] = _dot(xb, wm_ref[:, 0:1024])
    z_ref[...] = _dot(xb, wm_ref[:, 1024:1536])
    pool_ref[...] = _dot(xb, wm_ref[:, 1536:1792])
    dte_ref[...] = _dot(xb, wm_ref[:, 1792:2304])
    qt_ref[...] = (_dot_nt(wqt_ref[...], xb) + qones_ref[...]).astype(BF16)
    vt_ref[...] = (_dot_nt(wvt_ref[...], xb) + vones_ref[...]).astype(BF16)
    rows_ref[...] = _dot_nt(wst_ref[...], xb)

    @pl.when(i % tiles_per_batch == 0)
    def _():
        carry_c[...] = jnp.zeros_like(carry_c)

    small_c = _dot(xb, ws_ref[...])
    c3 = _split3(_log_sigmoid(small_c + fbc_ref[...]))
    tri = _tri_ones(tm, lower=True)
    cs_c = _dot(tri, c3[0]) + _dot(tri, c3[1]) + _dot(tri, c3[2]) + carry_c[...]
    carry_c[...] = cs_c[tm - 1:tm, :]
    n3 = _split3(cs_c * (-LOG2E))
    kx = _dot(xb, wkx_ref[...])
    for term in range(FOX_BIAS_TERMS):
        kx = kx + _dot(n3[term], place_ref[term])
    kx_ref[...] = kx.astype(BF16)


def _in_proj(x2, wm, wkx, wqt, wvt, ws, wst, fbc, place, qones, vones, *, seq_len, tm):
    n = x2.shape[0]
    grid = (n // tm,)
    full = lambda a: pl.BlockSpec(a.shape, lambda i: (0,) * a.ndim)
    rowblk = lambda w: pl.BlockSpec((tm, w), lambda i: (i, 0))
    colblk = lambda h: pl.BlockSpec((h, tm), lambda i: (0, i))
    out_shape = (
        jax.ShapeDtypeStruct((n, 1024), F32),
        jax.ShapeDtypeStruct((n, 512), F32),
        jax.ShapeDtypeStruct((n, 256), F32),
        jax.ShapeDtypeStruct((n, 512), F32),
        jax.ShapeDtypeStruct((n, FOX_EXT), BF16),
        jax.ShapeDtypeStruct((FOX_EXT, n), BF16),
        jax.ShapeDtypeStruct((FOX_EXT, n), BF16),
        jax.ShapeDtypeStruct((16, n), F32),
    )
    out_specs = (rowblk(1024), rowblk(512), rowblk(256), rowblk(512), rowblk(FOX_EXT),
                 colblk(FOX_EXT), colblk(FOX_EXT), colblk(16))
    args = (x2, wm, wkx, wqt, wvt, ws, wst, fbc, place, qones, vones)
    return pl.pallas_call(
        functools.partial(_in_proj_kernel, tiles_per_batch=seq_len // tm),
        out_shape=out_shape,
        grid=grid,
        in_specs=[rowblk(D_MODEL)] + [full(a) for a in args[1:]],
        out_specs=out_specs,
        scratch_shapes=[pltpu.VMEM((1, LANES), F32)],
        compiler_params=_cparams(("arbitrary",)),
        name="in_proj",
    )(*args)


def _ssd_kernel(xbc_ref, z_ref, dte_ref, rows_ref, cw_ref, cb_ref, dtb_ref, alog_ref, dsk_ref, nw_ref,
                dtbr_ref, alogr_ref, y_ref, tail, ubuf, xc, state, *, chunk):
    t = pl.program_id(1)
    tb = xbc_ref.shape[0]
    L = chunk
    CONV_PIECE = 64
    HALO = 8

    @pl.when(t == 0)
    def _():
        tail[...] = jnp.zeros_like(tail)
        state[...] = jnp.zeros_like(state)

    ubuf[0:HALO, :] = tail[...]
    ubuf[HALO:HALO + tb, :] = xbc_ref[...]
    tail[...] = xbc_ref[tb - HALO:tb, :]
    for r0 in range(0, tb, CONV_PIECE):
        acc = jnp.broadcast_to(cb_ref[...], (CONV_PIECE, SSD_CONV_CH))
        for k in range(SSD_CONV):
            off = r0 + HALO - (SSD_CONV - 1) + k
            acc = acc + cw_ref[k:k + 1, :] * ubuf[off:off + CONV_PIECE, :]
        xc[r0:r0 + CONV_PIECE, :] = _silu(acc)

    a_e = -jnp.exp(alog_ref[...])
    a_r = -jnp.exp(alogr_ref[...])
    tri = _tri_ones(L, lower=True)
    upp = _tri_ones(L, lower=False)
    rr = lax.broadcasted_iota(jnp.int32, (L, L), 0)
    cc = lax.broadcasted_iota(jnp.int32, (L, L), 1)
    causal = cc <= rr
    lane = lax.broadcasted_iota(jnp.int32, (L, LANES), 1)
    left = lane < SSD_HEAD_DIM
    HG = SSD_HEADS // SSD_GROUPS
    GW = SSD_WIDTH // SSD_GROUPS

    def body(c, carry):
        r0 = pl.multiple_of(c * L, L)
        rows = pl.ds(r0, L)
        xs = xc[rows, 0:SSD_WIDTH]
        dt_e = _softplus(dte_ref[rows, :] + dtb_ref[...])
        a3 = _split3(dt_e * a_e)
        acs_e = _dot(tri, a3[0]) + _dot(tri, a3[1]) + _dot(tri, a3[2])
        dt_r = _softplus(rows_ref[:, rows] + dtbr_ref[...])
        ar3 = _split3(dt_r * a_r)
        acs_r = _dot(ar3[0], upp) + _dot(ar3[1], upp) + _dot(ar3[2], upp)
        total = acs_e[L - 1:L, :]
        x_dt = xs * dt_e
        xb = x_dt.astype(BF16)
        xd = (x_dt * jnp.exp(total - acs_e)).astype(BF16)
        eacs = jnp.exp(acs_e)
        y_parts = []
        for g in range(SSD_GROUPS):
            bm = xc[rows, SSD_WIDTH + g * SSD_STATE:SSD_WIDTH + (g + 1) * SSD_STATE].astype(BF16)
            cm = xc[rows, SSD_WIDTH + (SSD_GROUPS + g) * SSD_STATE:
                    SSD_WIDTH + (SSD_GROUPS + g + 1) * SSD_STATE].astype(BF16)
            cb = _dot_nt(cm, bm)
            for pair in range(HG // 2):
                lo = g * GW + pair * LANES
                x_pair = xb[:, lo:lo + LANES]
                outs = []
                for sub in range(2):
                    h = g * HG + pair * 2 + sub
                    seg = acs_e[:, h * SSD_HEAD_DIM:h * SSD_HEAD_DIM + 1] - acs_r[h:h + 1, :]
                    decay = jnp.exp(jnp.where(causal, seg, -jnp.inf))
                    outs.append(_dot((cb * decay).astype(BF16), x_pair))
                y_parts.append(jnp.where(left, outs[0], outs[1]))
            s_prev = state[g]
            y_off = _dot(cm, s_prev.astype(BF16)) * eacs[:, g * GW:(g + 1) * GW]
            y_parts[-2] = y_parts[-2] + y_off[:, 0:LANES]
            y_parts[-1] = y_parts[-1] + y_off[:, LANES:2 * LANES]
            state[g] = s_prev * jnp.exp(total[:, g * GW:(g + 1) * GW]) + _dot_tn(bm, xd[:, g * GW:(g + 1) * GW])
        y = jnp.concatenate(y_parts, axis=1) + xs * dsk_ref[...]
        y = y * _silu(z_ref[rows, :])
        outs = []
        for g in range(SSD_GROUPS):
            yg = y[:, g * GW:(g + 1) * GW]
            ms = jnp.mean(yg * yg, axis=-1, keepdims=True)
            outs.append(yg * lax.rsqrt(ms + RMS_EPS))
        y_ref[rows, :] = (jnp.concatenate(outs, axis=1) * nw_ref[...]).astype(y_ref.dtype)
        return carry

    lax.fori_loop(0, tb // L, body, 0)


def _ssd(xbc, z, dte, rowsp, cw, cb, dtb_e, alog_e, dsk_e, nw, dtb_r, alog_r, *, batch, seq_len, tb, chunk):
    n = xbc.shape[0]
    tpb = seq_len // tb
    full = lambda a: pl.BlockSpec(a.shape, lambda b, t: (0,) * a.ndim)
    rowblk = lambda w: pl.BlockSpec((tb, w), lambda b, t: (b * tpb + t, 0))
    return pl.pallas_call(
        functools.partial(_ssd_kernel, chunk=chunk),
        out_shape=jax.ShapeDtypeStruct((n, SSD_WIDTH), BF16),
        grid=(batch, tpb),
        in_specs=[rowblk(SSD_CONV_CH), rowblk(SSD_WIDTH), rowblk(SSD_WIDTH),
                  pl.BlockSpec((16, tb), lambda b, t: (0, b * tpb + t)),
                  full(cw), full(cb), full(dtb_e), full(alog_e), full(dsk_e), full(nw),
                  full(dtb_r), full(alog_r)],
        out_specs=rowblk(SSD_WIDTH),
        scratch_shapes=[pltpu.VMEM((8, SSD_CONV_CH), F32),
                        pltpu.VMEM((tb + 8, SSD_CONV_CH), F32),
                        pltpu.VMEM((tb, SSD_CONV_CH), F32),
                        pltpu.VMEM((SSD_GROUPS, SSD_STATE, SSD_WIDTH // SSD_GROUPS), F32)],
        compiler_params=_cparams(("parallel", "arbitrary")),
        name="ssd_mixer",
    )(xbc, z, dte, rowsp, cw, cb, dtb_e, alog_e, dsk_e, nw, dtb_r, alog_r)


def _fox_kernel(qi_ref, kj_ref, kx_ref, qt_ref, vt_ref, o_ref, m_sc, acc_sc, *, ratio, qw):
    step_id = pl.program_id(1)
    i = qi_ref[step_id]
    j = kj_ref[step_id]
    tk = kx_ref.shape[0]
    tq = qt_ref.shape[1]
    vrows = FOX_HEAD_DIM + FOX_ONES_ROWS

    @pl.when(j == 0)
    def _():
        m_sc[...] = jnp.full_like(m_sc, -jnp.inf)
        acc_sc[...] = jnp.zeros_like(acc_sc)

    def step(masked):
        units = [(h, c0) for h in range(FOX_HEADS) for c0 in range(0, tq, qw)]

        def score(unit):
            h, c0 = unit
            return _dot(kx_ref[:, h * FOX_SLOT:(h + 1) * FOX_SLOT],
                        qt_ref[h * FOX_SLOT:(h + 1) * FOX_SLOT, c0:c0 + qw])

        scores = [score(u) for u in units[:FOX_LOOKAHEAD]]
        for n_unit, (h, c0) in enumerate(units):
            if n_unit + FOX_LOOKAHEAD < len(units):
                scores.append(score(units[n_unit + FOX_LOOKAHEAD]))
            st = scores[n_unit]
            lo = h * FOX_SLOT
            if masked:
                key = j * tk + lax.broadcasted_iota(jnp.int32, (tk, qw), 0)
                qry = i * tq + c0 + lax.broadcasted_iota(jnp.int32, (tk, qw), 1)
                st = jnp.where(key <= qry, st, -jnp.inf)
            m_prev = m_sc[h, :, c0:c0 + qw]
            m_new = jnp.maximum(m_prev, jnp.max(st, axis=0, keepdims=True))
            alpha = jnp.exp2(m_prev - m_new)
            p = jnp.exp2(st - m_new).astype(BF16)
            acc_sc[h, :, c0:c0 + qw] = alpha * acc_sc[h, :, c0:c0 + qw] + _dot(vt_ref[lo:lo + vrows, :], p)
            m_sc[h, :, c0:c0 + qw] = m_new

    @pl.when(j < i * ratio)
    def _():
        step(False)

    @pl.when(j >= i * ratio)
    def _():
        step(True)

    @pl.when(j == (i + 1) * ratio - 1)
    def _():
        parts = [acc_sc[h, 0:FOX_HEAD_DIM, :] / acc_sc[h, FOX_HEAD_DIM:FOX_HEAD_DIM + 1, :]
                 for h in range(FOX_HEADS)]
        o_ref[...] = jnp.concatenate(parts, axis=0).T.astype(o_ref.dtype)


def _fox(kx, qt, vt, *, batch, seq_len, tq, tk):
    n = kx.shape[0]
    assert tq % tk == 0
    ratio = tq // tk
    nq = seq_len // tq
    nk = seq_len // tk
    pairs = [(i, j) for i in range(nq) for j in range((i + 1) * ratio)]
    qi = jnp.asarray([pq for pq, _ in pairs], jnp.int32)
    kj = jnp.asarray([pk for _, pk in pairs], jnp.int32)
    return pl.pallas_call(
        functools.partial(_fox_kernel, ratio=ratio, qw=min(FOX_QW, tq)),
        out_shape=jax.ShapeDtypeStruct((n, FOX_WIDTH), BF16),
        grid_spec=pltpu.PrefetchScalarGridSpec(
            num_scalar_prefetch=2,
            grid=(batch, len(pairs)),
            in_specs=[pl.BlockSpec((tk, FOX_EXT), lambda b, s, qi, kj: (b * nk + kj[s], 0)),
                      pl.BlockSpec((FOX_EXT, tq), lambda b, s, qi, kj: (0, b * nq + qi[s])),
                      pl.BlockSpec((FOX_EXT, tk), lambda b, s, qi, kj: (0, b * nk + kj[s]))],
            out_specs=pl.BlockSpec((tq, FOX_WIDTH), lambda b, s, qi, kj: (b * nq + qi[s], 0)),
            scratch_shapes=[pltpu.VMEM((FOX_HEADS, 1, tq), F32),
                            pltpu.VMEM((FOX_HEADS, FOX_HEAD_DIM + FOX_ONES_ROWS, tq), F32)]),
        compiler_params=_cparams(("parallel", "arbitrary")),
        name="fox_attention",
    )(qi, kj, kx, qt, vt)


POOL_HALO = 16


def _pool_kernel(u_ref, w_ref, b_ref, sc_ref, y_ref, tail, ubuf):
    t = pl.program_id(1)
    tb = u_ref.shape[0]
    P = POOL_PIECE

    @pl.when(t == 0)
    def _():
        tail[...] = jnp.zeros_like(tail)

    ubuf[0:POOL_HALO, :] = tail[...]
    ubuf[POOL_HALO:POOL_HALO + tb, :] = u_ref[...]
    tail[...] = u_ref[tb - POOL_HALO:tb, :]
    group = lax.broadcasted_iota(jnp.int32, (P, POOL_WIDTH), 1) // POOL_GROUP_DIM
    for p0 in range(0, tb, P):
        w1 = ubuf[p0:p0 + P + POOL_HALO, :]
        n1 = P + POOL_HALO
        a2 = w1[1:n1] + w1[0:n1 - 1]
        a4 = a2[2:n1 - 1] + a2[0:n1 - 3]
        a8 = a4[4:n1 - 3] + a4[0:n1 - 7]
        a16 = a8[8:n1 - 7] + a8[0:n1 - 15]
        sums = (a2[15:15 + P], a4[13:13 + P], a8[9:9 + P], a16[1:1 + P])
        tpos = t * tb + p0 + lax.broadcasted_iota(jnp.int32, (P, 1), 0) + 1
        pooled = jnp.zeros((P, POOL_WIDTH), F32)
        for g, win in enumerate(POOL_WINDOWS):
            cnt = jnp.minimum(tpos, win).astype(F32)
            pooled = jnp.where(group == g, sums[g] / cnt, pooled)
        pooled = pooled - w1[POOL_HALO:POOL_HALO + P]
        y = _dot(pooled.astype(BF16), w_ref[...]) + b_ref[...]
        y_ref[p0:p0 + P, :] = (y * sc_ref[...]).astype(y_ref.dtype)


def _pool(u, wbd, b, sc, *, batch, seq_len, tb):
    n = u.shape[0]
    tpb = seq_len // tb
    full = lambda a: pl.BlockSpec(a.shape, lambda bb, t: (0,) * a.ndim)
    rowblk = pl.BlockSpec((tb, POOL_WIDTH), lambda bb, t: (bb * tpb + t, 0))
    return pl.pallas_call(
        _pool_kernel,
        out_shape=jax.ShapeDtypeStruct((n, POOL_WIDTH), BF16),
        grid=(batch, tpb),
        in_specs=[rowblk, full(wbd), full(b), full(sc)],
        out_specs=rowblk,
        scratch_shapes=[pltpu.VMEM((POOL_HALO, POOL_WIDTH), F32),
                        pltpu.VMEM((tb + POOL_HALO, POOL_WIDTH), F32)],
        compiler_params=_cparams(("parallel", "arbitrary")),
        name="pool_mixer",
    )(u, wbd, b, sc)


def _out_proj_kernel(x_ref, ys_ref, yf_ref, yp_ref, w_ref, g_ref, b_ref, o_ref):
    mix = (_dot(ys_ref[...], w_ref[0:512, :]) + _dot(yf_ref[...], w_ref[512:768, :])
           + _dot(yp_ref[...], w_ref[768:1024, :]))
    o_ref[...] = _layer_norm(DN_ALPHA * x_ref[...] + mix, g_ref[...], b_ref[...])


def _out_proj(x2, ys, yf, yp, w, g, b, *, tm):
    n = x2.shape[0]
    full = lambda a: pl.BlockSpec(a.shape, lambda i: (0,) * a.ndim)
    rowblk = lambda wd: pl.BlockSpec((tm, wd), lambda i: (i, 0))
    return pl.pallas_call(
        _out_proj_kernel,
        out_shape=jax.ShapeDtypeStruct((n, D_MODEL), F32),
        grid=(n // tm,),
        in_specs=[rowblk(D_MODEL), rowblk(512), rowblk(256), rowblk(256), full(w), full(g), full(b)],
        out_specs=rowblk(D_MODEL),
        compiler_params=_cparams(("parallel",)),
        name="out_proj_ln",
    )(x2, ys, yf, yp, w, g, b)


def _kv_proj_kernel(m_ref, w_ref, o_ref):
    o_ref[...] = _dot(m_ref[...].astype(BF16), w_ref[...]).astype(o_ref.dtype)


def _kv_proj(mem2, wkv):
    m = mem2.shape[0]
    return pl.pallas_call(
        _kv_proj_kernel,
        out_shape=jax.ShapeDtypeStruct((m, wkv.shape[1]), BF16),
        grid=(1,),
        in_specs=[pl.BlockSpec(mem2.shape, lambda i: (0, 0)), pl.BlockSpec(wkv.shape, lambda i: (0, 0))],
        out_specs=pl.BlockSpec((m, wkv.shape[1]), lambda i: (0, 0)),
        compiler_params=_cparams(("arbitrary",)),
        name="xattn_kv_proj",
    )(mem2, wkv)


def _xattn_kernel(x_ref, kv_ref, wq_ref, wo_ref, g_ref, b_ref, o_ref):
    x = x_ref[...]
    q = _dot(x.astype(BF16), wq_ref[...]).astype(BF16)
    heads = []
    for h in range(XATTN_HEADS):
        lo = h * XATTN_HEAD_DIM
        k_h = kv_ref[:, lo:lo + XATTN_HEAD_DIM]
        v_h = kv_ref[:, D_MODEL + lo:D_MODEL + lo + XATTN_HEAD_DIM]
        s = _dot_nt(q[:, lo:lo + XATTN_HEAD_DIM], k_h)
        p = jnp.exp(s - jnp.max(s, axis=-1, keepdims=True))
        l = jnp.sum(p, axis=-1, keepdims=True)
        heads.append((_dot(p.astype(BF16), v_h) / l).astype(BF16))
    o = jnp.concatenate(heads, axis=1)
    xa = _dot(o, wo_ref[...])
    o_ref[...] = _layer_norm(DN_ALPHA * x + xa, g_ref[...], b_ref[...])


def _xattn(x2, kv, wq, wo, g, b, *, batch, seq_len, mem_len, tm):
    n = x2.shape[0]
    tpb = seq_len // tm
    full = lambda a: pl.BlockSpec(a.shape, lambda bb, t: (0,) * a.ndim)
    rowblk = pl.BlockSpec((tm, D_MODEL), lambda bb, t: (bb * tpb + t, 0))
    return pl.pallas_call(
        _xattn_kernel,
        out_shape=jax.ShapeDtypeStruct((n, D_MODEL), F32),
        grid=(batch, tpb),
        in_specs=[rowblk, pl.BlockSpec((mem_len, 2 * D_MODEL), lambda bb, t: (bb, 0)),
                  full(wq), full(wo), full(g), full(b)],
        out_specs=rowblk,
        compiler_params=_cparams(("parallel", "parallel")),
        name="xattn_ln",
    )(x2, kv, wq, wo, g, b)


ROW_SLABS = D_MODEL // LANES


def _ffn_kernel(texp_ref, nvalid_ref, x_ref, w1_ref, w3_ref, w2_ref, g_ref, b_ref, o_ref, xb_sc, acc_sc,
                *, grouped):
    t = pl.program_id(0)
    f = pl.program_id(1)
    nf = pl.num_programs(1)

    @pl.when(t < nvalid_ref[0])
    def _():
        @pl.when(f == 0)
        def _():
            if grouped:
                for s in range(ROW_SLABS):
                    xb_sc[:, s * LANES:(s + 1) * LANES] = x_ref[:, s, :].astype(BF16)
            else:
                xb_sc[...] = x_ref[...].astype(BF16)
            acc_sc[...] = jnp.zeros_like(acc_sc)

        xb = xb_sc[...]
        h1 = _dot(xb, w1_ref[0])
        h3 = _dot(xb, w3_ref[0])
        acc_sc[...] += _dot((_silu(h1) * h3).astype(BF16), w2_ref[0])

        @pl.when(f == nf - 1)
        def _():
            if grouped:
                for s in range(ROW_SLABS):
                    o_ref[:, s, :] = acc_sc[:, s * LANES:(s + 1) * LANES]
            else:
                o_ref[...] = _layer_norm(DN_ALPHA * x_ref[...] + acc_sc[...], g_ref[...], b_ref[...])

    @pl.when((t >= nvalid_ref[0]) & (f == 0))
    def _():
        o_ref[...] = jnp.zeros_like(o_ref)


def _ffn(tile_expert, nvalid, x, w1, w3, w2, g, b, *, tm, tf, grouped):
    n = x.shape[0]
    nt = n // tm
    nf = D_FF // tf

    def tile_of(t, nv):
        return jnp.minimum(t, nv[0] - 1)

    def f_of(t, f, nv):
        return jnp.where(t < nv[0], f, nf - 1)

    if grouped:
        rowblk = pl.BlockSpec((tm, ROW_SLABS, LANES), lambda t, f, te, nv: (tile_of(t, nv), 0, 0))
        outblk = pl.BlockSpec((tm, ROW_SLABS, LANES), lambda t, f, te, nv: (t, 0, 0))
        out_shape = jax.ShapeDtypeStruct((n, ROW_SLABS, LANES), F32)
    else:
        rowblk = pl.BlockSpec((tm, D_MODEL), lambda t, f, te, nv: (tile_of(t, nv), 0))
        outblk = pl.BlockSpec((tm, D_MODEL), lambda t, f, te, nv: (t, 0))
        out_shape = jax.ShapeDtypeStruct((n, D_MODEL), F32)
    w13 = pl.BlockSpec((1, D_MODEL, tf), lambda t, f, te, nv: (te[tile_of(t, nv)], 0, f_of(t, f, nv)))
    w2s = pl.BlockSpec((1, tf, D_MODEL), lambda t, f, te, nv: (te[tile_of(t, nv)], f_of(t, f, nv), 0))
    vec = pl.BlockSpec((1, D_MODEL), lambda t, f, te, nv: (0, 0))
    return pl.pallas_call(
        functools.partial(_ffn_kernel, grouped=grouped),
        out_shape=out_shape,
        grid_spec=pltpu.PrefetchScalarGridSpec(
            num_scalar_prefetch=2,
            grid=(nt, nf),
            in_specs=[rowblk, w13, w13, w2s, vec, vec],
            out_specs=outblk,
            scratch_shapes=[pltpu.VMEM((tm, D_MODEL), BF16), pltpu.VMEM((tm, D_MODEL), F32)]),
        compiler_params=_cparams(("arbitrary", "arbitrary"), vmem_mib=56),
        name="swiglu_grouped" if grouped else "swiglu_ln",
    )(tile_expert, nvalid, x, w1, w3, w2, g, b)


def _router_kernel(x_ref, w_ref, route_ref, counts_ref, carry):
    i = pl.program_id(0)
    tm = x_ref.shape[0]

    @pl.when(i == 0)
    def _():
        carry[...] = jnp.zeros_like(carry)

    xh, xm, _ = _split3(x_ref[...])
    wh, wm, _ = _split3(w_ref[...])
    logits = _dot(xh, wh) + (_dot(xh, wm) + _dot(xm, wh))
    lane = lax.broadcasted_iota(jnp.int32, (tm, LANES), 1)
    logits = jnp.where(lane < N_EXPERTS, logits, -jnp.inf)
    m1 = jnp.max(logits, axis=-1, keepdims=True)
    i1 = jnp.min(jnp.where(logits == m1, lane, LANES), axis=-1, keepdims=True)
    rest = jnp.where(lane == i1, -jnp.inf, logits)
    m2 = jnp.max(rest, axis=-1, keepdims=True)
    i2 = jnp.min(jnp.where(rest == m2, lane, LANES), axis=-1, keepdims=True)
    e21 = jnp.exp(m2 - m1)
    g1 = 1.0 / (1.0 + e21)
    g2 = e21 / (1.0 + e21)
    hit1 = lane == i1
    hit2 = lane == i2
    onehot = jnp.where(hit1 | hit2, 1.0, 0.0).astype(BF16)
    r = lax.broadcasted_iota(jnp.int32, (tm, tm), 0)
    c = lax.broadcasted_iota(jnp.int32, (tm, tm), 1)
    strict = jnp.where(c < r, 1.0, 0.0).astype(BF16)
    before = _dot(strict, onehot) + carry[...]
    rank1 = jnp.sum(jnp.where(hit1, before, 0.0), axis=-1, keepdims=True)
    rank2 = jnp.sum(jnp.where(hit2, before, 0.0), axis=-1, keepdims=True)
    carry[...] = carry[...] + jnp.sum(onehot.astype(F32), axis=0, keepdims=True)
    out = jnp.where(lane == 0, i1.astype(F32), 0.0)
    out = jnp.where(lane == 1, i2.astype(F32), out)
    out = jnp.where(lane == 2, g1, out)
    out = jnp.where(lane == 3, g2, out)
    out = jnp.where(lane == 4, rank1, out)
    out = jnp.where(lane == 5, rank2, out)
    route_ref[...] = out
    counts_ref[...] = carry[...]


def _router(x2, wr, *, tm):
    n = x2.shape[0]
    return pl.pallas_call(
        _router_kernel,
        out_shape=(jax.ShapeDtypeStruct((n, LANES), F32), jax.ShapeDtypeStruct((1, LANES), F32)),
        grid=(n // tm,),
        in_specs=[pl.BlockSpec((tm, D_MODEL), lambda i: (i, 0)), pl.BlockSpec(wr.shape, lambda i: (0, 0))],
        out_specs=(pl.BlockSpec((tm, LANES), lambda i: (i, 0)), pl.BlockSpec((1, LANES), lambda i: (0, 0))),
        scratch_shapes=[pltpu.VMEM((1, LANES), F32)],
        compiler_params=_cparams(("arbitrary",)),
        name="moe_router",
    )(x2, wr)


POS_ROWS = 8


ROW_UNROLL = 8


def _for_each_row_copy(row_copy, action):
    tokens_per_pos_row = LANES // TOP_K
    for prow in range(POS_ROWS):
        def body(c, carry):
            for k in range(TOP_K):
                copy = row_copy(prow * tokens_per_pos_row + c, (prow, TOP_K * c + k), k)
                getattr(copy, action)()
            return carry

        lax.fori_loop(0, tokens_per_pos_row, body, 0, unroll=ROW_UNROLL)


def _dispatch_kernel(pend_ref, padded_ref, pos_hbm, x_ref, xs_hbm, pos_smem, stage, zeros_vmem, sem_pos, sem_rows,
                     *, row_tm, pad_tm):
    i = pl.program_id(0)

    def zero_copy(start):
        return pltpu.make_async_copy(zeros_vmem, xs_hbm.at[pl.ds(start, pad_tm)], sem_rows)

    @pl.when(i == 0)
    def _():
        zeros_vmem[...] = jnp.zeros_like(zeros_vmem)
        used = pend_ref[N_EXPERTS - 1]
        fills = [(padded_ref[e] > 0, pend_ref[e] - pad_tm) for e in range(N_EXPERTS)]
        fills += [(used + e * pad_tm < xs_hbm.shape[0], used + e * pad_tm) for e in range(N_EXPERTS)]
        for cond, start in fills:
            @pl.when(cond)
            def _():
                zero_copy(start).start()
        for cond, start in fills:
            @pl.when(cond)
            def _():
                zero_copy(start).wait()

    pos_copy = pltpu.make_async_copy(pos_hbm.at[i], pos_smem, sem_pos)
    pos_copy.start()
    for s in range(ROW_SLABS):
        stage[:, s, :] = x_ref[:, s * LANES:(s + 1) * LANES]
    pos_copy.wait()

    def row_copy(r, entry, k):
        return pltpu.make_async_copy(stage.at[r], xs_hbm.at[pos_smem[entry]], sem_rows)

    _for_each_row_copy(row_copy, "start")
    _for_each_row_copy(row_copy, "wait")


def _dispatch(pend, padded, pos3, x2, *, cap, row_tm, pad_tm):
    n = x2.shape[0]
    assert TOP_K * row_tm == POS_ROWS * LANES
    return pl.pallas_call(
        functools.partial(_dispatch_kernel, row_tm=row_tm, pad_tm=pad_tm),
        out_shape=jax.ShapeDtypeStruct((cap, ROW_SLABS, LANES), F32),
        grid_spec=pltpu.PrefetchScalarGridSpec(
            num_scalar_prefetch=2,
            grid=(n // row_tm,),
            in_specs=[pl.BlockSpec(memory_space=pl.ANY),
                      pl.BlockSpec((row_tm, D_MODEL), lambda i, pe, pa: (i, 0))],
            out_specs=pl.BlockSpec(memory_space=pl.ANY),
            scratch_shapes=[pltpu.SMEM((POS_ROWS, LANES), jnp.int32),
                            pltpu.VMEM((row_tm, ROW_SLABS, LANES), F32),
                            pltpu.VMEM((pad_tm, ROW_SLABS, LANES), F32),
                            pltpu.SemaphoreType.DMA(()),
                            pltpu.SemaphoreType.DMA(())]),
        compiler_params=_cparams(("arbitrary",)),
        name="moe_dispatch",
    )(pend, padded, pos3, x2)


def _combine_kernel(pos_hbm, x_ref, route_ref, ys_hbm, g_ref, b_ref, o_ref, pos_smem, rows0, rows1, ff_sc,
                    sem_pos, sem_rows, *, row_tm):
    i = pl.program_id(0)
    pos_copy = pltpu.make_async_copy(pos_hbm.at[i], pos_smem, sem_pos)
    pos_copy.start()
    pos_copy.wait()
    rows = (rows0, rows1)

    def row_copy(r, entry, k):
        return pltpu.make_async_copy(ys_hbm.at[pos_smem[entry]], rows[k].at[r], sem_rows)

    _for_each_row_copy(row_copy, "start")
    _for_each_row_copy(row_copy, "wait")
    g1 = jnp.broadcast_to(route_ref[:, 2:3], (row_tm, LANES))
    g2 = jnp.broadcast_to(route_ref[:, 3:4], (row_tm, LANES))
    for s in range(ROW_SLABS):
        ff_sc[:, s * LANES:(s + 1) * LANES] = rows0[:, s, :] * g1 + rows1[:, s, :] * g2
    o_ref[...] = _layer_norm(DN_ALPHA * x_ref[...] + ff_sc[...], g_ref[...], b_ref[...])


def _combine(pos3, x2, route, ys, g, b, *, row_tm):
    n = x2.shape[0]
    assert TOP_K * row_tm == POS_ROWS * LANES
    rowblk = lambda w: pl.BlockSpec((row_tm, w), lambda i: (i, 0))
    vec = pl.BlockSpec((1, D_MODEL), lambda i: (0, 0))
    return pl.pallas_call(
        functools.partial(_combine_kernel, row_tm=row_tm),
        out_shape=jax.ShapeDtypeStruct((n, D_MODEL), F32),
        grid=(n // row_tm,),
        in_specs=[pl.BlockSpec(memory_space=pl.ANY), rowblk(D_MODEL), rowblk(LANES),
                  pl.BlockSpec(memory_space=pl.ANY), vec, vec],
        out_specs=rowblk(D_MODEL),
        scratch_shapes=[pltpu.SMEM((POS_ROWS, LANES), jnp.int32),
                        pltpu.VMEM((row_tm, ROW_SLABS, LANES), F32),
                        pltpu.VMEM((row_tm, ROW_SLABS, LANES), F32),
                        pltpu.VMEM((row_tm, D_MODEL), F32),
                        pltpu.SemaphoreType.DMA(()),
                        pltpu.SemaphoreType.DMA(())],
        compiler_params=_cparams(("arbitrary",)),
        name="moe_combine_ln",
    )(pos3, x2, route, ys, g, b)


def _row(v):
    return v.reshape(1, -1).astype(F32)


def _repeat_heads(v, width):
    return jnp.repeat(v.astype(F32), width).reshape(1, -1)


def _pad_rows16(v8, offset):
    out = jnp.zeros((16,), F32).at[offset:offset + v8.shape[0]].set(v8.astype(F32))
    return out.reshape(16, 1)


def _mixer_weights(w_in_l):
    offs = [0]
    for s in IN_SIZES:
        offs.append(offs[-1] + s)
    seg = lambda k: w_in_l[:, offs[k]:offs[k + 1]]
    wz, wxbc, wdt, wq, wk, wv, wf, wp = (seg(k) for k in range(8))
    wm = jnp.concatenate([wxbc, wz, wp, jnp.repeat(wdt, SSD_HEAD_DIM, axis=1)], axis=1).astype(BF16)
    ws = jnp.concatenate([wdt, wf, jnp.zeros((D_MODEL, LANES - 12), F32)], axis=1).astype(BF16)
    wst = ws[:, 0:16].T

    def slots(w):
        w = w.reshape(D_MODEL, FOX_HEADS, FOX_HEAD_DIM)
        w = jnp.pad(w, ((0, 0), (0, 0), (0, FOX_SLOT - FOX_HEAD_DIM)))
        return w.reshape(D_MODEL, FOX_EXT)

    wq = wq * (FOX_HEAD_DIM ** -0.5 * LOG2E)
    return wm, slots(wk).astype(BF16), slots(wq).T.astype(BF16), slots(wv).T.astype(BF16), ws, wst


def _fox_constants():
    place = jnp.zeros((FOX_BIAS_TERMS, LANES, FOX_EXT), F32)
    qones = jnp.zeros((FOX_EXT, 1), F32)
    vones = jnp.zeros((FOX_EXT, 1), F32)
    for h in range(FOX_HEADS):
        base = h * FOX_SLOT + FOX_HEAD_DIM
        for term in range(FOX_BIAS_TERMS):
            place = place.at[term, SSD_HEADS + h, base + term].set(1.0)
        qones = qones.at[base:base + FOX_BIAS_TERMS].set(1.0)
        vones = vones.at[base:base + FOX_ONES_ROWS].set(1.0)
    return place.astype(BF16), qones, vones


def _token_mixing(x2, p, layer, *, batch, seq_len):
    wm, wkx, wqt, wvt, ws, wst = _mixer_weights(p["w_in"][layer])
    fb = p["fox_f_bias"][layer].astype(F32)
    fbc = jnp.zeros((LANES,), F32).at[SSD_HEADS:SSD_HEADS + FOX_HEADS].set(fb).reshape(1, LANES)
    place, qones, vones = _fox_constants()
    xbc, z, pool_in, dte, kx, qt, vt, rowsp = _in_proj(
        x2, wm, wkx, wqt, wvt, ws, wst, fbc, place, qones, vones, seq_len=seq_len, tm=min(IN_TM, seq_len))

    y_ssd = _ssd(xbc, z, dte, rowsp,
                 p["ssm_conv_w"][layer].astype(F32), _row(p["ssm_conv_b"][layer]),
                 _repeat_heads(p["ssm_dt_bias"][layer], SSD_HEAD_DIM),
                 _repeat_heads(p["ssm_a_log"][layer], SSD_HEAD_DIM),
                 _repeat_heads(p["ssm_d"][layer], SSD_HEAD_DIM),
                 _row(p["ssm_norm_w"][layer]),
                 _pad_rows16(p["ssm_dt_bias"][layer], 0), _pad_rows16(p["ssm_a_log"][layer], 0),
                 batch=batch, seq_len=seq_len, tb=min(SSD_TB, seq_len), chunk=SSD_L)

    y_fox = _fox(kx, qt, vt, batch=batch, seq_len=seq_len, tq=min(FOX_TQ, seq_len), tk=min(FOX_TK, seq_len))

    wbd = jax.scipy.linalg.block_diag(*[p["pool_w"][layer][g] for g in range(len(POOL_WINDOWS))]).astype(BF16)
    y_pool = _pool(pool_in, wbd, _row(p["pool_b"][layer]), _row(p["pool_scale"][layer]),
                   batch=batch, seq_len=seq_len, tb=min(POOL_TB, seq_len))

    return _out_proj(x2, y_ssd, y_fox, y_pool, p["w_out"][layer].astype(BF16),
                     _row(p["ln1_g"][layer]), _row(p["ln1_b"][layer]), tm=min(OUT_TM, seq_len))


def _cross_attention(x2, mem2, p, layer, *, batch, seq_len, mem_len):
    wkv = jnp.concatenate([p["xa_wk"][layer], p["xa_wv"][layer]], axis=1).astype(BF16)
    kv = _kv_proj(mem2, wkv)
    wq = (p["xa_wq"][layer] * (XATTN_HEAD_DIM ** -0.5)).astype(BF16)
    return _xattn(x2, kv, wq, p["xa_wo"][layer].astype(BF16), _row(p["ln2_g"][layer]), _row(p["ln2_b"][layer]),
                  batch=batch, seq_len=seq_len, mem_len=mem_len, tm=min(XA_TM, seq_len))


def _dense_ffn(x2, p, layer):
    j = layer // 2
    n = x2.shape[0]
    tm = min(FFN_TM, n)
    tile_expert = jnp.zeros((n // tm,), jnp.int32)
    nvalid = jnp.full((1,), n // tm, jnp.int32)
    return _ffn(tile_expert, nvalid, x2, p["ffn_w1"][j][None].astype(BF16), p["ffn_w3"][j][None].astype(BF16),
                p["ffn_w2"][j][None].astype(BF16), _row(p["ln3_g"][layer]), _row(p["ln3_b"][layer]),
                tm=tm, tf=FFN_TF, grouped=False)


def _moe_ffn(x2, p, layer):
    j = layer // 2
    n = x2.shape[0]
    tm = min(MOE_TM, n)
    row_tm = min(ROW_TM, n)
    wr = jnp.concatenate([p["router_w"][j].astype(F32), jnp.zeros((D_MODEL, LANES - N_EXPERTS), F32)], axis=1)
    route, counts = _router(x2, wr, tm=min(ROUTE_TM, n))

    counts = counts[0, :N_EXPERTS].astype(jnp.int32)
    padded = (counts + tm - 1) // tm * tm
    pend = jnp.cumsum(padded)
    pstart = pend - padded
    experts = route[:, 0:TOP_K].astype(jnp.int32)
    ranks = route[:, 4:4 + TOP_K].astype(jnp.int32)
    pos = (pstart[experts] + ranks).astype(jnp.int32)
    pos3 = pos.reshape(n // row_tm, POS_ROWS, LANES)
    cap = n * TOP_K + N_EXPERTS * tm
    ntiles = cap // tm
    tile_start = jnp.arange(ntiles, dtype=jnp.int32) * tm
    tile_expert = jnp.minimum(jnp.sum((pend[None, :] <= tile_start[:, None]).astype(jnp.int32), axis=1),
                              N_EXPERTS - 1).astype(jnp.int32)
    nvalid = (pend[-1:] // tm).astype(jnp.int32)

    xs = _dispatch(pend.astype(jnp.int32), padded.astype(jnp.int32), pos3, x2, cap=cap, row_tm=row_tm, pad_tm=tm)
    ys = _ffn(tile_expert, nvalid, xs, p["moe_w1"][j].astype(BF16), p["moe_w3"][j].astype(BF16),
              p["moe_w2"][j].astype(BF16), _row(p["ln3_g"][layer]), _row(p["ln3_b"][layer]),
              tm=tm, tf=FFN_TF, grouped=True)
    return _combine(pos3, x2, route, ys, _row(p["ln3_g"][layer]), _row(p["ln3_b"][layer]), row_tm=row_tm)


def _forward(x, mem, p):
    batch, seq_len, _ = x.shape
    mem_len = mem.shape[1]
    x2 = x.reshape(batch * seq_len, D_MODEL).astype(F32)
    mem2 = mem.reshape(batch * mem_len, D_MODEL).astype(F32)
    for layer in range(DEPTH):
        x2 = _token_mixing(x2, p, layer, batch=batch, seq_len=seq_len)
        x2 = _cross_attention(x2, mem2, p, layer, batch=batch, seq_len=seq_len, mem_len=mem_len)
        x2 = _dense_ffn(x2, p, layer) if layer % 2 == 0 else _moe_ffn(x2, p, layer)
    return x2.reshape(batch, seq_len, D_MODEL)


def kernel(x, mem, w_in, ssm_conv_w, ssm_conv_b, ssm_dt_bias, ssm_a_log, ssm_d, ssm_norm_w, fox_f_bias, pool_w, pool_b, pool_scale, w_out, ln1_g, ln1_b, xa_wq, xa_wk, xa_wv, xa_wo, ln2_g, ln2_b, ffn_w1, ffn_w3, ffn_w2, router_w, moe_w1, moe_w3, moe_w2, ln3_g, ln3_b):
    p = dict(w_in=w_in, ssm_conv_w=ssm_conv_w, ssm_conv_b=ssm_conv_b, ssm_dt_bias=ssm_dt_bias,
             ssm_a_log=ssm_a_log, ssm_d=ssm_d, ssm_norm_w=ssm_norm_w, fox_f_bias=fox_f_bias, pool_w=pool_w,
             pool_b=pool_b, pool_scale=pool_scale, w_out=w_out, ln1_g=ln1_g, ln1_b=ln1_b, xa_wq=xa_wq,
             xa_wk=xa_wk, xa_wv=xa_wv, xa_wo=xa_wo, ln2_g=ln2_g, ln2_b=ln2_b, ffn_w1=ffn_w1, ffn_w3=ffn_w3,
             ffn_w2=ffn_w2, router_w=router_w, moe_w1=moe_w1, moe_w3=moe_w3, moe_w2=moe_w2, ln3_g=ln3_g,
             ln3_b=ln3_b)
    return _forward(x, mem, p)
```

```python
import functools

import jax
import jax.numpy as jnp
import numpy as np
from jax import lax
from jax.experimental import pallas as pl
from jax.experimental.pallas import tpu as pltpu

F32 = jnp.float32
BF16 = jnp.bfloat16

D_MODEL = 1024
DEPTH = 4
SSD_WIDTH = 512
SSD_HEAD_DIM = 64
SSD_HEADS = 8
SSD_GROUPS = 2
SSD_STATE = 128
SSD_CONV = 4
SSD_CONV_CH = 1024
FOX_WIDTH = 256
FOX_HEAD_DIM = 64
FOX_HEADS = 4
POOL_WIDTH = 256
POOL_WINDOWS = (2, 4, 8, 16)
POOL_GROUP_DIM = 64
IN_SIZES = (512, 1024, 8, 256, 256, 256, 4, 256)
XATTN_HEADS = 4
XATTN_HEAD_DIM = 256
D_FF = 3584
N_EXPERTS = 8
TOP_K = 2
DN_ALPHA = (2 * DEPTH) ** 0.25
LN_EPS = 1e-5
RMS_EPS = 1e-5

LANES = 128
MIB = 1024 * 1024

IN_TM = 512
SSD_TB = 512
SSD_L = 128
FOX_TQ = 1024
FOX_TK = 512
FOX_QW = 128
FOX_LOOKAHEAD = 4
POOL_TB = 512
POOL_PIECE = 128
OUT_TM = 512
XA_TM = 512
FFN_TM = 1024
FFN_TF = 512
MOE_TM = 1024
ROUTE_TM = 512
ROW_TM = 512


def _cparams(sem, vmem_mib=48):
    return pltpu.CompilerParams(dimension_semantics=sem, vmem_limit_bytes=vmem_mib * MIB)


def _dot(a, b):
    return jnp.dot(a, b, preferred_element_type=F32)


def _dot_nt(a, b):
    return lax.dot_general(a, b, (((1,), (1,)), ((), ())), preferred_element_type=F32)


def _dot_tn(a, b):
    return lax.dot_general(a, b, (((0,), (0,)), ((), ())), preferred_element_type=F32)


def _split3(x):
    hi = x.astype(BF16)
    r1 = x - hi.astype(F32)
    mid = r1.astype(BF16)
    lo = (r1 - mid.astype(F32)).astype(BF16)
    return hi, mid, lo


def _silu(x):
    return x / (1.0 + jnp.exp(-x))


def _softplus(x):
    return jnp.maximum(x, 0.0) + jnp.log1p(jnp.exp(-jnp.abs(x)))


def _log_sigmoid(x):
    return jnp.minimum(x, 0.0) - jnp.log1p(jnp.exp(-jnp.abs(x)))


def _layer_norm(v, g, b):
    mu = jnp.mean(v, axis=-1, keepdims=True)
    d = v - mu
    var = jnp.mean(d * d, axis=-1, keepdims=True)
    return d * lax.rsqrt(var + LN_EPS) * g + b


def _tri_ones(n, lower):
    r = lax.broadcasted_iota(jnp.int32, (n, n), 0)
    c = lax.broadcasted_iota(jnp.int32, (n, n), 1)
    m = (c <= r) if lower else (r <= c)
    return jnp.where(m, 1.0, 0.0).astype(BF16)


FOX_SLOT = LANES
FOX_EXT = FOX_HEADS * FOX_SLOT
FOX_BIAS_TERMS = 3
FOX_ONES_ROWS = 16
LOG2E = 1.4426950408889634


def _in_proj_kernel(x_ref, wm_ref, wkx_ref, wqt_ref, wvt_ref, ws_ref, wst_ref, fbc_ref, place_ref, qones_ref,
                    vones_ref, xbc_ref, z_ref, pool_ref, dte_ref, kx_ref, qt_ref, vt_ref, rows_ref,
                    carry_c, *, tiles_per_batch):
    i = pl.program_id(0)
    tm = x_ref.shape[0]
    xb = x_ref[...].astype(BF16)

    xbc_ref[...] = _dot(xb, wm_ref[:, 0:1024])
    z_ref[...] = _dot(xb, wm_ref[:, 1024:1536])
    pool_ref[...] = _dot(xb, wm_ref[:, 1536:1792])
    dte_ref[...] = _dot(xb, wm_ref[:, 1792:2304])
    qt_ref[...] = (_dot_nt(wqt_ref[...], xb) + qones_ref[...]).astype(BF16)
    vt_ref[...] = (_dot_nt(wvt_ref[...], xb) + vones_ref[...]).astype(BF16)
    rows_ref[...] = _dot_nt(wst_ref[...], xb)

    @pl.when(i % tiles_per_batch == 0)
    def _():
        carry_c[...] = jnp.zeros_like(carry_c)

    small_c = _dot(xb, ws_ref[...])
    c3 = _split3(_log_sigmoid(small_c + fbc_ref[...]))
    tri = _tri_ones(tm, lower=True)
    cs_c = _dot(tri, c3[0]) + _dot(tri, c3[1]) + _dot(tri, c3[2]) + carry_c[...]
    carry_c[...] = cs_c[tm - 1:tm, :]
    n3 = _split3(cs_c * (-LOG2E))
    kx = _dot(xb, wkx_ref[...])
    for term in range(FOX_BIAS_TERMS):
        kx = kx + _dot(n3[term], place_ref[term])
    kx_ref[...] = kx.astype(BF16)


def _in_proj(x2, wm, wkx, wqt, wvt, ws, wst, fbc, place, qones, vones, *, seq_len, tm):
    n = x2.shape[0]
    grid = (n // tm,)
    full = lambda a: pl.BlockSpec(a.shape, lambda i: (0,) * a.ndim)
    rowblk = lambda w: pl.BlockSpec((tm, w), lambda i: (i, 0))
    colblk = lambda h: pl.BlockSpec((h, tm), lambda i: (0, i))
    out_shape = (
        jax.ShapeDtypeStruct((n, 1024), F32),
        jax.ShapeDtypeStruct((n, 512), F32),
        jax.ShapeDtypeStruct((n, 256), F32),
        jax.ShapeDtypeStruct((n, 512), F32),
        jax.ShapeDtypeStruct((n, FOX_EXT), BF16),
        jax.ShapeDtypeStruct((FOX_EXT, n), BF16),
        jax.ShapeDtypeStruct((FOX_EXT, n), BF16),
        jax.ShapeDtypeStruct((16, n), F32),
    )
    out_specs = (rowblk(1024), rowblk(512), rowblk(256), rowblk(512), rowblk(FOX_EXT),
                 colblk(FOX_EXT), colblk(FOX_EXT), colblk(16))
    args = (x2, wm, wkx, wqt, wvt, ws, wst, fbc, place, qones, vones)
    return pl.pallas_call(
        functools.partial(_in_proj_kernel, tiles_per_batch=seq_len // tm),
        out_shape=out_shape,
        grid=grid,
        in_specs=[rowblk(D_MODEL)] + [full(a) for a in args[1:]],
        out_specs=out_specs,
        scratch_shapes=[pltpu.VMEM((1, LANES), F32)],
        compiler_params=_cparams(("arbitrary",)),
        name="in_proj",
    )(*args)


def _ssd_kernel(xbc_ref, z_ref, dte_ref, rows_ref, cw_ref, cb_ref, dtb_ref, alog_ref, dsk_ref, nw_ref,
                dtbr_ref, alogr_ref, y_ref, tail, ubuf, xc, state, *, chunk):
    t = pl.program_id(1)
    tb = xbc_ref.shape[0]
    L = chunk
    CONV_PIECE = 64
    HALO = 8

    @pl.when(t == 0)
    def _():
        tail[...] = jnp.zeros_like(tail)
        state[...] = jnp.zeros_like(state)

    ubuf[0:HALO, :] = tail[...]
    ubuf[HALO:HALO + tb, :] = xbc_ref[...]
    tail[...] = xbc_ref[tb - HALO:tb, :]
    for r0 in range(0, tb, CONV_PIECE):
        acc = jnp.broadcast_to(cb_ref[...], (CONV_PIECE, SSD_CONV_CH))
        for k in range(SSD_CONV):
            off = r0 + HALO - (SSD_CONV - 1) + k
            acc = acc + cw_ref[k:k + 1, :] * ubuf[off:off + CONV_PIECE, :]
        xc[r0:r0 + CONV_PIECE, :] = _silu(acc)

    a_e = -jnp.exp(alog_ref[...])
    a_r = -jnp.exp(alogr_ref[...])
    tri = _tri_ones(L, lower=True)
    upp = _tri_ones(L, lower=False)
    rr = lax.broadcasted_iota(jnp.int32, (L, L), 0)
    cc = lax.broadcasted_iota(jnp.int32, (L, L), 1)
    causal = cc <= rr
    lane = lax.broadcasted_iota(jnp.int32, (L, LANES), 1)
    left = lane < SSD_HEAD_DIM
    HG = SSD_HEADS // SSD_GROUPS
    GW = SSD_WIDTH // SSD_GROUPS

    def body(c, carry):
        r0 = pl.multiple_of(c * L, L)
        rows = pl.ds(r0, L)
        xs = xc[rows, 0:SSD_WIDTH]
        dt_e = _softplus(dte_ref[rows, :] + dtb_ref[...])
        a3 = _split3(dt_e * a_e)
        acs_e = _dot(tri, a3[0]) + _dot(tri, a3[1]) + _dot(tri, a3[2])
        dt_r = _softplus(rows_ref[:, rows] + dtbr_ref[...])
        ar3 = _split3(dt_r * a_r)
        acs_r = _dot(ar3[0], upp) + _dot(ar3[1], upp) + _dot(ar3[2], upp)
        total = acs_e[L - 1:L, :]
        x_dt = xs * dt_e
        xb = x_dt.astype(BF16)
        xd = (x_dt * jnp.exp(total - acs_e)).astype(BF16)
        eacs = jnp.exp(acs_e)
        y_parts = []
        for g in range(SSD_GROUPS):
            bm = xc[rows, SSD_WIDTH + g * SSD_STATE:SSD_WIDTH + (g + 1) * SSD_STATE].astype(BF16)
            cm = xc[rows, SSD_WIDTH + (SSD_GROUPS + g) * SSD_STATE:
                    SSD_WIDTH + (SSD_GROUPS + g + 1) * SSD_STATE].astype(BF16)
            cb = _dot_nt(cm, bm)
            for pair in range(HG // 2):
                lo = g * GW + pair * LANES
                x_pair = xb[:, lo:lo + LANES]
                outs = []
                for sub in range(2):
                    h = g * HG + pair * 2 + sub
                    seg = acs_e[:, h * SSD_HEAD_DIM:h * SSD_HEAD_DIM + 1] - acs_r[h:h + 1, :]
                    decay = jnp.exp(jnp.where(causal, seg, -jnp.inf))
                    outs.append(_dot((cb * decay).astype(BF16), x_pair))
                y_parts.append(jnp.where(left, outs[0], outs[1]))
            s_prev = state[g]
            y_off = _dot(cm, s_prev.astype(BF16)) * eacs[:, g * GW:(g + 1) * GW]
            y_parts[-2] = y_parts[-2] + y_off[:, 0:LANES]
            y_parts[-1] = y_parts[-1] + y_off[:, LANES:2 * LANES]
            state[g] = s_prev * jnp.exp(total[:, g * GW:(g + 1) * GW]) + _dot_tn(bm, xd[:, g * GW:(g + 1) * GW])
        y = jnp.concatenate(y_parts, axis=1) + xs * dsk_ref[...]
        y = y * _silu(z_ref[rows, :])
        outs = []
        for g in range(SSD_GROUPS):
            yg = y[:, g * GW:(g + 1) * GW]
            ms = jnp.mean(yg * yg, axis=-1, keepdims=True)
            outs.append(yg * lax.rsqrt(ms + RMS_EPS))
        y_ref[rows, :] = (jnp.concatenate(outs, axis=1) * nw_ref[...]).astype(y_ref.dtype)
        return carry

    lax.fori_loop(0, tb // L, body, 0)


def _ssd(xbc, z, dte, rowsp, cw, cb, dtb_e, alog_e, dsk_e, nw, dtb_r, alog_r, *, batch, seq_len, tb, chunk):
    n = xbc.shape[0]
    tpb = seq_len // tb
    full = lambda a: pl.BlockSpec(a.shape, lambda b, t: (0,) * a.ndim)
    rowblk = lambda w: pl.BlockSpec((tb, w), lambda b, t: (b * tpb + t, 0))
    return pl.pallas_call(
        functools.partial(_ssd_kernel, chunk=chunk),
        out_shape=jax.ShapeDtypeStruct((n, SSD_WIDTH), BF16),
        grid=(batch, tpb),
        in_specs=[rowblk(SSD_CONV_CH), rowblk(SSD_WIDTH), rowblk(SSD_WIDTH),
                  pl.BlockSpec((16, tb), lambda b, t: (0, b * tpb + t)),
                  full(cw), full(cb), full(dtb_e), full(alog_e), full(dsk_e), full(nw),
                  full(dtb_r), full(alog_r)],
        out_specs=rowblk(SSD_WIDTH),
        scratch_shapes=[pltpu.VMEM((8, SSD_CONV_CH), F32),
                        pltpu.VMEM((tb + 8, SSD_CONV_CH), F32),
                        pltpu.VMEM((tb, SSD_CONV_CH), F32),
                        pltpu.VMEM((SSD_GROUPS, SSD_STATE, SSD_WIDTH // SSD_GROUPS), F32)],
        compiler_params=_cparams(("parallel", "arbitrary")),
        name="ssd_mixer",
    )(xbc, z, dte, rowsp, cw, cb, dtb_e, alog_e, dsk_e, nw, dtb_r, alog_r)


def _fox_kernel(qi_ref, kj_ref, kx_ref, qt_ref, vt_ref, o_ref, m_sc, acc_sc, *, ratio, qw):
    step_id = pl.program_id(1)
    i = qi_ref[step_id]
    j = kj_ref[step_id]
    tk = kx_ref.shape[0]
    tq = qt_ref.shape[1]
    vrows = FOX_HEAD_DIM + FOX_ONES_ROWS

    @pl.when(j == 0)
    def _():
        m_sc[...] = jnp.full_like(m_sc, -jnp.inf)
        acc_sc[...] = jnp.zeros_like(acc_sc)

    def step(masked):
        units = [(h, c0) for h in range(FOX_HEADS) for c0 in range(0, tq, qw)]

        def score(unit):
            h, c0 = unit
            return _dot(kx_ref[:, h * FOX_SLOT:(h + 1) * FOX_SLOT],
                        qt_ref[h * FOX_SLOT:(h + 1) * FOX_SLOT, c0:c0 + qw])

        scores = [score(u) for u in units[:FOX_LOOKAHEAD]]
        for n_unit, (h, c0) in enumerate(units):
            if n_unit + FOX_LOOKAHEAD < len(units):
                scores.append(score(units[n_unit + FOX_LOOKAHEAD]))
            st = scores[n_unit]
            lo = h * FOX_SLOT
            if masked:
                key = j * tk + lax.broadcasted_iota(jnp.int32, (tk, qw), 0)
                qry = i * tq + c0 + lax.broadcasted_iota(jnp.int32, (tk, qw), 1)
                st = jnp.where(key <= qry, st, -jnp.inf)
            m_prev = m_sc[h, :, c0:c0 + qw]
            m_new = jnp.maximum(m_prev, jnp.max(st, axis=0, keepdims=True))
            alpha = jnp.exp2(m_prev - m_new)
            p = jnp.exp2(st - m_new).astype(BF16)
            acc_sc[h, :, c0:c0 + qw] = alpha * acc_sc[h, :, c0:c0 + qw] + _dot(vt_ref[lo:lo + vrows, :], p)
            m_sc[h, :, c0:c0 + qw] = m_new

    @pl.when(j < i * ratio)
    def _():
        step(False)

    @pl.when(j >= i * ratio)
    def _():
        step(True)

    @pl.when(j == (i + 1) * ratio - 1)
    def _():
        parts = [acc_sc[h, 0:FOX_HEAD_DIM, :] / acc_sc[h, FOX_HEAD_DIM:FOX_HEAD_DIM + 1, :]
                 for h in range(FOX_HEADS)]
        o_ref[...] = jnp.concatenate(parts, axis=0).T.astype(o_ref.dtype)


def _fox(kx, qt, vt, *, batch, seq_len, tq, tk):
    n = kx.shape[0]
    assert tq % tk == 0
    ratio = tq // tk
    nq = seq_len // tq
    nk = seq_len // tk
    pairs = [(i, j) for i in range(nq) for j in range((i + 1) * ratio)]
    qi = jnp.asarray([pq for pq, _ in pairs], jnp.int32)
    kj = jnp.asarray([pk for _, pk in pairs], jnp.int32)
    return pl.pallas_call(
        functools.partial(_fox_kernel, ratio=ratio, qw=min(FOX_QW, tq)),
        out_shape=jax.ShapeDtypeStruct((n, FOX_WIDTH), BF16),
        grid_spec=pltpu.PrefetchScalarGridSpec(
            num_scalar_prefetch=2,
            grid=(batch, len(pairs)),
            in_specs=[pl.BlockSpec((tk, FOX_EXT), lambda b, s, qi, kj: (b * nk + kj[s], 0)),
                      pl.BlockSpec((FOX_EXT, tq), lambda b, s, qi, kj: (0, b * nq + qi[s])),
                      pl.BlockSpec((FOX_EXT, tk), lambda b, s, qi, kj: (0, b * nk + kj[s]))],
            out_specs=pl.BlockSpec((tq, FOX_WIDTH), lambda b, s, qi, kj: (b * nq + qi[s], 0)),
            scratch_shapes=[pltpu.VMEM((FOX_HEADS, 1, tq), F32),
                            pltpu.VMEM((FOX_HEADS, FOX_HEAD_DIM + FOX_ONES_ROWS, tq), F32)]),
        compiler_params=_cparams(("parallel", "arbitrary")),
        name="fox_attention",
    )(qi, kj, kx, qt, vt)


POOL_HALO = 16


def _pool_kernel(u_ref, w_ref, b_ref, sc_ref, y_ref, tail, ubuf):
    t = pl.program_id(1)
    tb = u_ref.shape[0]
    P = POOL_PIECE

    @pl.when(t == 0)
    def _():
        tail[...] = jnp.zeros_like(tail)

    ubuf[0:POOL_HALO, :] = tail[...]
    ubuf[POOL_HALO:POOL_HALO + tb, :] = u_ref[...]
    tail[...] = u_ref[tb - POOL_HALO:tb, :]
    group = lax.broadcasted_iota(jnp.int32, (P, POOL_WIDTH), 1) // POOL_GROUP_DIM
    for p0 in range(0, tb, P):
        w1 = ubuf[p0:p0 + P + POOL_HALO, :]
        n1 = P + POOL_HALO
        a2 = w1[1:n1] + w1[0:n1 - 1]
        a4 = a2[2:n1 - 1] + a2[0:n1 - 3]
        a8 = a4[4:n1 - 3] + a4[0:n1 - 7]
        a16 = a8[8:n1 - 7] + a8[0:n1 - 15]
        sums = (a2[15:15 + P], a4[13:13 + P], a8[9:9 + P], a16[1:1 + P])
        tpos = t * tb + p0 + lax.broadcasted_iota(jnp.int32, (P, 1), 0) + 1
        pooled = jnp.zeros((P, POOL_WIDTH), F32)
        for g, win in enumerate(POOL_WINDOWS):
            cnt = jnp.minimum(tpos, win).astype(F32)
            pooled = jnp.where(group == g, sums[g] / cnt, pooled)
        pooled = pooled - w1[POOL_HALO:POOL_HALO + P]
        y = _dot(pooled.astype(BF16), w_ref[...]) + b_ref[...]
        y_ref[p0:p0 + P, :] = (y * sc_ref[...]).astype(y_ref.dtype)


def _pool(u, wbd, b, sc, *, batch, seq_len, tb):
    n = u.shape[0]
    tpb = seq_len // tb
    full = lambda a: pl.BlockSpec(a.shape, lambda bb, t: (0,) * a.ndim)
    rowblk = pl.BlockSpec((tb, POOL_WIDTH), lambda bb, t: (bb * tpb + t, 0))
    return pl.pallas_call(
        _pool_kernel,
        out_shape=jax.ShapeDtypeStruct((n, POOL_WIDTH), BF16),
        grid=(batch, tpb),
        in_specs=[rowblk, full(wbd), full(b), full(sc)],
        out_specs=rowblk,
        scratch_shapes=[pltpu.VMEM((POOL_HALO, POOL_WIDTH), F32),
                        pltpu.VMEM((tb + POOL_HALO, POOL_WIDTH), F32)],
        compiler_params=_cparams(("parallel", "arbitrary")),
        name="pool_mixer",
    )(u, wbd, b, sc)


def _out_proj_kernel(x_ref, ys_ref, yf_ref, yp_ref, w_ref, g_ref, b_ref, o_ref):
    mix = (_dot(ys_ref[...], w_ref[0:512, :]) + _dot(yf_ref[...], w_ref[512:768, :])
           + _dot(yp_ref[...], w_ref[768:1024, :]))
    o_ref[...] = _layer_norm(DN_ALPHA * x_ref[...] + mix, g_ref[...], b_ref[...])


def _out_proj(x2, ys, yf, yp, w, g, b, *, tm):
    n = x2.shape[0]
    full = lambda a: pl.BlockSpec(a.shape, lambda i: (0,) * a.ndim)
    rowblk = lambda wd: pl.BlockSpec((tm, wd), lambda i: (i, 0))
    return pl.pallas_call(
        _out_proj_kernel,
        out_shape=jax.ShapeDtypeStruct((n, D_MODEL), F32),
        grid=(n // tm,),
        in_specs=[rowblk(D_MODEL), rowblk(512), rowblk(256), rowblk(256), full(w), full(g), full(b)],
        out_specs=rowblk(D_MODEL),
        compiler_params=_cparams(("parallel",)),
        name="out_proj_ln",
    )(x2, ys, yf, yp, w, g, b)


def _kv_proj_kernel(m_ref, w_ref, o_ref):
    o_ref[...] = _dot(m_ref[...].astype(BF16), w_ref[...]).astype(o_ref.dtype)


def _kv_proj(mem2, wkv):
    m = mem2.shape[0]
    return pl.pallas_call(
        _kv_proj_kernel,
        out_shape=jax.ShapeDtypeStruct((m, wkv.shape[1]), BF16),
        grid=(1,),
        in_specs=[pl.BlockSpec(mem2.shape, lambda i: (0, 0)), pl.BlockSpec(wkv.shape, lambda i: (0, 0))],
        out_specs=pl.BlockSpec((m, wkv.shape[1]), lambda i: (0, 0)),
        compiler_params=_cparams(("arbitrary",)),
        name="xattn_kv_proj",
    )(mem2, wkv)


def _xattn_kernel(x_ref, kv_ref, wq_ref, wo_ref, g_ref, b_ref, o_ref):
    x = x_ref[...]
    q = _dot(x.astype(BF16), wq_ref[...]).astype(BF16)
    heads = []
    for h in range(XATTN_HEADS):
        lo = h * XATTN_HEAD_DIM
        k_h = kv_ref[:, lo:lo + XATTN_HEAD_DIM]
        v_h = kv_ref[:, D_MODEL + lo:D_MODEL + lo + XATTN_HEAD_DIM]
        s = _dot_nt(q[:, lo:lo + XATTN_HEAD_DIM], k_h)
        p = jnp.exp(s - jnp.max(s, axis=-1, keepdims=True))
        l = jnp.sum(p, axis=-1, keepdims=True)
        heads.append((_dot(p.astype(BF16), v_h) / l).astype(BF16))
    o = jnp.concatenate(heads, axis=1)
    xa = _dot(o, wo_ref[...])
    o_ref[...] = _layer_norm(DN_ALPHA * x + xa, g_ref[...], b_ref[...])


def _xattn(x2, kv, wq, wo, g, b, *, batch, seq_len, mem_len, tm):
    n = x2.shape[0]
    tpb = seq_len // tm
    full = lambda a: pl.BlockSpec(a.shape, lambda bb, t: (0,) * a.ndim)
    rowblk = pl.BlockSpec((tm, D_MODEL), lambda bb, t: (bb * tpb + t, 0))
    return pl.pallas_call(
        _xattn_kernel,
        out_shape=jax.ShapeDtypeStruct((n, D_MODEL), F32),
        grid=(batch, tpb),
        in_specs=[rowblk, pl.BlockSpec((mem_len, 2 * D_MODEL), lambda bb, t: (bb, 0)),
                  full(wq), full(wo), full(g), full(b)],
        out_specs=rowblk,
        compiler_params=_cparams(("parallel", "parallel")),
        name="xattn_ln",
    )(x2, kv, wq, wo, g, b)


ROW_SLABS = D_MODEL // LANES


def _ffn_kernel(texp_ref, nvalid_ref, x_ref, w1_ref, w3_ref, w2_ref, g_ref, b_ref, o_ref, xb_sc, acc_sc,
                *, grouped):
    t = pl.program_id(0)
    f = pl.program_id(1)
    nf = pl.num_programs(1)

    @pl.when(t < nvalid_ref[0])
    def _():
        @pl.when(f == 0)
        def _():
            xb_sc[...] = x_ref[...].reshape(xb_sc.shape).astype(BF16)
            acc_sc[...] = jnp.zeros_like(acc_sc)

        xb = xb_sc[...]
        h1 = _dot(xb, w1_ref[0])
        h3 = _dot(xb, w3_ref[0])
        acc_sc[...] += _dot((_silu(h1) * h3).astype(BF16), w2_ref[0])

        @pl.when(f == nf - 1)
        def _():
            if grouped:
                o_ref[...] = acc_sc[...].reshape(o_ref.shape)
            else:
                o_ref[...] = _layer_norm(DN_ALPHA * x_ref[...] + acc_sc[...], g_ref[...], b_ref[...])

    @pl.when((t >= nvalid_ref[0]) & (f == 0))
    def _():
        o_ref[...] = jnp.zeros_like(o_ref)


def _ffn(tile_expert, nvalid, x, w1, w3, w2, g, b, *, tm, tf, grouped):
    n = x.shape[0]
    nt = n // tm
    nf = D_FF // tf

    def tile_of(t, nv):
        return jnp.minimum(t, nv[0] - 1)

    def f_of(t, f, nv):
        return jnp.where(t < nv[0], f, nf - 1)

    if grouped:
        rowblk = pl.BlockSpec((tm, ROW_SLABS, LANES), lambda t, f, te, nv: (tile_of(t, nv), 0, 0))
        outblk = pl.BlockSpec((tm, ROW_SLABS, LANES), lambda t, f, te, nv: (t, 0, 0))
        out_shape = jax.ShapeDtypeStruct((n, ROW_SLABS, LANES), F32)
    else:
        rowblk = pl.BlockSpec((tm, D_MODEL), lambda t, f, te, nv: (tile_of(t, nv), 0))
        outblk = pl.BlockSpec((tm, D_MODEL), lambda t, f, te, nv: (t, 0))
        out_shape = jax.ShapeDtypeStruct((n, D_MODEL), F32)
    w13 = pl.BlockSpec((1, D_MODEL, tf), lambda t, f, te, nv: (te[tile_of(t, nv)], 0, f_of(t, f, nv)))
    w2s = pl.BlockSpec((1, tf, D_MODEL), lambda t, f, te, nv: (te[tile_of(t, nv)], f_of(t, f, nv), 0))
    vec = pl.BlockSpec((1, D_MODEL), lambda t, f, te, nv: (0, 0))
    return pl.pallas_call(
        functools.partial(_ffn_kernel, grouped=grouped),
        out_shape=out_shape,
        grid_spec=pltpu.PrefetchScalarGridSpec(
            num_scalar_prefetch=2,
            grid=(nt, nf),
            in_specs=[rowblk, w13, w13, w2s, vec, vec],
            out_specs=outblk,
            scratch_shapes=[pltpu.VMEM((tm, D_MODEL), BF16), pltpu.VMEM((tm, D_MODEL), F32)]),
        compiler_params=_cparams(("arbitrary", "arbitrary"), vmem_mib=56),
        name="swiglu_grouped" if grouped else "swiglu_ln",
    )(tile_expert, nvalid, x, w1, w3, w2, g, b)


def _router_kernel(x_ref, w_ref, route_ref, counts_ref, carry):
    i = pl.program_id(0)
    tm = x_ref.shape[0]

    @pl.when(i == 0)
    def _():
        carry[...] = jnp.zeros_like(carry)

    xh, xm, _ = _split3(x_ref[...])
    wh, wm, _ = _split3(w_ref[...])
    logits = _dot(xh, wh) + (_dot(xh, wm) + _dot(xm, wh))
    lane = lax.broadcasted_iota(jnp.int32, (tm, LANES), 1)
    logits = jnp.where(lane < N_EXPERTS, logits, -jnp.inf)
    m1 = jnp.max(logits, axis=-1, keepdims=True)
    i1 = jnp.min(jnp.where(logits == m1, lane, LANES), axis=-1, keepdims=True)
    rest = jnp.where(lane == i1, -jnp.inf, logits)
    m2 = jnp.max(rest, axis=-1, keepdims=True)
    i2 = jnp.min(jnp.where(rest == m2, lane, LANES), axis=-1, keepdims=True)
    e21 = jnp.exp(m2 - m1)
    g1 = 1.0 / (1.0 + e21)
    g2 = e21 / (1.0 + e21)
    hit1 = lane == i1
    hit2 = lane == i2
    onehot = jnp.where(hit1 | hit2, 1.0, 0.0).astype(BF16)
    r = lax.broadcasted_iota(jnp.int32, (tm, tm), 0)
    c = lax.broadcasted_iota(jnp.int32, (tm, tm), 1)
    strict = jnp.where(c < r, 1.0, 0.0).astype(BF16)
    before = _dot(strict, onehot) + carry[...]
    rank1 = jnp.sum(jnp.where(hit1, before, 0.0), axis=-1, keepdims=True)
    rank2 = jnp.sum(jnp.where(hit2, before, 0.0), axis=-1, keepdims=True)
    carry[...] = carry[...] + jnp.sum(onehot.astype(F32), axis=0, keepdims=True)
    out = jnp.where(lane == 0, i1.astype(F32), 0.0)
    out = jnp.where(lane == 1, i2.astype(F32), out)
    out = jnp.where(lane == 2, g1, out)
    out = jnp.where(lane == 3, g2, out)
    out = jnp.where(lane == 4, rank1, out)
    out = jnp.where(lane == 5, rank2, out)
    route_ref[...] = out
    counts_ref[...] = carry[...]


def _router(x2, wr, *, tm):
    n = x2.shape[0]
    return pl.pallas_call(
        _router_kernel,
        out_shape=(jax.ShapeDtypeStruct((n, LANES), F32), jax.ShapeDtypeStruct((1, LANES), F32)),
        grid=(n // tm,),
        in_specs=[pl.BlockSpec((tm, D_MODEL), lambda i: (i, 0)), pl.BlockSpec(wr.shape, lambda i: (0, 0))],
        out_specs=(pl.BlockSpec((tm, LANES), lambda i: (i, 0)), pl.BlockSpec((1, LANES), lambda i: (0, 0))),
        scratch_shapes=[pltpu.VMEM((1, LANES), F32)],
        compiler_params=_cparams(("arbitrary",)),
        name="moe_router",
    )(x2, wr)


POS_ROWS = 8


ROW_UNROLL = 8


def _for_each_row_copy(row_copy, action):
    tokens_per_pos_row = LANES // TOP_K
    for prow in range(POS_ROWS):
        def body(c, carry):
            for k in range(TOP_K):
                copy = row_copy(prow * tokens_per_pos_row + c, (prow, TOP_K * c + k), k)
                getattr(copy, action)()
            return carry

        lax.fori_loop(0, tokens_per_pos_row, body, 0, unroll=ROW_UNROLL)


def _dispatch_kernel(pend_ref, padded_ref, pos_hbm, x_ref, xs_hbm, pos_smem, stage, zeros_vmem, sem_pos, sem_rows,
                     *, row_tm, pad_tm):
    i = pl.program_id(0)

    def zero_copy(start):
        return pltpu.make_async_copy(zeros_vmem, xs_hbm.at[pl.ds(start, pad_tm)], sem_rows)

    @pl.when(i == 0)
    def _():
        zeros_vmem[...] = jnp.zeros_like(zeros_vmem)
        used = pend_ref[N_EXPERTS - 1]
        fills = [(padded_ref[e] > 0, pend_ref[e] - pad_tm) for e in range(N_EXPERTS)]
        fills += [(used + e * pad_tm < xs_hbm.shape[0], used + e * pad_tm) for e in range(N_EXPERTS)]
        for cond, start in fills:
            @pl.when(cond)
            def _():
                zero_copy(start).start()
        for cond, start in fills:
            @pl.when(cond)
            def _():
                zero_copy(start).wait()

    pos_copy = pltpu.make_async_copy(pos_hbm.at[i], pos_smem, sem_pos)
    pos_copy.start()
    stage[...] = x_ref[...].reshape(stage.shape)
    pos_copy.wait()

    def row_copy(r, entry, k):
        return pltpu.make_async_copy(stage.at[r], xs_hbm.at[pos_smem[entry]], sem_rows)

    _for_each_row_copy(row_copy, "start")
    _for_each_row_copy(row_copy, "wait")


def _dispatch(pend, padded, pos3, x2, *, cap, row_tm, pad_tm):
    n = x2.shape[0]
    assert TOP_K * row_tm == POS_ROWS * LANES
    return pl.pallas_call(
        functools.partial(_dispatch_kernel, row_tm=row_tm, pad_tm=pad_tm),
        out_shape=jax.ShapeDtypeStruct((cap, ROW_SLABS, LANES), F32),
        grid_spec=pltpu.PrefetchScalarGridSpec(
            num_scalar_prefetch=2,
            grid=(n // row_tm,),
            in_specs=[pl.BlockSpec(memory_space=pl.ANY),
                      pl.BlockSpec((row_tm, D_MODEL), lambda i, pe, pa: (i, 0))],
            out_specs=pl.BlockSpec(memory_space=pl.ANY),
            scratch_shapes=[pltpu.SMEM((POS_ROWS, LANES), jnp.int32),
                            pltpu.VMEM((row_tm, ROW_SLABS, LANES), F32),
                            pltpu.VMEM((pad_tm, ROW_SLABS, LANES), F32),
                            pltpu.SemaphoreType.DMA(()),
                            pltpu.SemaphoreType.DMA(())]),
        compiler_params=_cparams(("arbitrary",)),
        name="moe_dispatch",
    )(pend, padded, pos3, x2)


def _combine_kernel(pos_hbm, x_ref, route_ref, ys_hbm, g_ref, b_ref, o_ref, pos_smem, rows0, rows1,
                    sem_pos, sem_rows, *, row_tm):
    i = pl.program_id(0)
    pos_copy = pltpu.make_async_copy(pos_hbm.at[i], pos_smem, sem_pos)
    pos_copy.start()
    pos_copy.wait()
    rows = (rows0, rows1)

    def row_copy(r, entry, k):
        return pltpu.make_async_copy(ys_hbm.at[pos_smem[entry]], rows[k].at[r], sem_rows)

    _for_each_row_copy(row_copy, "start")
    _for_each_row_copy(row_copy, "wait")
    flat = (row_tm, D_MODEL)
    ff = rows0[...].reshape(flat) * route_ref[:, 2:3] + rows1[...].reshape(flat) * route_ref[:, 3:4]
    o_ref[...] = _layer_norm(DN_ALPHA * x_ref[...] + ff, g_ref[...], b_ref[...])


def _combine(pos3, x2, route, ys, g, b, *, row_tm):
    n = x2.shape[0]
    assert TOP_K * row_tm == POS_ROWS * LANES
    rowblk = lambda w: pl.BlockSpec((row_tm, w), lambda i: (i, 0))
    vec = pl.BlockSpec((1, D_MODEL), lambda i: (0, 0))
    return pl.pallas_call(
        functools.partial(_combine_kernel, row_tm=row_tm),
        out_shape=jax.ShapeDtypeStruct((n, D_MODEL), F32),
        grid=(n // row_tm,),
        in_specs=[pl.BlockSpec(memory_space=pl.ANY), rowblk(D_MODEL), rowblk(LANES),
                  pl.BlockSpec(memory_space=pl.ANY), vec, vec],
        out_specs=rowblk(D_MODEL),
        scratch_shapes=[pltpu.SMEM((POS_ROWS, LANES), jnp.int32),
                        pltpu.VMEM((row_tm, ROW_SLABS, LANES), F32),
                        pltpu.VMEM((row_tm, ROW_SLABS, LANES), F32),
                        pltpu.SemaphoreType.DMA(()),
                        pltpu.SemaphoreType.DMA(())],
        compiler_params=_cparams(("arbitrary",)),
        name="moe_combine_ln",
    )(pos3, x2, route, ys, g, b)


def _fox_constants():
    place = np.zeros((FOX_BIAS_TERMS, LANES, FOX_EXT), np.float32)
    qones = np.zeros((FOX_EXT, 1), np.float32)
    vones = np.zeros((FOX_EXT, 1), np.float32)
    for h in range(FOX_HEADS):
        base = h * FOX_SLOT + FOX_HEAD_DIM
        for term in range(FOX_BIAS_TERMS):
            place[term, SSD_HEADS + h, base + term] = 1.0
        qones[base:base + FOX_BIAS_TERMS] = 1.0
        vones[base:base + FOX_ONES_ROWS] = 1.0
    return jnp.asarray(place, BF16), jnp.asarray(qones), jnp.asarray(vones)


def _prepare(p):
    depth = p["w_in"].shape[0]
    offs = np.concatenate([[0], np.cumsum(IN_SIZES)])
    w_in = p["w_in"].astype(F32)
    wz, wxbc, wdt, wq, wk, wv, wf, wp = (w_in[:, :, offs[k]:offs[k + 1]] for k in range(len(IN_SIZES)))

    def slots(w):
        w = w.reshape(depth, D_MODEL, FOX_HEADS, FOX_HEAD_DIM)
        w = jnp.pad(w, ((0, 0), (0, 0), (0, 0), (0, FOX_SLOT - FOX_HEAD_DIM)))
        return w.reshape(depth, D_MODEL, FOX_EXT)

    def vec(v, width):
        return v.astype(F32).reshape(v.shape[0], 1, width)

    def per_head(v):
        return jnp.repeat(v.astype(F32), SSD_HEAD_DIM, axis=1).reshape(depth, 1, SSD_WIDTH)

    def head_rows(v):
        return jnp.pad(v.astype(F32), ((0, 0), (0, 16 - SSD_HEADS))).reshape(depth, 16, 1)

    ws = jnp.concatenate([wdt, wf, jnp.zeros((depth, D_MODEL, LANES - SSD_HEADS - FOX_HEADS), F32)], axis=2)
    eye = jnp.asarray(np.eye(len(POOL_WINDOWS), dtype=np.float32))
    q = dict(
        wm=jnp.concatenate([wxbc, wz, wp, jnp.repeat(wdt, SSD_HEAD_DIM, axis=2)], axis=2).astype(BF16),
        wkx=slots(wk).astype(BF16),
        wqt=jnp.swapaxes(slots(wq * (FOX_HEAD_DIM ** -0.5 * LOG2E)), 1, 2).astype(BF16),
        wvt=jnp.swapaxes(slots(wv), 1, 2).astype(BF16),
        ws=ws.astype(BF16),
        wst=jnp.swapaxes(ws[:, :, 0:16], 1, 2).astype(BF16),
        fbc=jnp.pad(p["fox_f_bias"].astype(F32), ((0, 0), (SSD_HEADS, LANES - SSD_HEADS - FOX_HEADS))
                    ).reshape(depth, 1, LANES),
        conv_w=p["ssm_conv_w"].astype(F32), conv_b=vec(p["ssm_conv_b"], SSD_CONV_CH),
        dtb_e=per_head(p["ssm_dt_bias"]), alog_e=per_head(p["ssm_a_log"]), dsk_e=per_head(p["ssm_d"]),
        norm_w=vec(p["ssm_norm_w"], SSD_WIDTH),
        dtb_r=head_rows(p["ssm_dt_bias"]), alog_r=head_rows(p["ssm_a_log"]),
        pool_wbd=jnp.einsum("lgij,gh->lgihj", p["pool_w"].astype(F32), eye
                            ).reshape(depth, POOL_WIDTH, POOL_WIDTH).astype(BF16),
        pool_b=vec(p["pool_b"].reshape(depth, POOL_WIDTH), POOL_WIDTH), pool_sc=vec(p["pool_scale"], POOL_WIDTH),
        w_out=p["w_out"].astype(BF16),
        xa_wkv=jnp.concatenate([p["xa_wk"], p["xa_wv"]], axis=2).astype(BF16),
        xa_wq=(p["xa_wq"] * (XATTN_HEAD_DIM ** -0.5)).astype(BF16),
        xa_wo=p["xa_wo"].astype(BF16),
        ffn_w1=p["ffn_w1"].astype(BF16), ffn_w3=p["ffn_w3"].astype(BF16), ffn_w2=p["ffn_w2"].astype(BF16),
        moe_w1=p["moe_w1"].astype(BF16), moe_w3=p["moe_w3"].astype(BF16), moe_w2=p["moe_w2"].astype(BF16),
        router=jnp.pad(p["router_w"].astype(F32), ((0, 0), (0, 0), (0, LANES - N_EXPERTS))),
    )
    for name in ("ln1_g", "ln1_b", "ln2_g", "ln2_b", "ln3_g", "ln3_b"):
        q[name] = vec(p[name], D_MODEL)
    return q


def _token_mixing(x2, q, layer, *, batch, seq_len):
    place, qones, vones = _fox_constants()
    xbc, z, pool_in, dte, kx, qt, vt, rowsp = _in_proj(
        x2, q["wm"][layer], q["wkx"][layer], q["wqt"][layer], q["wvt"][layer], q["ws"][layer], q["wst"][layer],
        q["fbc"][layer], place, qones, vones, seq_len=seq_len, tm=min(IN_TM, seq_len))

    y_ssd = _ssd(xbc, z, dte, rowsp, q["conv_w"][layer], q["conv_b"][layer], q["dtb_e"][layer],
                 q["alog_e"][layer], q["dsk_e"][layer], q["norm_w"][layer], q["dtb_r"][layer], q["alog_r"][layer],
                 batch=batch, seq_len=seq_len, tb=min(SSD_TB, seq_len), chunk=SSD_L)

    y_fox = _fox(kx, qt, vt, batch=batch, seq_len=seq_len, tq=min(FOX_TQ, seq_len), tk=min(FOX_TK, seq_len))

    y_pool = _pool(pool_in, q["pool_wbd"][layer], q["pool_b"][layer], q["pool_sc"][layer],
                   batch=batch, seq_len=seq_len, tb=min(POOL_TB, seq_len))

    return _out_proj(x2, y_ssd, y_fox, y_pool, q["w_out"][layer], q["ln1_g"][layer], q["ln1_b"][layer],
                     tm=min(OUT_TM, seq_len))


def _cross_attention(x2, mem2, q, layer, *, batch, seq_len, mem_len):
    kv = _kv_proj(mem2, q["xa_wkv"][layer])
    return _xattn(x2, kv, q["xa_wq"][layer], q["xa_wo"][layer], q["ln2_g"][layer], q["ln2_b"][layer],
                  batch=batch, seq_len=seq_len, mem_len=mem_len, tm=min(XA_TM, seq_len))


def _dense_ffn(x2, q, layer):
    j = layer // 2
    n = x2.shape[0]
    tm = min(FFN_TM, n)
    tile_expert = jnp.zeros((n // tm,), jnp.int32)
    nvalid = jnp.full((1,), n // tm, jnp.int32)
    return _ffn(tile_expert, nvalid, x2, q["ffn_w1"][j:j + 1], q["ffn_w3"][j:j + 1], q["ffn_w2"][j:j + 1],
                q["ln3_g"][layer], q["ln3_b"][layer], tm=tm, tf=FFN_TF, grouped=False)


def _moe_ffn(x2, q, layer):
    j = layer // 2
    n = x2.shape[0]
    tm = min(MOE_TM, n)
    row_tm = min(ROW_TM, n)
    route, counts = _router(x2, q["router"][j], tm=min(ROUTE_TM, n))

    counts = counts[0, :N_EXPERTS].astype(jnp.int32)
    padded = (counts + tm - 1) // tm * tm
    pend = jnp.cumsum(padded)
    pstart = pend - padded
    experts = route[:, 0:TOP_K].astype(jnp.int32)
    ranks = route[:, 4:4 + TOP_K].astype(jnp.int32)
    pos = (pstart[experts] + ranks).astype(jnp.int32)
    pos3 = pos.reshape(n // row_tm, POS_ROWS, LANES)
    cap = n * TOP_K + N_EXPERTS * tm
    ntiles = cap // tm
    tile_start = jnp.arange(ntiles, dtype=jnp.int32) * tm
    tile_expert = jnp.minimum(jnp.sum((pend[None, :] <= tile_start[:, None]).astype(jnp.int32), axis=1),
                              N_EXPERTS - 1).astype(jnp.int32)
    nvalid = (pend[-1:] // tm).astype(jnp.int32)

    xs = _dispatch(pend.astype(jnp.int32), padded.astype(jnp.int32), pos3, x2, cap=cap, row_tm=row_tm, pad_tm=tm)
    ys = _ffn(tile_expert, nvalid, xs, q["moe_w1"][j], q["moe_w3"][j], q["moe_w2"][j],
              q["ln3_g"][layer], q["ln3_b"][layer], tm=tm, tf=FFN_TF, grouped=True)
    return _combine(pos3, x2, route, ys, q["ln3_g"][layer], q["ln3_b"][layer], row_tm=row_tm)


def _forward(x, mem, p):
    batch, seq_len, _ = x.shape
    mem_len = mem.shape[1]
    x2 = x.reshape(batch * seq_len, D_MODEL).astype(F32)
    mem2 = mem.reshape(batch * mem_len, D_MODEL).astype(F32)
    q = _prepare(p)
    for layer in range(DEPTH):
        x2 = _token_mixing(x2, q, layer, batch=batch, seq_len=seq_len)
        x2 = _cross_attention(x2, mem2, q, layer, batch=batch, seq_len=seq_len, mem_len=mem_len)
        x2 = _dense_ffn(x2, q, layer) if layer % 2 == 0 else _moe_ffn(x2, q, layer)
    return x2.reshape(batch, seq_len, D_MODEL)


def kernel(x, mem, w_in, ssm_conv_w, ssm_conv_b, ssm_dt_bias, ssm_a_log, ssm_d, ssm_norm_w, fox_f_bias, pool_w, pool_b, pool_scale, w_out, ln1_g, ln1_b, xa_wq, xa_wk, xa_wv, xa_wo, ln2_g, ln2_b, ffn_w1, ffn_w3, ffn_w2, router_w, moe_w1, moe_w3, moe_w2, ln3_g, ln3_b):
    p = dict(w_in=w_in, ssm_conv_w=ssm_conv_w, ssm_conv_b=ssm_conv_b, ssm_dt_bias=ssm_dt_bias,
             ssm_a_log=ssm_a_log, ssm_d=ssm_d, ssm_norm_w=ssm_norm_w, fox_f_bias=fox_f_bias, pool_w=pool_w,
             pool_b=pool_b, pool_scale=pool_scale, w_out=w_out, ln1_g=ln1_g, ln1_b=ln1_b, xa_wq=xa_wq,
             xa_wk=xa_wk, xa_wv=xa_wv, xa_wo=xa_wo, ln2_g=ln2_g, ln2_b=ln2_b, ffn_w1=ffn_w1, ffn_w3=ffn_w3,
             ffn_w2=ffn_w2, router_w=router_w, moe_w1=moe_w1, moe_w3=moe_w3, moe_w2=moe_w2, ln3_g=ln3_g,
             ln3_b=ln3_b)
    return _forward(x, mem, p)
```

```python
import functools

import jax
import jax.numpy as jnp
import numpy as np
from jax import lax
from jax.experimental import pallas as pl
from jax.experimental.pallas import tpu as pltpu

F32 = jnp.float32
BF16 = jnp.bfloat16

D_MODEL = 1024
DEPTH = 4
SSD_WIDTH = 512
SSD_HEAD_DIM = 64
SSD_HEADS = 8
SSD_GROUPS = 2
SSD_STATE = 128
SSD_CONV = 4
SSD_CONV_CH = 1024
FOX_WIDTH = 256
FOX_HEAD_DIM = 64
FOX_HEADS = 4
POOL_WIDTH = 256
POOL_WINDOWS = (2, 4, 8, 16)
POOL_GROUP_DIM = 64
IN_SIZES = (512, 1024, 8, 256, 256, 256, 4, 256)
XATTN_HEADS = 4
XATTN_HEAD_DIM = 256
D_FF = 3584
N_EXPERTS = 8
TOP_K = 2
DN_ALPHA = (2 * DEPTH) ** 0.25
LN_EPS = 1e-5
RMS_EPS = 1e-5

LANES = 128
MIB = 1024 * 1024

IN_TM = 512
SSD_TB = 512
SSD_L = 128
FOX_TQ = 1024
FOX_TK = 512
FOX_QW = 128
FOX_PV_GROUP = 2
FOX_LOOKAHEAD = 4
POOL_TB = 512
POOL_PIECE = 128
OUT_TM = 512
XA_TM = 512
FFN_TM = 1024
FFN_TF = 512
MOE_TM = 1024
ROUTE_TM = 512
ROW_TM = 512


def _cparams(sem, vmem_mib=48):
    return pltpu.CompilerParams(dimension_semantics=sem, vmem_limit_bytes=vmem_mib * MIB)


def _dot(a, b):
    return jnp.dot(a, b, preferred_element_type=F32)


def _dot_nt(a, b):
    return lax.dot_general(a, b, (((1,), (1,)), ((), ())), preferred_element_type=F32)


def _dot_tn(a, b):
    return lax.dot_general(a, b, (((0,), (0,)), ((), ())), preferred_element_type=F32)


def _split3(x):
    hi = x.astype(BF16)
    r1 = x - hi.astype(F32)
    mid = r1.astype(BF16)
    lo = (r1 - mid.astype(F32)).astype(BF16)
    return hi, mid, lo


def _silu(x):
    return x / (1.0 + jnp.exp(-x))


def _softplus(x):
    return jnp.maximum(x, 0.0) + jnp.log1p(jnp.exp(-jnp.abs(x)))


def _log_sigmoid(x):
    return jnp.minimum(x, 0.0) - jnp.log1p(jnp.exp(-jnp.abs(x)))


def _layer_norm(v, g, b):
    mu = jnp.mean(v, axis=-1, keepdims=True)
    d = v - mu
    var = jnp.mean(d * d, axis=-1, keepdims=True)
    return d * lax.rsqrt(var + LN_EPS) * g + b


def _tri_ones(n, lower):
    r = lax.broadcasted_iota(jnp.int32, (n, n), 0)
    c = lax.broadcasted_iota(jnp.int32, (n, n), 1)
    m = (c <= r) if lower else (r <= c)
    return jnp.where(m, 1.0, 0.0).astype(BF16)


FOX_SLOT = LANES
FOX_EXT = FOX_HEADS * FOX_SLOT
FOX_BIAS_TERMS = 3
FOX_ONES_ROWS = 16
LOG2E = 1.4426950408889634


def _in_proj_kernel(x_ref, wm_ref, wkx_ref, wqt_ref, wvt_ref, ws_ref, wst_ref, fbc_ref, place_ref, qones_ref,
                    vones_ref, xbc_ref, z_ref, pool_ref, dte_ref, kx_ref, qt_ref, vt_ref, rows_ref,
                    carry_c, *, tiles_per_batch):
    i = pl.program_id(0)
    tm = x_ref.shape[0]
    xb = x_ref[...].astype(BF16)

    xbc_ref[...] = _dot(xb, wm_ref[:, 0:1024])
    z_ref[...] = _dot(xb, wm_ref[:, 1024:1536])
    pool_ref[...] = _dot(xb, wm_ref[:, 1536:1792])
    dte_ref[...] = _dot(xb, wm_ref[:, 1792:2304])
    qt_ref[...] = (_dot_nt(wqt_ref[...], xb) + qones_ref[...]).astype(BF16)
    vt_ref[...] = (_dot_nt(wvt_ref[...], xb) + vones_ref[...]).astype(BF16)
    rows_ref[...] = _dot_nt(wst_ref[...], xb)

    @pl.when(i % tiles_per_batch == 0)
    def _():
        carry_c[...] = jnp.zeros_like(carry_c)

    small_c = _dot(xb, ws_ref[...])
    c3 = _split3(_log_sigmoid(small_c + fbc_ref[...]))
    tri = _tri_ones(tm, lower=True)
    cs_c = _dot(tri, c3[0]) + _dot(tri, c3[1]) + _dot(tri, c3[2]) + carry_c[...]
    carry_c[...] = cs_c[tm - 1:tm, :]
    n3 = _split3(cs_c * (-LOG2E))
    kx = _dot(xb, wkx_ref[...])
    for term in range(FOX_BIAS_TERMS):
        kx = kx + _dot(n3[term], place_ref[term])
    kx_ref[...] = kx.astype(BF16)


def _in_proj(x2, wm, wkx, wqt, wvt, ws, wst, fbc, place, qones, vones, *, seq_len, tm):
    n = x2.shape[0]
    grid = (n // tm,)
    full = lambda a: pl.BlockSpec(a.shape, lambda i: (0,) * a.ndim)
    rowblk = lambda w: pl.BlockSpec((tm, w), lambda i: (i, 0))
    colblk = lambda h: pl.BlockSpec((h, tm), lambda i: (0, i))
    out_shape = (
        jax.ShapeDtypeStruct((n, 1024), F32),
        jax.ShapeDtypeStruct((n, 512), F32),
        jax.ShapeDtypeStruct((n, 256), F32),
        jax.ShapeDtypeStruct((n, 512), F32),
        jax.ShapeDtypeStruct((n, FOX_EXT), BF16),
        jax.ShapeDtypeStruct((FOX_EXT, n), BF16),
        jax.ShapeDtypeStruct((FOX_EXT, n), BF16),
        jax.ShapeDtypeStruct((16, n), F32),
    )
    out_specs = (rowblk(1024), rowblk(512), rowblk(256), rowblk(512), rowblk(FOX_EXT),
                 colblk(FOX_EXT), colblk(FOX_EXT), colblk(16))
    args = (x2, wm, wkx, wqt, wvt, ws, wst, fbc, place, qones, vones)
    return pl.pallas_call(
        functools.partial(_in_proj_kernel, tiles_per_batch=seq_len // tm),
        out_shape=out_shape,
        grid=grid,
        in_specs=[rowblk(D_MODEL)] + [full(a) for a in args[1:]],
        out_specs=out_specs,
        scratch_shapes=[pltpu.VMEM((1, LANES), F32)],
        compiler_params=_cparams(("arbitrary",)),
        name="in_proj",
    )(*args)


def _ssd_kernel(xbc_ref, z_ref, dte_ref, rows_ref, cw_ref, cb_ref, dtb_ref, alog_ref, dsk_ref, nw_ref,
                dtbr_ref, alogr_ref, y_ref, tail, ubuf, xc, state, *, chunk):
    t = pl.program_id(1)
    tb = xbc_ref.shape[0]
    L = chunk
    CONV_PIECE = 64
    HALO = 8

    @pl.when(t == 0)
    def _():
        tail[...] = jnp.zeros_like(tail)
        state[...] = jnp.zeros_like(state)

    ubuf[0:HALO, :] = tail[...]
    ubuf[HALO:HALO + tb, :] = xbc_ref[...]
    tail[...] = xbc_ref[tb - HALO:tb, :]
    for r0 in range(0, tb, CONV_PIECE):
        window = ubuf[r0:r0 + HALO + CONV_PIECE, :]
        acc = jnp.broadcast_to(cb_ref[...], (CONV_PIECE, SSD_CONV_CH))
        for k in range(SSD_CONV):
            delay = SSD_CONV - 1 - k
            tap = window if delay == 0 else pltpu.roll(window, shift=delay, axis=0)
            acc = acc + cw_ref[k:k + 1, :] * tap[HALO:HALO + CONV_PIECE, :]
        xc[r0:r0 + CONV_PIECE, :] = _silu(acc)

    a_e = -jnp.exp(alog_ref[...])
    a_r = -jnp.exp(alogr_ref[...])
    tri = _tri_ones(L, lower=True)
    upp = _tri_ones(L, lower=False)
    rr = lax.broadcasted_iota(jnp.int32, (L, L), 0)
    cc = lax.broadcasted_iota(jnp.int32, (L, L), 1)
    causal = cc <= rr
    lane = lax.broadcasted_iota(jnp.int32, (L, LANES), 1)
    left = lane < SSD_HEAD_DIM
    HG = SSD_HEADS // SSD_GROUPS
    GW = SSD_WIDTH // SSD_GROUPS

    def body(c, carry):
        r0 = pl.multiple_of(c * L, L)
        rows = pl.ds(r0, L)
        xs = xc[rows, 0:SSD_WIDTH]
        dt_e = _softplus(dte_ref[rows, :] + dtb_ref[...])
        a3 = _split3(dt_e * a_e)
        acs_e = _dot(tri, a3[0]) + _dot(tri, a3[1]) + _dot(tri, a3[2])
        dt_r = _softplus(rows_ref[:, rows] + dtbr_ref[...])
        ar3 = _split3(dt_r * a_r)
        acs_r = _dot(ar3[0], upp) + _dot(ar3[1], upp) + _dot(ar3[2], upp)
        total = acs_e[L - 1:L, :]
        x_dt = xs * dt_e
        xb = x_dt.astype(BF16)
        xd = (x_dt * jnp.exp(total - acs_e)).astype(BF16)
        eacs = jnp.exp(acs_e)
        y_parts = []
        for g in range(SSD_GROUPS):
            bm = xc[rows, SSD_WIDTH + g * SSD_STATE:SSD_WIDTH + (g + 1) * SSD_STATE].astype(BF16)
            cm = xc[rows, SSD_WIDTH + (SSD_GROUPS + g) * SSD_STATE:
                    SSD_WIDTH + (SSD_GROUPS + g + 1) * SSD_STATE].astype(BF16)
            cb = _dot_nt(cm, bm)
            for pair in range(HG // 2):
                lo = g * GW + pair * LANES
                x_pair = xb[:, lo:lo + LANES]
                outs = []
                for sub in range(2):
                    h = g * HG + pair * 2 + sub
                    seg = acs_e[:, h * SSD_HEAD_DIM:h * SSD_HEAD_DIM + 1] - acs_r[h:h + 1, :]
                    decay = jnp.exp(jnp.where(causal, seg, -jnp.inf))
                    outs.append(_dot((cb * decay).astype(BF16), x_pair))
                y_parts.append(jnp.where(left, outs[0], outs[1]))
            s_prev = state[g]
            y_off = _dot(cm, s_prev.astype(BF16)) * eacs[:, g * GW:(g + 1) * GW]
            y_parts[-2] = y_parts[-2] + y_off[:, 0:LANES]
            y_parts[-1] = y_parts[-1] + y_off[:, LANES:2 * LANES]
            state[g] = s_prev * jnp.exp(total[:, g * GW:(g + 1) * GW]) + _dot_tn(bm, xd[:, g * GW:(g + 1) * GW])
        y = jnp.concatenate(y_parts, axis=1) + xs * dsk_ref[...]
        y = y * _silu(z_ref[rows, :])
        outs = []
        for g in range(SSD_GROUPS):
            yg = y[:, g * GW:(g + 1) * GW]
            ms = jnp.mean(yg * yg, axis=-1, keepdims=True)
            outs.append(yg * lax.rsqrt(ms + RMS_EPS))
        y_ref[rows, :] = (jnp.concatenate(outs, axis=1) * nw_ref[...]).astype(y_ref.dtype)
        return carry

    lax.fori_loop(0, tb // L, body, 0, unroll=True)


def _ssd(xbc, z, dte, rowsp, cw, cb, dtb_e, alog_e, dsk_e, nw, dtb_r, alog_r, *, batch, seq_len, tb, chunk):
    n = xbc.shape[0]
    tpb = seq_len // tb
    full = lambda a: pl.BlockSpec(a.shape, lambda b, t: (0,) * a.ndim)
    rowblk = lambda w: pl.BlockSpec((tb, w), lambda b, t: (b * tpb + t, 0))
    return pl.pallas_call(
        functools.partial(_ssd_kernel, chunk=chunk),
        out_shape=jax.ShapeDtypeStruct((n, SSD_WIDTH), BF16),
        grid=(batch, tpb),
        in_specs=[rowblk(SSD_CONV_CH), rowblk(SSD_WIDTH), rowblk(SSD_WIDTH),
                  pl.BlockSpec((16, tb), lambda b, t: (0, b * tpb + t)),
                  full(cw), full(cb), full(dtb_e), full(alog_e), full(dsk_e), full(nw),
                  full(dtb_r), full(alog_r)],
        out_specs=rowblk(SSD_WIDTH),
        scratch_shapes=[pltpu.VMEM((8, SSD_CONV_CH), F32),
                        pltpu.VMEM((tb + 8, SSD_CONV_CH), F32),
                        pltpu.VMEM((tb, SSD_CONV_CH), F32),
                        pltpu.VMEM((SSD_GROUPS, SSD_STATE, SSD_WIDTH // SSD_GROUPS), F32)],
        compiler_params=_cparams(("parallel", "arbitrary")),
        name="ssd_mixer",
    )(xbc, z, dte, rowsp, cw, cb, dtb_e, alog_e, dsk_e, nw, dtb_r, alog_r)


def _fox_kernel(qi_ref, kj_ref, kx_ref, qt_ref, vt_ref, o_ref, m_sc, acc_sc, *, ratio, qw):
    step_id = pl.program_id(1)
    i = qi_ref[step_id]
    j = kj_ref[step_id]
    tk = kx_ref.shape[0]
    tq = qt_ref.shape[1]
    vrows = FOX_HEAD_DIM + FOX_ONES_ROWS

    @pl.when(j == 0)
    def _():
        m_sc[...] = jnp.full_like(m_sc, -jnp.inf)
        acc_sc[...] = jnp.zeros_like(acc_sc)

    def step(masked):
        units = [(h, c0) for h in range(FOX_HEADS) for c0 in range(0, tq, qw)]

        def score(unit):
            h, c0 = unit
            return _dot(kx_ref[:, h * FOX_SLOT:(h + 1) * FOX_SLOT],
                        qt_ref[h * FOX_SLOT:(h + 1) * FOX_SLOT, c0:c0 + qw])

        scores = [score(u) for u in units[:FOX_LOOKAHEAD]]
        pending = []
        for n_unit, (h, c0) in enumerate(units):
            if n_unit + FOX_LOOKAHEAD < len(units):
                scores.append(score(units[n_unit + FOX_LOOKAHEAD]))
            st = scores[n_unit]
            lo = h * FOX_SLOT
            if masked:
                key = j * tk + lax.broadcasted_iota(jnp.int32, (tk, qw), 0)
                qry = i * tq + c0 + lax.broadcasted_iota(jnp.int32, (tk, qw), 1)
                st = jnp.where(key <= qry, st, -jnp.inf)
            m_prev = m_sc[h, :, c0:c0 + qw]
            m_new = jnp.maximum(m_prev, jnp.max(st, axis=0, keepdims=True))
            pending.append((jnp.exp2(m_prev - m_new), jnp.exp2(st - m_new).astype(BF16)))
            m_sc[h, :, c0:c0 + qw] = m_new
            if len(pending) == FOX_PV_GROUP:
                g0 = c0 + qw - FOX_PV_GROUP * qw
                alpha = jnp.concatenate([a for a, _ in pending], axis=1)
                p = jnp.concatenate([pp for _, pp in pending], axis=1)
                acc_sc[h, :, g0:c0 + qw] = alpha * acc_sc[h, :, g0:c0 + qw] + _dot(vt_ref[lo:lo + vrows, :], p)
                pending = []

    @pl.when(j < i * ratio)
    def _():
        step(False)

    @pl.when(j >= i * ratio)
    def _():
        step(True)

    @pl.when(j == (i + 1) * ratio - 1)
    def _():
        parts = [acc_sc[h, 0:FOX_HEAD_DIM, :] / acc_sc[h, FOX_HEAD_DIM:FOX_HEAD_DIM + 1, :]
                 for h in range(FOX_HEADS)]
        o_ref[...] = jnp.concatenate(parts, axis=0).T.astype(o_ref.dtype)


def _fox(kx, qt, vt, *, batch, seq_len, tq, tk):
    n = kx.shape[0]
    assert tq % tk == 0
    ratio = tq // tk
    nq = seq_len // tq
    nk = seq_len // tk
    pairs = [(i, j) for i in range(nq) for j in range((i + 1) * ratio)]
    qi = jnp.asarray([pq for pq, _ in pairs], jnp.int32)
    kj = jnp.asarray([pk for _, pk in pairs], jnp.int32)
    return pl.pallas_call(
        functools.partial(_fox_kernel, ratio=ratio, qw=min(FOX_QW, tq)),
        out_shape=jax.ShapeDtypeStruct((n, FOX_WIDTH), BF16),
        grid_spec=pltpu.PrefetchScalarGridSpec(
            num_scalar_prefetch=2,
            grid=(batch, len(pairs)),
            in_specs=[pl.BlockSpec((tk, FOX_EXT), lambda b, s, qi, kj: (b * nk + kj[s], 0)),
                      pl.BlockSpec((FOX_EXT, tq), lambda b, s, qi, kj: (0, b * nq + qi[s])),
                      pl.BlockSpec((FOX_EXT, tk), lambda b, s, qi, kj: (0, b * nk + kj[s]))],
            out_specs=pl.BlockSpec((tq, FOX_WIDTH), lambda b, s, qi, kj: (b * nq + qi[s], 0)),
            scratch_shapes=[pltpu.VMEM((FOX_HEADS, 1, tq), F32),
                            pltpu.VMEM((FOX_HEADS, FOX_HEAD_DIM + FOX_ONES_ROWS, tq), F32)]),
        compiler_params=_cparams(("parallel", "arbitrary")),
        name="fox_attention",
    )(qi, kj, kx, qt, vt)


POOL_HALO = 16


def _pool_kernel(u_ref, w_ref, b_ref, sc_ref, y_ref, tail, ubuf):
    t = pl.program_id(1)
    tb = u_ref.shape[0]
    P = POOL_PIECE

    @pl.when(t == 0)
    def _():
        tail[...] = jnp.zeros_like(tail)

    ubuf[0:POOL_HALO, :] = tail[...]
    ubuf[POOL_HALO:POOL_HALO + tb, :] = u_ref[...]
    tail[...] = u_ref[tb - POOL_HALO:tb, :]
    group = lax.broadcasted_iota(jnp.int32, (P, POOL_WIDTH), 1) // POOL_GROUP_DIM
    for p0 in range(0, tb, P):
        w1 = ubuf[p0:p0 + P + POOL_HALO, :]
        n1 = P + POOL_HALO
        a2 = w1[1:n1] + w1[0:n1 - 1]
        a4 = a2[2:n1 - 1] + a2[0:n1 - 3]
        a8 = a4[4:n1 - 3] + a4[0:n1 - 7]
        a16 = a8[8:n1 - 7] + a8[0:n1 - 15]
        sums = (a2[15:15 + P], a4[13:13 + P], a8[9:9 + P], a16[1:1 + P])
        tpos = t * tb + p0 + lax.broadcasted_iota(jnp.int32, (P, 1), 0) + 1
        pooled = jnp.zeros((P, POOL_WIDTH), F32)
        for g, win in enumerate(POOL_WINDOWS):
            cnt = jnp.minimum(tpos, win).astype(F32)
            pooled = jnp.where(group == g, sums[g] / cnt, pooled)
        pooled = pooled - w1[POOL_HALO:POOL_HALO + P]
        y = _dot(pooled.astype(BF16), w_ref[...]) + b_ref[...]
        y_ref[p0:p0 + P, :] = (y * sc_ref[...]).astype(y_ref.dtype)


def _pool(u, wbd, b, sc, *, batch, seq_len, tb):
    n = u.shape[0]
    tpb = seq_len // tb
    full = lambda a: pl.BlockSpec(a.shape, lambda bb, t: (0,) * a.ndim)
    rowblk = pl.BlockSpec((tb, POOL_WIDTH), lambda bb, t: (bb * tpb + t, 0))
    return pl.pallas_call(
        _pool_kernel,
        out_shape=jax.ShapeDtypeStruct((n, POOL_WIDTH), BF16),
        grid=(batch, tpb),
        in_specs=[rowblk, full(wbd), full(b), full(sc)],
        out_specs=rowblk,
        scratch_shapes=[pltpu.VMEM((POOL_HALO, POOL_WIDTH), F32),
                        pltpu.VMEM((tb + POOL_HALO, POOL_WIDTH), F32)],
        compiler_params=_cparams(("parallel", "arbitrary")),
        name="pool_mixer",
    )(u, wbd, b, sc)


def _out_proj_kernel(x_ref, ys_ref, yf_ref, yp_ref, w_ref, g_ref, b_ref, o_ref):
    mix = (_dot(ys_ref[...], w_ref[0:512, :]) + _dot(yf_ref[...], w_ref[512:768, :])
           + _dot(yp_ref[...], w_ref[768:1024, :]))
    o_ref[...] = _layer_norm(DN_ALPHA * x_ref[...] + mix, g_ref[...], b_ref[...])


def _out_proj(x2, ys, yf, yp, w, g, b, *, tm):
    n = x2.shape[0]
    full = lambda a: pl.BlockSpec(a.shape, lambda i: (0,) * a.ndim)
    rowblk = lambda wd: pl.BlockSpec((tm, wd), lambda i: (i, 0))
    return pl.pallas_call(
        _out_proj_kernel,
        out_shape=jax.ShapeDtypeStruct((n, D_MODEL), F32),
        grid=(n // tm,),
        in_specs=[rowblk(D_MODEL), rowblk(512), rowblk(256), rowblk(256), full(w), full(g), full(b)],
        out_specs=rowblk(D_MODEL),
        compiler_params=_cparams(("parallel",)),
        name="out_proj_ln",
    )(x2, ys, yf, yp, w, g, b)


def _kv_proj_kernel(m_ref, w_ref, o_ref):
    o_ref[...] = _dot(m_ref[...].astype(BF16), w_ref[...]).astype(o_ref.dtype)


def _kv_proj(mem2, wkv):
    m = mem2.shape[0]
    return pl.pallas_call(
        _kv_proj_kernel,
        out_shape=jax.ShapeDtypeStruct((m, wkv.shape[1]), BF16),
        grid=(1,),
        in_specs=[pl.BlockSpec(mem2.shape, lambda i: (0, 0)), pl.BlockSpec(wkv.shape, lambda i: (0, 0))],
        out_specs=pl.BlockSpec((m, wkv.shape[1]), lambda i: (0, 0)),
        compiler_params=_cparams(("arbitrary",)),
        name="xattn_kv_proj",
    )(mem2, wkv)


def _xattn_kernel(x_ref, kv_ref, wq_ref, wo_ref, g_ref, b_ref, o_ref):
    x = x_ref[...]
    q = _dot(x.astype(BF16), wq_ref[...]).astype(BF16)
    heads = []
    for h in range(XATTN_HEADS):
        lo = h * XATTN_HEAD_DIM
        k_h = kv_ref[:, lo:lo + XATTN_HEAD_DIM]
        v_h = kv_ref[:, D_MODEL + lo:D_MODEL + lo + XATTN_HEAD_DIM]
        s = _dot_nt(q[:, lo:lo + XATTN_HEAD_DIM], k_h)
        p = jnp.exp(s - jnp.max(s, axis=-1, keepdims=True))
        l = jnp.sum(p, axis=-1, keepdims=True)
        heads.append((_dot(p.astype(BF16), v_h) / l).astype(BF16))
    o = jnp.concatenate(heads, axis=1)
    xa = _dot(o, wo_ref[...])
    o_ref[...] = _layer_norm(DN_ALPHA * x + xa, g_ref[...], b_ref[...])


def _xattn(x2, kv, wq, wo, g, b, *, batch, seq_len, mem_len, tm):
    n = x2.shape[0]
    tpb = seq_len // tm
    full = lambda a: pl.BlockSpec(a.shape, lambda bb, t: (0,) * a.ndim)
    rowblk = pl.BlockSpec((tm, D_MODEL), lambda bb, t: (bb * tpb + t, 0))
    return pl.pallas_call(
        _xattn_kernel,
        out_shape=jax.ShapeDtypeStruct((n, D_MODEL), F32),
        grid=(batch, tpb),
        in_specs=[rowblk, pl.BlockSpec((mem_len, 2 * D_MODEL), lambda bb, t: (bb, 0)),
                  full(wq), full(wo), full(g), full(b)],
        out_specs=rowblk,
        compiler_params=_cparams(("parallel", "parallel")),
        name="xattn_ln",
    )(x2, kv, wq, wo, g, b)


ROW_SLABS = D_MODEL // LANES


def _ffn_kernel(texp_ref, nvalid_ref, x_ref, w1_ref, w3_ref, w2_ref, g_ref, b_ref, o_ref, xb_sc, acc_sc,
                *, grouped):
    t = pl.program_id(0)
    f = pl.program_id(1)
    nf = pl.num_programs(1)

    @pl.when(t < nvalid_ref[0])
    def _():
        @pl.when(f == 0)
        def _():
            xb_sc[...] = x_ref[...].reshape(xb_sc.shape).astype(BF16)
            acc_sc[...] = jnp.zeros_like(acc_sc)

        xb = xb_sc[...]
        h1 = _dot(xb, w1_ref[0].astype(BF16))
        h3 = _dot(xb, w3_ref[0].astype(BF16))
        acc_sc[...] += _dot((_silu(h1) * h3).astype(BF16), w2_ref[0].astype(BF16))

        @pl.when(f == nf - 1)
        def _():
            if grouped:
                o_ref[...] = acc_sc[...].reshape(o_ref.shape)
            else:
                o_ref[...] = _layer_norm(DN_ALPHA * x_ref[...] + acc_sc[...], g_ref[...], b_ref[...])

    @pl.when((t >= nvalid_ref[0]) & (f == 0))
    def _():
        o_ref[...] = jnp.zeros_like(o_ref)


def _ffn(tile_expert, nvalid, x, w1, w3, w2, g, b, *, tm, tf, grouped):
    n = x.shape[0]
    nt = n // tm
    nf = D_FF // tf

    def tile_of(t, nv):
        return jnp.minimum(t, nv[0] - 1)

    def f_of(t, f, nv):
        return jnp.where(t < nv[0], f, nf - 1)

    if grouped:
        rowblk = pl.BlockSpec((tm, ROW_SLABS, LANES), lambda t, f, te, nv: (tile_of(t, nv), 0, 0))
        outblk = pl.BlockSpec((tm, ROW_SLABS, LANES), lambda t, f, te, nv: (t, 0, 0))
        out_shape = jax.ShapeDtypeStruct((n, ROW_SLABS, LANES), F32)
    else:
        rowblk = pl.BlockSpec((tm, D_MODEL), lambda t, f, te, nv: (tile_of(t, nv), 0))
        outblk = pl.BlockSpec((tm, D_MODEL), lambda t, f, te, nv: (t, 0))
        out_shape = jax.ShapeDtypeStruct((n, D_MODEL), F32)
    w13 = pl.BlockSpec((1, D_MODEL, tf), lambda t, f, te, nv: (te[tile_of(t, nv)], 0, f_of(t, f, nv)))
    w2s = pl.BlockSpec((1, tf, D_MODEL), lambda t, f, te, nv: (te[tile_of(t, nv)], f_of(t, f, nv), 0))
    vec = pl.BlockSpec((1, D_MODEL), lambda t, f, te, nv: (0, 0))
    return pl.pallas_call(
        functools.partial(_ffn_kernel, grouped=grouped),
        out_shape=out_shape,
        grid_spec=pltpu.PrefetchScalarGridSpec(
            num_scalar_prefetch=2,
            grid=(nt, nf),
            in_specs=[rowblk, w13, w13, w2s, vec, vec],
            out_specs=outblk,
            scratch_shapes=[pltpu.VMEM((tm, D_MODEL), BF16), pltpu.VMEM((tm, D_MODEL), F32)]),
        compiler_params=_cparams(("arbitrary", "arbitrary"), vmem_mib=56),
        name="swiglu_grouped" if grouped else "swiglu_ln",
    )(tile_expert, nvalid, x, w1, w3, w2, g, b)


def _router_kernel(x_ref, w_ref, route_ref, counts_ref, carry):
    i = pl.program_id(0)
    tm = x_ref.shape[0]

    @pl.when(i == 0)
    def _():
        carry[...] = jnp.zeros_like(carry)

    xh, xm, _ = _split3(x_ref[...])
    wh, wm, _ = _split3(w_ref[...])
    logits = _dot(xh, wh) + (_dot(xh, wm) + _dot(xm, wh))
    lane = lax.broadcasted_iota(jnp.int32, (tm, LANES), 1)
    logits = jnp.where(lane < N_EXPERTS, logits, -jnp.inf)
    m1 = jnp.max(logits, axis=-1, keepdims=True)
    i1 = jnp.min(jnp.where(logits == m1, lane, LANES), axis=-1, keepdims=True)
    rest = jnp.where(lane == i1, -jnp.inf, logits)
    m2 = jnp.max(rest, axis=-1, keepdims=True)
    i2 = jnp.min(jnp.where(rest == m2, lane, LANES), axis=-1, keepdims=True)
    e21 = jnp.exp(m2 - m1)
    g1 = 1.0 / (1.0 + e21)
    g2 = e21 / (1.0 + e21)
    hit1 = lane == i1
    hit2 = lane == i2
    onehot = jnp.where(hit1 | hit2, 1.0, 0.0).astype(BF16)
    r = lax.broadcasted_iota(jnp.int32, (tm, tm), 0)
    c = lax.broadcasted_iota(jnp.int32, (tm, tm), 1)
    strict = jnp.where(c < r, 1.0, 0.0).astype(BF16)
    before = _dot(strict, onehot) + carry[...]
    rank1 = jnp.sum(jnp.where(hit1, before, 0.0), axis=-1, keepdims=True)
    rank2 = jnp.sum(jnp.where(hit2, before, 0.0), axis=-1, keepdims=True)
    carry[...] = carry[...] + jnp.sum(onehot.astype(F32), axis=0, keepdims=True)
    out = jnp.where(lane == 0, i1.astype(F32), 0.0)
    out = jnp.where(lane == 1, i2.astype(F32), out)
    out = jnp.where(lane == 2, g1, out)
    out = jnp.where(lane == 3, g2, out)
    out = jnp.where(lane == 4, rank1, out)
    out = jnp.where(lane == 5, rank2, out)
    route_ref[...] = out
    counts_ref[...] = carry[...]


def _router(x2, wr, *, tm):
    n = x2.shape[0]
    return pl.pallas_call(
        _router_kernel,
        out_shape=(jax.ShapeDtypeStruct((n, LANES), F32), jax.ShapeDtypeStruct((1, LANES), F32)),
        grid=(n // tm,),
        in_specs=[pl.BlockSpec((tm, D_MODEL), lambda i: (i, 0)), pl.BlockSpec(wr.shape, lambda i: (0, 0))],
        out_specs=(pl.BlockSpec((tm, LANES), lambda i: (i, 0)), pl.BlockSpec((1, LANES), lambda i: (0, 0))),
        scratch_shapes=[pltpu.VMEM((1, LANES), F32)],
        compiler_params=_cparams(("arbitrary",)),
        name="moe_router",
    )(x2, wr)


POS_ROWS = 8


ROW_UNROLL = 8


def _for_each_row_copy(row_copy, action):
    tokens_per_pos_row = LANES // TOP_K
    for prow in range(POS_ROWS):
        def body(c, carry):
            for k in range(TOP_K):
                copy = row_copy(prow * tokens_per_pos_row + c, (prow, TOP_K * c + k), k)
                getattr(copy, action)()
            return carry

        lax.fori_loop(0, tokens_per_pos_row, body, 0, unroll=ROW_UNROLL)


def _dispatch_kernel(pend_ref, padded_ref, pos_hbm, x_ref, xs_hbm, pos_smem, stage, zeros_vmem, sem_pos, sem_rows,
                     *, row_tm, pad_tm):
    i = pl.program_id(0)

    def zero_copy(start):
        return pltpu.make_async_copy(zeros_vmem, xs_hbm.at[pl.ds(start, pad_tm)], sem_rows)

    @pl.when(i == 0)
    def _():
        zeros_vmem[...] = jnp.zeros_like(zeros_vmem)
        used = pend_ref[N_EXPERTS - 1]
        fills = [(padded_ref[e] > 0, pend_ref[e] - pad_tm) for e in range(N_EXPERTS)]
        fills += [(used + e * pad_tm < xs_hbm.shape[0], used + e * pad_tm) for e in range(N_EXPERTS)]
        for cond, start in fills:
            @pl.when(cond)
            def _():
                zero_copy(start).start()
        for cond, start in fills:
            @pl.when(cond)
            def _():
                zero_copy(start).wait()

    pos_copy = pltpu.make_async_copy(pos_hbm.at[i], pos_smem, sem_pos)
    pos_copy.start()
    stage[...] = x_ref[...].reshape(stage.shape)
    pos_copy.wait()

    def row_copy(r, entry, k):
        return pltpu.make_async_copy(stage.at[r], xs_hbm.at[pos_smem[entry]], sem_rows)

    _for_each_row_copy(row_copy, "start")
    _for_each_row_copy(row_copy, "wait")


def _dispatch(pend, padded, pos3, x2, *, cap, row_tm, pad_tm):
    n = x2.shape[0]
    assert TOP_K * row_tm == POS_ROWS * LANES
    return pl.pallas_call(
        functools.partial(_dispatch_kernel, row_tm=row_tm, pad_tm=pad_tm),
        out_shape=jax.ShapeDtypeStruct((cap, ROW_SLABS, LANES), F32),
        grid_spec=pltpu.PrefetchScalarGridSpec(
            num_scalar_prefetch=2,
            grid=(n // row_tm,),
            in_specs=[pl.BlockSpec(memory_space=pl.ANY),
                      pl.BlockSpec((row_tm, D_MODEL), lambda i, pe, pa: (i, 0))],
            out_specs=pl.BlockSpec(memory_space=pl.ANY),
            scratch_shapes=[pltpu.SMEM((POS_ROWS, LANES), jnp.int32),
                            pltpu.VMEM((row_tm, ROW_SLABS, LANES), F32),
                            pltpu.VMEM((pad_tm, ROW_SLABS, LANES), F32),
                            pltpu.SemaphoreType.DMA(()),
                            pltpu.SemaphoreType.DMA(())]),
        compiler_params=_cparams(("arbitrary",)),
        name="moe_dispatch",
    )(pend, padded, pos3, x2)


def _combine_kernel(pos_hbm, x_ref, route_ref, ys_hbm, g_ref, b_ref, o_ref, pos_smem, rows0, rows1,
                    sem_pos, sem_rows, *, row_tm):
    i = pl.program_id(0)
    pos_copy = pltpu.make_async_copy(pos_hbm.at[i], pos_smem, sem_pos)
    pos_copy.start()
    pos_copy.wait()
    rows = (rows0, rows1)

    def row_copy(r, entry, k):
        return pltpu.make_async_copy(ys_hbm.at[pos_smem[entry]], rows[k].at[r], sem_rows)

    _for_each_row_copy(row_copy, "start")
    _for_each_row_copy(row_copy, "wait")
    flat = (row_tm, D_MODEL)
    ff = rows0[...].reshape(flat) * route_ref[:, 2:3] + rows1[...].reshape(flat) * route_ref[:, 3:4]
    o_ref[...] = _layer_norm(DN_ALPHA * x_ref[...] + ff, g_ref[...], b_ref[...])


def _combine(pos3, x2, route, ys, g, b, *, row_tm):
    n = x2.shape[0]
    assert TOP_K * row_tm == POS_ROWS * LANES
    rowblk = lambda w: pl.BlockSpec((row_tm, w), lambda i: (i, 0))
    vec = pl.BlockSpec((1, D_MODEL), lambda i: (0, 0))
    return pl.pallas_call(
        functools.partial(_combine_kernel, row_tm=row_tm),
        out_shape=jax.ShapeDtypeStruct((n, D_MODEL), F32),
        grid=(n // row_tm,),
        in_specs=[pl.BlockSpec(memory_space=pl.ANY), rowblk(D_MODEL), rowblk(LANES),
                  pl.BlockSpec(memory_space=pl.ANY), vec, vec],
        out_specs=rowblk(D_MODEL),
        scratch_shapes=[pltpu.SMEM((POS_ROWS, LANES), jnp.int32),
                        pltpu.VMEM((row_tm, ROW_SLABS, LANES), F32),
                        pltpu.VMEM((row_tm, ROW_SLABS, LANES), F32),
                        pltpu.SemaphoreType.DMA(()),
                        pltpu.SemaphoreType.DMA(())],
        compiler_params=_cparams(("arbitrary",)),
        name="moe_combine_ln",
    )(pos3, x2, route, ys, g, b)


def _fox_constants():
    place = np.zeros((FOX_BIAS_TERMS, LANES, FOX_EXT), np.float32)
    qones = np.zeros((FOX_EXT, 1), np.float32)
    vones = np.zeros((FOX_EXT, 1), np.float32)
    for h in range(FOX_HEADS):
        base = h * FOX_SLOT + FOX_HEAD_DIM
        for term in range(FOX_BIAS_TERMS):
            place[term, SSD_HEADS + h, base + term] = 1.0
        qones[base:base + FOX_BIAS_TERMS] = 1.0
        vones[base:base + FOX_ONES_ROWS] = 1.0
    return jnp.asarray(place, BF16), jnp.asarray(qones), jnp.asarray(vones)


def _prepare(p):
    depth = p["w_in"].shape[0]
    offs = np.concatenate([[0], np.cumsum(IN_SIZES)])
    w_in = p["w_in"].astype(F32)
    wz, wxbc, wdt, wq, wk, wv, wf, wp = (w_in[:, :, offs[k]:offs[k + 1]] for k in range(len(IN_SIZES)))

    def slots(w):
        w = w.reshape(depth, D_MODEL, FOX_HEADS, FOX_HEAD_DIM)
        w = jnp.pad(w, ((0, 0), (0, 0), (0, 0), (0, FOX_SLOT - FOX_HEAD_DIM)))
        return w.reshape(depth, D_MODEL, FOX_EXT)

    def vec(v, width):
        return v.astype(F32).reshape(v.shape[0], 1, width)

    def per_head(v):
        return jnp.repeat(v.astype(F32), SSD_HEAD_DIM, axis=1).reshape(depth, 1, SSD_WIDTH)

    def head_rows(v):
        return jnp.pad(v.astype(F32), ((0, 0), (0, 16 - SSD_HEADS))).reshape(depth, 16, 1)

    ws = jnp.concatenate([wdt, wf, jnp.zeros((depth, D_MODEL, LANES - SSD_HEADS - FOX_HEADS), F32)], axis=2)
    eye = jnp.asarray(np.eye(len(POOL_WINDOWS), dtype=np.float32))
    q = dict(
        wm=jnp.concatenate([wxbc, wz, wp, jnp.repeat(wdt, SSD_HEAD_DIM, axis=2)], axis=2).astype(BF16),
        wkx=slots(wk).astype(BF16),
        wqt=jnp.swapaxes(slots(wq * (FOX_HEAD_DIM ** -0.5 * LOG2E)), 1, 2).astype(BF16),
        wvt=jnp.swapaxes(slots(wv), 1, 2).astype(BF16),
        ws=ws.astype(BF16),
        wst=jnp.swapaxes(ws[:, :, 0:16], 1, 2).astype(BF16),
        fbc=jnp.pad(p["fox_f_bias"].astype(F32), ((0, 0), (SSD_HEADS, LANES - SSD_HEADS - FOX_HEADS))
                    ).reshape(depth, 1, LANES),
        conv_w=p["ssm_conv_w"].astype(F32), conv_b=vec(p["ssm_conv_b"], SSD_CONV_CH),
        dtb_e=per_head(p["ssm_dt_bias"]), alog_e=per_head(p["ssm_a_log"]), dsk_e=per_head(p["ssm_d"]),
        norm_w=vec(p["ssm_norm_w"], SSD_WIDTH),
        dtb_r=head_rows(p["ssm_dt_bias"]), alog_r=head_rows(p["ssm_a_log"]),
        pool_wbd=jnp.einsum("lgij,gh->lgihj", p["pool_w"].astype(F32), eye
                            ).reshape(depth, POOL_WIDTH, POOL_WIDTH).astype(BF16),
        pool_b=vec(p["pool_b"].reshape(depth, POOL_WIDTH), POOL_WIDTH), pool_sc=vec(p["pool_scale"], POOL_WIDTH),
        w_out=p["w_out"].astype(BF16),
        xa_wkv=jnp.concatenate([p["xa_wk"], p["xa_wv"]], axis=2).astype(BF16),
        xa_wq=(p["xa_wq"] * (XATTN_HEAD_DIM ** -0.5)).astype(BF16),
        xa_wo=p["xa_wo"].astype(BF16),
        ffn_w1=p["ffn_w1"].astype(BF16), ffn_w3=p["ffn_w3"].astype(BF16), ffn_w2=p["ffn_w2"].astype(BF16),
        moe_w1=p["moe_w1"].reshape(-1, D_MODEL, D_FF),
        moe_w3=p["moe_w3"].reshape(-1, D_MODEL, D_FF),
        moe_w2=p["moe_w2"].reshape(-1, D_FF, D_MODEL),
        router=jnp.pad(p["router_w"].astype(F32), ((0, 0), (0, 0), (0, LANES - N_EXPERTS))),
    )
    for name in ("ln1_g", "ln1_b", "ln2_g", "ln2_b", "ln3_g", "ln3_b"):
        q[name] = vec(p[name], D_MODEL)
    return q


def _token_mixing(x2, q, layer, *, batch, seq_len):
    place, qones, vones = _fox_constants()
    xbc, z, pool_in, dte, kx, qt, vt, rowsp = _in_proj(
        x2, q["wm"][layer], q["wkx"][layer], q["wqt"][layer], q["wvt"][layer], q["ws"][layer], q["wst"][layer],
        q["fbc"][layer], place, qones, vones, seq_len=seq_len, tm=min(IN_TM, seq_len))

    y_ssd = _ssd(xbc, z, dte, rowsp, q["conv_w"][layer], q["conv_b"][layer], q["dtb_e"][layer],
                 q["alog_e"][layer], q["dsk_e"][layer], q["norm_w"][layer], q["dtb_r"][layer], q["alog_r"][layer],
                 batch=batch, seq_len=seq_len, tb=min(SSD_TB, seq_len), chunk=SSD_L)

    y_fox = _fox(kx, qt, vt, batch=batch, seq_len=seq_len, tq=min(FOX_TQ, seq_len), tk=min(FOX_TK, seq_len))

    y_pool = _pool(pool_in, q["pool_wbd"][layer], q["pool_b"][layer], q["pool_sc"][layer],
                   batch=batch, seq_len=seq_len, tb=min(POOL_TB, seq_len))

    return _out_proj(x2, y_ssd, y_fox, y_pool, q["w_out"][layer], q["ln1_g"][layer], q["ln1_b"][layer],
                     tm=min(OUT_TM, seq_len))


def _cross_attention(x2, mem2, q, layer, *, batch, seq_len, mem_len):
    kv = _kv_proj(mem2, q["xa_wkv"][layer])
    return _xattn(x2, kv, q["xa_wq"][layer], q["xa_wo"][layer], q["ln2_g"][layer], q["ln2_b"][layer],
                  batch=batch, seq_len=seq_len, mem_len=mem_len, tm=min(XA_TM, seq_len))


def _dense_ffn(x2, q, layer):
    j = layer // 2
    n = x2.shape[0]
    tm = min(FFN_TM, n)
    tile_expert = jnp.full((n // tm,), j, jnp.int32)
    nvalid = jnp.full((1,), n // tm, jnp.int32)
    return _ffn(tile_expert, nvalid, x2, q["ffn_w1"], q["ffn_w3"], q["ffn_w2"],
                q["ln3_g"][layer], q["ln3_b"][layer], tm=tm, tf=FFN_TF, grouped=False)


def _moe_ffn(x2, q, layer):
    j = layer // 2
    n = x2.shape[0]
    tm = min(MOE_TM, n)
    row_tm = min(ROW_TM, n)
    route, counts = _router(x2, q["router"][j], tm=min(ROUTE_TM, n))

    counts = counts[0, :N_EXPERTS].astype(jnp.int32)
    padded = (counts + tm - 1) // tm * tm
    pend = jnp.cumsum(padded)
    pstart = pend - padded
    experts = route[:, 0:TOP_K].astype(jnp.int32)
    ranks = route[:, 4:4 + TOP_K].astype(jnp.int32)
    pos = (pstart[experts] + ranks).astype(jnp.int32)
    pos3 = pos.reshape(n // row_tm, POS_ROWS, LANES)
    cap = n * TOP_K + N_EXPERTS * tm
    ntiles = cap // tm
    tile_start = jnp.arange(ntiles, dtype=jnp.int32) * tm
    tile_expert = jnp.minimum(jnp.sum((pend[None, :] <= tile_start[:, None]).astype(jnp.int32), axis=1),
                              N_EXPERTS - 1).astype(jnp.int32)
    nvalid = (pend[-1:] // tm).astype(jnp.int32)

    xs = _dispatch(pend.astype(jnp.int32), padded.astype(jnp.int32), pos3, x2, cap=cap, row_tm=row_tm, pad_tm=tm)
    ys = _ffn(tile_expert + j * N_EXPERTS, nvalid, xs, q["moe_w1"], q["moe_w3"], q["moe_w2"],
              q["ln3_g"][layer], q["ln3_b"][layer], tm=tm, tf=FFN_TF, grouped=True)
    return _combine(pos3, x2, route, ys, q["ln3_g"][layer], q["ln3_b"][layer], row_tm=row_tm)


def _forward(x, mem, p):
    batch, seq_len, _ = x.shape
    mem_len = mem.shape[1]
    x2 = x.reshape(batch * seq_len, D_MODEL).astype(F32)
    mem2 = mem.reshape(batch * mem_len, D_MODEL).astype(F32)
    q = _prepare(p)
    for layer in range(DEPTH):
        x2 = _token_mixing(x2, q, layer, batch=batch, seq_len=seq_len)
        x2 = _cross_attention(x2, mem2, q, layer, batch=batch, seq_len=seq_len, mem_len=mem_len)
        x2 = _dense_ffn(x2, q, layer) if layer % 2 == 0 else _moe_ffn(x2, q, layer)
    return x2.reshape(batch, seq_len, D_MODEL)


def kernel(x, mem, w_in, ssm_conv_w, ssm_conv_b, ssm_dt_bias, ssm_a_log, ssm_d, ssm_norm_w, fox_f_bias, pool_w, pool_b, pool_scale, w_out, ln1_g, ln1_b, xa_wq, xa_wk, xa_wv, xa_wo, ln2_g, ln2_b, ffn_w1, ffn_w3, ffn_w2, router_w, moe_w1, moe_w3, moe_w2, ln3_g, ln3_b):
    p = dict(w_in=w_in, ssm_conv_w=ssm_conv_w, ssm_conv_b=ssm_conv_b, ssm_dt_bias=ssm_dt_bias,
             ssm_a_log=ssm_a_log, ssm_d=ssm_d, ssm_norm_w=ssm_norm_w, fox_f_bias=fox_f_bias, pool_w=pool_w,
             pool_b=pool_b, pool_scale=pool_scale, w_out=w_out, ln1_g=ln1_g, ln1_b=ln1_b, xa_wq=xa_wq,
             xa_wk=xa_wk, xa_wv=xa_wv, xa_wo=xa_wo, ln2_g=ln2_g, ln2_b=ln2_b, ffn_w1=ffn_w1, ffn_w3=ffn_w3,
             ffn_w2=ffn_w2, router_w=router_w, moe_w1=moe_w1, moe_w3=moe_w3, moe_w2=moe_w2, ln3_g=ln3_g,
             ln3_b=ln3_b)
    return _forward(x, mem, p)
```

```python
import functools

import jax
import jax.numpy as jnp
import numpy as np
from jax import lax
from jax.experimental import pallas as pl
from jax.experimental.pallas import tpu as pltpu

F32 = jnp.float32
BF16 = jnp.bfloat16

D_MODEL = 1024
DEPTH = 4
SSD_WIDTH = 512
SSD_HEAD_DIM = 64
SSD_HEADS = 8
SSD_GROUPS = 2
SSD_STATE = 128
SSD_CONV = 4
SSD_CONV_CH = 1024
FOX_WIDTH = 256
FOX_HEAD_DIM = 64
FOX_HEADS = 4
POOL_WIDTH = 256
POOL_WINDOWS = (2, 4, 8, 16)
POOL_GROUP_DIM = 64
IN_SIZES = (512, 1024, 8, 256, 256, 256, 4, 256)
XATTN_HEADS = 4
XATTN_HEAD_DIM = 256
D_FF = 3584
N_EXPERTS = 8
TOP_K = 2
DN_ALPHA = (2 * DEPTH) ** 0.25
LN_EPS = 1e-5
RMS_EPS = 1e-5

LANES = 128
MIB = 1024 * 1024

IN_TM = 512
SSD_TB = 512
SSD_L = 128
FOX_TQ = 1024
FOX_TK = 512
FOX_QW = 128
FOX_KSUB = 512
FOX_LOOKAHEAD = 4
POOL_TB = 512
POOL_PIECE = 128
OUT_TM = 512
XA_TM = 512
FFN_TM = 1024
FFN_TF = 512
MOE_TM = 1024
ROUTE_TM = 512
ROW_TM = 512


def _cparams(sem, vmem_mib=48):
    return pltpu.CompilerParams(dimension_semantics=sem, vmem_limit_bytes=vmem_mib * MIB)


def _dot(a, b):
    return jnp.dot(a, b, preferred_element_type=F32)


def _dot_nt(a, b):
    return lax.dot_general(a, b, (((1,), (1,)), ((), ())), preferred_element_type=F32)


def _dot_tn(a, b):
    return lax.dot_general(a, b, (((0,), (0,)), ((), ())), preferred_element_type=F32)


def _split3(x):
    hi = x.astype(BF16)
    r1 = x - hi.astype(F32)
    mid = r1.astype(BF16)
    lo = (r1 - mid.astype(F32)).astype(BF16)
    return hi, mid, lo


def _silu(x):
    return x / (1.0 + jnp.exp(-x))


def _softplus(x):
    return jnp.maximum(x, 0.0) + jnp.log1p(jnp.exp(-jnp.abs(x)))


def _log_sigmoid(x):
    return jnp.minimum(x, 0.0) - jnp.log1p(jnp.exp(-jnp.abs(x)))


def _layer_norm(v, g, b):
    mu = jnp.mean(v, axis=-1, keepdims=True)
    d = v - mu
    var = jnp.mean(d * d, axis=-1, keepdims=True)
    return d * lax.rsqrt(var + LN_EPS) * g + b


def _tri_ones(n, lower):
    r = lax.broadcasted_iota(jnp.int32, (n, n), 0)
    c = lax.broadcasted_iota(jnp.int32, (n, n), 1)
    m = (c <= r) if lower else (r <= c)
    return jnp.where(m, 1.0, 0.0).astype(BF16)


FOX_SLOT = LANES
FOX_EXT = FOX_HEADS * FOX_SLOT
FOX_BIAS_TERMS = 3
FOX_ONES_ROWS = 16
LOG2E = 1.4426950408889634


def _in_proj_kernel(x_ref, wm_ref, wkx_ref, wqt_ref, wvt_ref, ws_ref, wst_ref, fbc_ref, place_ref, qones_ref,
                    vones_ref, xbc_ref, z_ref, pool_ref, dte_ref, kx_ref, qt_ref, vt_ref, rows_ref,
                    carry_c, *, tiles_per_batch):
    i = pl.program_id(0)
    tm = x_ref.shape[0]
    xb = x_ref[...].astype(BF16)

    xbc_ref[...] = _dot(xb, wm_ref[:, 0:1024])
    z_ref[...] = _dot(xb, wm_ref[:, 1024:1536])
    pool_ref[...] = _dot(xb, wm_ref[:, 1536:1792])
    dte_ref[...] = _dot(xb, wm_ref[:, 1792:2304])
    qt_ref[...] = (_dot_nt(wqt_ref[...], xb) + qones_ref[...]).astype(BF16)
    vt_ref[...] = (_dot_nt(wvt_ref[...], xb) + vones_ref[...]).astype(BF16)
    rows_ref[...] = _dot_nt(wst_ref[...], xb)

    @pl.when(i % tiles_per_batch == 0)
    def _():
        carry_c[...] = jnp.zeros_like(carry_c)

    small_c = _dot(xb, ws_ref[...])
    c3 = _split3(_log_sigmoid(small_c + fbc_ref[...]))
    tri = _tri_ones(tm, lower=True)
    cs_c = _dot(tri, c3[0]) + _dot(tri, c3[1]) + _dot(tri, c3[2]) + carry_c[...]
    carry_c[...] = cs_c[tm - 1:tm, :]
    n3 = _split3(cs_c * (-LOG2E))
    kx = _dot(xb, wkx_ref[...])
    for term in range(FOX_BIAS_TERMS):
        kx = kx + _dot(n3[term], place_ref[term])
    kx_ref[...] = kx.astype(BF16)


def _in_proj(x2, wm, wkx, wqt, wvt, ws, wst, fbc, place, qones, vones, *, seq_len, tm):
    n = x2.shape[0]
    grid = (n // tm,)
    full = lambda a: pl.BlockSpec(a.shape, lambda i: (0,) * a.ndim)
    rowblk = lambda w: pl.BlockSpec((tm, w), lambda i: (i, 0))
    colblk = lambda h: pl.BlockSpec((h, tm), lambda i: (0, i))
    out_shape = (
        jax.ShapeDtypeStruct((n, 1024), F32),
        jax.ShapeDtypeStruct((n, 512), F32),
        jax.ShapeDtypeStruct((n, 256), F32),
        jax.ShapeDtypeStruct((n, 512), F32),
        jax.ShapeDtypeStruct((n, FOX_EXT), BF16),
        jax.ShapeDtypeStruct((FOX_EXT, n), BF16),
        jax.ShapeDtypeStruct((FOX_EXT, n), BF16),
        jax.ShapeDtypeStruct((16, n), F32),
    )
    out_specs = (rowblk(1024), rowblk(512), rowblk(256), rowblk(512), rowblk(FOX_EXT),
                 colblk(FOX_EXT), colblk(FOX_EXT), colblk(16))
    args = (x2, wm, wkx, wqt, wvt, ws, wst, fbc, place, qones, vones)
    return pl.pallas_call(
        functools.partial(_in_proj_kernel, tiles_per_batch=seq_len // tm),
        out_shape=out_shape,
        grid=grid,
        in_specs=[rowblk(D_MODEL)] + [full(a) for a in args[1:]],
        out_specs=out_specs,
        scratch_shapes=[pltpu.VMEM((1, LANES), F32)],
        compiler_params=_cparams(("arbitrary",)),
        name="in_proj",
    )(*args)


def _ssd_kernel(xbc_ref, z_ref, dte_ref, rows_ref, cw_ref, cb_ref, dtb_ref, alog_ref, dsk_ref, nw_ref,
                dtbr_ref, alogr_ref, y_ref, tail, ubuf, xc, state, *, chunk):
    t = pl.program_id(1)
    tb = xbc_ref.shape[0]
    L = chunk
    CONV_PIECE = 64
    HALO = 8

    @pl.when(t == 0)
    def _():
        tail[...] = jnp.zeros_like(tail)
        state[...] = jnp.zeros_like(state)

    ubuf[0:HALO, :] = tail[...]
    ubuf[HALO:HALO + tb, :] = xbc_ref[...]
    tail[...] = xbc_ref[tb - HALO:tb, :]
    for r0 in range(0, tb, CONV_PIECE):
        window = ubuf[r0:r0 + HALO + CONV_PIECE, :]
        acc = jnp.broadcast_to(cb_ref[...], (CONV_PIECE, SSD_CONV_CH))
        for k in range(SSD_CONV):
            delay = SSD_CONV - 1 - k
            tap = window if delay == 0 else pltpu.roll(window, shift=delay, axis=0)
            acc = acc + cw_ref[k:k + 1, :] * tap[HALO:HALO + CONV_PIECE, :]
        xc[r0:r0 + CONV_PIECE, :] = _silu(acc)

    a_e = -jnp.exp(alog_ref[...])
    a_r = -jnp.exp(alogr_ref[...])
    tri = _tri_ones(L, lower=True)
    upp = _tri_ones(L, lower=False)
    rr = lax.broadcasted_iota(jnp.int32, (L, L), 0)
    cc = lax.broadcasted_iota(jnp.int32, (L, L), 1)
    causal = cc <= rr
    lane = lax.broadcasted_iota(jnp.int32, (L, LANES), 1)
    left = lane < SSD_HEAD_DIM
    HG = SSD_HEADS // SSD_GROUPS
    GW = SSD_WIDTH // SSD_GROUPS

    def body(c, carry):
        r0 = pl.multiple_of(c * L, L)
        rows = pl.ds(r0, L)
        xs = xc[rows, 0:SSD_WIDTH]
        dt_e = _softplus(dte_ref[rows, :] + dtb_ref[...])
        a3 = _split3(dt_e * a_e)
        acs_e = _dot(tri, a3[0]) + _dot(tri, a3[1]) + _dot(tri, a3[2])
        dt_r = _softplus(rows_ref[:, rows] + dtbr_ref[...])
        ar3 = _split3(dt_r * a_r)
        acs_r = _dot(ar3[0], upp) + _dot(ar3[1], upp) + _dot(ar3[2], upp)
        total = acs_e[L - 1:L, :]
        x_dt = xs * dt_e
        xb = x_dt.astype(BF16)
        xd = (x_dt * jnp.exp(total - acs_e)).astype(BF16)
        eacs = jnp.exp(acs_e)
        y_parts = []
        for g in range(SSD_GROUPS):
            bm = xc[rows, SSD_WIDTH + g * SSD_STATE:SSD_WIDTH + (g + 1) * SSD_STATE].astype(BF16)
            cm = xc[rows, SSD_WIDTH + (SSD_GROUPS + g) * SSD_STATE:
                    SSD_WIDTH + (SSD_GROUPS + g + 1) * SSD_STATE].astype(BF16)
            cb = _dot_nt(cm, bm)
            for pair in range(HG // 2):
                lo = g * GW + pair * LANES
                x_pair = xb[:, lo:lo + LANES]
                outs = []
                for sub in range(2):
                    h = g * HG + pair * 2 + sub
                    seg = acs_e[:, h * SSD_HEAD_DIM:h * SSD_HEAD_DIM + 1] - acs_r[h:h + 1, :]
                    decay = jnp.exp(jnp.where(causal, seg, -jnp.inf))
                    outs.append(_dot((cb * decay).astype(BF16), x_pair))
                y_parts.append(jnp.where(left, outs[0], outs[1]))
            s_prev = state[g]
            y_off = _dot(cm, s_prev.astype(BF16)) * eacs[:, g * GW:(g + 1) * GW]
            y_parts[-2] = y_parts[-2] + y_off[:, 0:LANES]
            y_parts[-1] = y_parts[-1] + y_off[:, LANES:2 * LANES]
            state[g] = s_prev * jnp.exp(total[:, g * GW:(g + 1) * GW]) + _dot_tn(bm, xd[:, g * GW:(g + 1) * GW])
        y = jnp.concatenate(y_parts, axis=1) + xs * dsk_ref[...]
        y = y * _silu(z_ref[rows, :])
        outs = []
        for g in range(SSD_GROUPS):
            yg = y[:, g * GW:(g + 1) * GW]
            ms = jnp.mean(yg * yg, axis=-1, keepdims=True)
            outs.append(yg * lax.rsqrt(ms + RMS_EPS))
        y_ref[rows, :] = (jnp.concatenate(outs, axis=1) * nw_ref[...]).astype(y_ref.dtype)
        return carry

    lax.fori_loop(0, tb // L, body, 0, unroll=True)


def _ssd(xbc, z, dte, rowsp, cw, cb, dtb_e, alog_e, dsk_e, nw, dtb_r, alog_r, *, batch, seq_len, tb, chunk):
    n = xbc.shape[0]
    tpb = seq_len // tb
    full = lambda a: pl.BlockSpec(a.shape, lambda b, t: (0,) * a.ndim)
    rowblk = lambda w: pl.BlockSpec((tb, w), lambda b, t: (b * tpb + t, 0))
    return pl.pallas_call(
        functools.partial(_ssd_kernel, chunk=chunk),
        out_shape=jax.ShapeDtypeStruct((n, SSD_WIDTH), BF16),
        grid=(batch, tpb),
        in_specs=[rowblk(SSD_CONV_CH), rowblk(SSD_WIDTH), rowblk(SSD_WIDTH),
                  pl.BlockSpec((16, tb), lambda b, t: (0, b * tpb + t)),
                  full(cw), full(cb), full(dtb_e), full(alog_e), full(dsk_e), full(nw),
                  full(dtb_r), full(alog_r)],
        out_specs=rowblk(SSD_WIDTH),
        scratch_shapes=[pltpu.VMEM((8, SSD_CONV_CH), F32),
                        pltpu.VMEM((tb + 8, SSD_CONV_CH), F32),
                        pltpu.VMEM((tb, SSD_CONV_CH), F32),
                        pltpu.VMEM((SSD_GROUPS, SSD_STATE, SSD_WIDTH // SSD_GROUPS), F32)],
        compiler_params=_cparams(("parallel", "arbitrary")),
        name="ssd_mixer",
    )(xbc, z, dte, rowsp, cw, cb, dtb_e, alog_e, dsk_e, nw, dtb_r, alog_r)


def _fox_kernel(qi_ref, kj_ref, kx_ref, qt_ref, vt_ref, o_ref, m_sc, acc_sc, *, ratio, qw):
    step_id = pl.program_id(1)
    i = qi_ref[step_id]
    j = kj_ref[step_id]
    tk = kx_ref.shape[0]
    tq = qt_ref.shape[1]
    vrows = FOX_HEAD_DIM + FOX_ONES_ROWS

    @pl.when(j == 0)
    def _():
        m_sc[...] = jnp.full_like(m_sc, -jnp.inf)
        acc_sc[...] = jnp.zeros_like(acc_sc)

    def step(diag):
        ksub = min(FOX_KSUB, tk)
        units = [(k0, h, c0) for k0 in range(0, tk, ksub) for h in range(FOX_HEADS) for c0 in range(0, tq, qw)]
        if diag is not None:
            units = [(k0, h, c0) for k0, h, c0 in units if c0 + qw > diag * tk + k0]

        def score(unit):
            k0, h, c0 = unit
            return _dot(kx_ref[k0:k0 + ksub, h * FOX_SLOT:(h + 1) * FOX_SLOT],
                        qt_ref[h * FOX_SLOT:(h + 1) * FOX_SLOT, c0:c0 + qw])

        scores = [score(u) for u in units[:FOX_LOOKAHEAD]]
        for n_unit, (k0, h, c0) in enumerate(units):
            if n_unit + FOX_LOOKAHEAD < len(units):
                scores.append(score(units[n_unit + FOX_LOOKAHEAD]))
            st = scores[n_unit]
            lo = h * FOX_SLOT
            if diag is not None and c0 < diag * tk + k0 + ksub - 1:
                key = diag * tk + k0 + lax.broadcasted_iota(jnp.int32, (ksub, qw), 0)
                qry = c0 + lax.broadcasted_iota(jnp.int32, (ksub, qw), 1)
                st = jnp.where(key <= qry, st, -jnp.inf)
            m_prev = m_sc[h, :, c0:c0 + qw]
            m_new = jnp.maximum(m_prev, jnp.max(st, axis=0, keepdims=True))
            alpha = jnp.exp2(m_prev - m_new)
            p = jnp.exp2(st - m_new).astype(BF16)
            acc_sc[h, :, c0:c0 + qw] = (alpha * acc_sc[h, :, c0:c0 + qw]
                                        + _dot(vt_ref[lo:lo + vrows, k0:k0 + ksub], p))
            m_sc[h, :, c0:c0 + qw] = m_new

    @pl.when(j < i * ratio)
    def _():
        step(None)

    for r in range(ratio):
        @pl.when(j == i * ratio + r)
        def _():
            step(r)

    @pl.when(j == (i + 1) * ratio - 1)
    def _():
        parts = [acc_sc[h, 0:FOX_HEAD_DIM, :] / acc_sc[h, FOX_HEAD_DIM:FOX_HEAD_DIM + 1, :]
                 for h in range(FOX_HEADS)]
        o_ref[...] = jnp.concatenate(parts, axis=0).T.astype(o_ref.dtype)


def _fox(kx, qt, vt, *, batch, seq_len, tq, tk):
    n = kx.shape[0]
    assert tq % tk == 0
    ratio = tq // tk
    nq = seq_len // tq
    nk = seq_len // tk
    pairs = [(i, j) for i in range(nq) for j in range((i + 1) * ratio)]
    qi = jnp.asarray([pq for pq, _ in pairs], jnp.int32)
    kj = jnp.asarray([pk for _, pk in pairs], jnp.int32)
    return pl.pallas_call(
        functools.partial(_fox_kernel, ratio=ratio, qw=min(FOX_QW, tq)),
        out_shape=jax.ShapeDtypeStruct((n, FOX_WIDTH), BF16),
        grid_spec=pltpu.PrefetchScalarGridSpec(
            num_scalar_prefetch=2,
            grid=(batch, len(pairs)),
            in_specs=[pl.BlockSpec((tk, FOX_EXT), lambda b, s, qi, kj: (b * nk + kj[s], 0)),
                      pl.BlockSpec((FOX_EXT, tq), lambda b, s, qi, kj: (0, b * nq + qi[s])),
                      pl.BlockSpec((FOX_EXT, tk), lambda b, s, qi, kj: (0, b * nk + kj[s]))],
            out_specs=pl.BlockSpec((tq, FOX_WIDTH), lambda b, s, qi, kj: (b * nq + qi[s], 0)),
            scratch_shapes=[pltpu.VMEM((FOX_HEADS, 1, tq), F32),
                            pltpu.VMEM((FOX_HEADS, FOX_HEAD_DIM + FOX_ONES_ROWS, tq), F32)]),
        compiler_params=_cparams(("parallel", "arbitrary")),
        name="fox_attention",
    )(qi, kj, kx, qt, vt)


POOL_HALO = 16


def _pool_kernel(u_ref, w_ref, b_ref, sc_ref, y_ref, tail, ubuf):
    t = pl.program_id(1)
    tb = u_ref.shape[0]
    P = POOL_PIECE

    @pl.when(t == 0)
    def _():
        tail[...] = jnp.zeros_like(tail)

    ubuf[0:POOL_HALO, :] = tail[...]
    ubuf[POOL_HALO:POOL_HALO + tb, :] = u_ref[...]
    tail[...] = u_ref[tb - POOL_HALO:tb, :]
    group = lax.broadcasted_iota(jnp.int32, (P, POOL_WIDTH), 1) // POOL_GROUP_DIM
    for p0 in range(0, tb, P):
        w1 = ubuf[p0:p0 + P + POOL_HALO, :]
        n1 = P + POOL_HALO
        a2 = w1[1:n1] + w1[0:n1 - 1]
        a4 = a2[2:n1 - 1] + a2[0:n1 - 3]
        a8 = a4[4:n1 - 3] + a4[0:n1 - 7]
        a16 = a8[8:n1 - 7] + a8[0:n1 - 15]
        sums = (a2[15:15 + P], a4[13:13 + P], a8[9:9 + P], a16[1:1 + P])
        tpos = t * tb + p0 + lax.broadcasted_iota(jnp.int32, (P, 1), 0) + 1
        pooled = jnp.zeros((P, POOL_WIDTH), F32)
        for g, win in enumerate(POOL_WINDOWS):
            cnt = jnp.minimum(tpos, win).astype(F32)
            pooled = jnp.where(group == g, sums[g] / cnt, pooled)
        pooled = pooled - w1[POOL_HALO:POOL_HALO + P]
        y = _dot(pooled.astype(BF16), w_ref[...]) + b_ref[...]
        y_ref[p0:p0 + P, :] = (y * sc_ref[...]).astype(y_ref.dtype)


def _pool(u, wbd, b, sc, *, batch, seq_len, tb):
    n = u.shape[0]
    tpb = seq_len // tb
    full = lambda a: pl.BlockSpec(a.shape, lambda bb, t: (0,) * a.ndim)
    rowblk = pl.BlockSpec((tb, POOL_WIDTH), lambda bb, t: (bb * tpb + t, 0))
    return pl.pallas_call(
        _pool_kernel,
        out_shape=jax.ShapeDtypeStruct((n, POOL_WIDTH), BF16),
        grid=(batch, tpb),
        in_specs=[rowblk, full(wbd), full(b), full(sc)],
        out_specs=rowblk,
        scratch_shapes=[pltpu.VMEM((POOL_HALO, POOL_WIDTH), F32),
                        pltpu.VMEM((tb + POOL_HALO, POOL_WIDTH), F32)],
        compiler_params=_cparams(("parallel", "arbitrary")),
        name="pool_mixer",
    )(u, wbd, b, sc)


def _out_proj_kernel(x_ref, ys_ref, yf_ref, yp_ref, w_ref, g_ref, b_ref, o_ref):
    mix = (_dot(ys_ref[...], w_ref[0:512, :]) + _dot(yf_ref[...], w_ref[512:768, :])
           + _dot(yp_ref[...], w_ref[768:1024, :]))
    o_ref[...] = _layer_norm(DN_ALPHA * x_ref[...] + mix, g_ref[...], b_ref[...])


def _out_proj(x2, ys, yf, yp, w, g, b, *, tm):
    n = x2.shape[0]
    full = lambda a: pl.BlockSpec(a.shape, lambda i: (0,) * a.ndim)
    rowblk = lambda wd: pl.BlockSpec((tm, wd), lambda i: (i, 0))
    return pl.pallas_call(
        _out_proj_kernel,
        out_shape=jax.ShapeDtypeStruct((n, D_MODEL), F32),
        grid=(n // tm,),
        in_specs=[rowblk(D_MODEL), rowblk(512), rowblk(256), rowblk(256), full(w), full(g), full(b)],
        out_specs=rowblk(D_MODEL),
        compiler_params=_cparams(("parallel",)),
        name="out_proj_ln",
    )(x2, ys, yf, yp, w, g, b)


def _kv_proj_kernel(m_ref, w_ref, o_ref):
    o_ref[...] = _dot(m_ref[...].astype(BF16), w_ref[...]).astype(o_ref.dtype)


def _kv_proj(mem2, wkv):
    m = mem2.shape[0]
    return pl.pallas_call(
        _kv_proj_kernel,
        out_shape=jax.ShapeDtypeStruct((m, wkv.shape[1]), BF16),
        grid=(1,),
        in_specs=[pl.BlockSpec(mem2.shape, lambda i: (0, 0)), pl.BlockSpec(wkv.shape, lambda i: (0, 0))],
        out_specs=pl.BlockSpec((m, wkv.shape[1]), lambda i: (0, 0)),
        compiler_params=_cparams(("arbitrary",)),
        name="xattn_kv_proj",
    )(mem2, wkv)


def _xattn_kernel(x_ref, kv_ref, wq_ref, wo_ref, g_ref, b_ref, o_ref):
    x = x_ref[...]
    q = _dot(x.astype(BF16), wq_ref[...]).astype(BF16)
    heads = []
    for h in range(XATTN_HEADS):
        lo = h * XATTN_HEAD_DIM
        k_h = kv_ref[:, lo:lo + XATTN_HEAD_DIM]
        v_h = kv_ref[:, D_MODEL + lo:D_MODEL + lo + XATTN_HEAD_DIM]
        s = _dot_nt(q[:, lo:lo + XATTN_HEAD_DIM], k_h)
        p = jnp.exp(s - jnp.max(s, axis=-1, keepdims=True))
        l = jnp.sum(p, axis=-1, keepdims=True)
        heads.append((_dot(p.astype(BF16), v_h) / l).astype(BF16))
    o = jnp.concatenate(heads, axis=1)
    xa = _dot(o, wo_ref[...])
    o_ref[...] = _layer_norm(DN_ALPHA * x + xa, g_ref[...], b_ref[...])


def _xattn(x2, kv, wq, wo, g, b, *, batch, seq_len, mem_len, tm):
    n = x2.shape[0]
    tpb = seq_len // tm
    full = lambda a: pl.BlockSpec(a.shape, lambda bb, t: (0,) * a.ndim)
    rowblk = pl.BlockSpec((tm, D_MODEL), lambda bb, t: (bb * tpb + t, 0))
    return pl.pallas_call(
        _xattn_kernel,
        out_shape=jax.ShapeDtypeStruct((n, D_MODEL), F32),
        grid=(batch, tpb),
        in_specs=[rowblk, pl.BlockSpec((mem_len, 2 * D_MODEL), lambda bb, t: (bb, 0)),
                  full(wq), full(wo), full(g), full(b)],
        out_specs=rowblk,
        compiler_params=_cparams(("parallel", "parallel")),
        name="xattn_ln",
    )(x2, kv, wq, wo, g, b)


ROW_SLABS = D_MODEL // LANES


def _ffn_kernel(texp_ref, nvalid_ref, x_ref, w1_ref, w3_ref, w2_ref, g_ref, b_ref, o_ref, xb_sc, acc_sc,
                *, grouped):
    t = pl.program_id(0)
    f = pl.program_id(1)
    nf = pl.num_programs(1)

    @pl.when(t < nvalid_ref[0])
    def _():
        @pl.when(f == 0)
        def _():
            xb_sc[...] = x_ref[...].reshape(xb_sc.shape).astype(BF16)
            acc_sc[...] = jnp.zeros_like(acc_sc)

        xb = xb_sc[...]
        h1 = _dot(xb, w1_ref[0].astype(BF16))
        h3 = _dot(xb, w3_ref[0].astype(BF16))
        acc_sc[...] += _dot((_silu(h1) * h3).astype(BF16), w2_ref[0].astype(BF16))

        @pl.when(f == nf - 1)
        def _():
            if grouped:
                o_ref[...] = acc_sc[...].reshape(o_ref.shape)
            else:
                o_ref[...] = _layer_norm(DN_ALPHA * x_ref[...] + acc_sc[...], g_ref[...], b_ref[...])

    @pl.when((t >= nvalid_ref[0]) & (f == 0))
    def _():
        o_ref[...] = jnp.zeros_like(o_ref)


def _ffn(tile_expert, nvalid, x, w1, w3, w2, g, b, *, tm, tf, grouped):
    n = x.shape[0]
    nt = n // tm
    nf = D_FF // tf

    def tile_of(t, nv):
        return jnp.minimum(t, nv[0] - 1)

    def f_of(t, f, nv):
        return jnp.where(t < nv[0], f, nf - 1)

    if grouped:
        rowblk = pl.BlockSpec((tm, ROW_SLABS, LANES), lambda t, f, te, nv: (tile_of(t, nv), 0, 0))
        outblk = pl.BlockSpec((tm, ROW_SLABS, LANES), lambda t, f, te, nv: (t, 0, 0))
        out_shape = jax.ShapeDtypeStruct((n, ROW_SLABS, LANES), F32)
    else:
        rowblk = pl.BlockSpec((tm, D_MODEL), lambda t, f, te, nv: (tile_of(t, nv), 0))
        outblk = pl.BlockSpec((tm, D_MODEL), lambda t, f, te, nv: (t, 0))
        out_shape = jax.ShapeDtypeStruct((n, D_MODEL), F32)
    w13 = pl.BlockSpec((1, D_MODEL, tf), lambda t, f, te, nv: (te[tile_of(t, nv)], 0, f_of(t, f, nv)))
    w2s = pl.BlockSpec((1, tf, D_MODEL), lambda t, f, te, nv: (te[tile_of(t, nv)], f_of(t, f, nv), 0))
    vec = pl.BlockSpec((1, D_MODEL), lambda t, f, te, nv: (0, 0))
    return pl.pallas_call(
        functools.partial(_ffn_kernel, grouped=grouped),
        out_shape=out_shape,
        grid_spec=pltpu.PrefetchScalarGridSpec(
            num_scalar_prefetch=2,
            grid=(nt, nf),
            in_specs=[rowblk, w13, w13, w2s, vec, vec],
            out_specs=outblk,
            scratch_shapes=[pltpu.VMEM((tm, D_MODEL), BF16), pltpu.VMEM((tm, D_MODEL), F32)]),
        compiler_params=_cparams(("arbitrary", "arbitrary"), vmem_mib=56),
        name="swiglu_grouped" if grouped else "swiglu_ln",
    )(tile_expert, nvalid, x, w1, w3, w2, g, b)


def _router_kernel(x_ref, w_ref, route_ref, counts_ref, carry):
    i = pl.program_id(0)
    tm = x_ref.shape[0]

    @pl.when(i == 0)
    def _():
        carry[...] = jnp.zeros_like(carry)

    xh, xm, _ = _split3(x_ref[...])
    wh, wm, _ = _split3(w_ref[...])
    logits = _dot(xh, wh) + (_dot(xh, wm) + _dot(xm, wh))
    lane = lax.broadcasted_iota(jnp.int32, (tm, LANES), 1)
    logits = jnp.where(lane < N_EXPERTS, logits, -jnp.inf)
    m1 = jnp.max(logits, axis=-1, keepdims=True)
    i1 = jnp.min(jnp.where(logits == m1, lane, LANES), axis=-1, keepdims=True)
    rest = jnp.where(lane == i1, -jnp.inf, logits)
    m2 = jnp.max(rest, axis=-1, keepdims=True)
    i2 = jnp.min(jnp.where(rest == m2, lane, LANES), axis=-1, keepdims=True)
    e21 = jnp.exp(m2 - m1)
    g1 = 1.0 / (1.0 + e21)
    g2 = e21 / (1.0 + e21)
    hit1 = lane == i1
    hit2 = lane == i2
    onehot = jnp.where(hit1 | hit2, 1.0, 0.0).astype(BF16)
    r = lax.broadcasted_iota(jnp.int32, (tm, tm), 0)
    c = lax.broadcasted_iota(jnp.int32, (tm, tm), 1)
    strict = jnp.where(c < r, 1.0, 0.0).astype(BF16)
    before = _dot(strict, onehot) + carry[...]
    rank1 = jnp.sum(jnp.where(hit1, before, 0.0), axis=-1, keepdims=True)
    rank2 = jnp.sum(jnp.where(hit2, before, 0.0), axis=-1, keepdims=True)
    carry[...] = carry[...] + jnp.sum(onehot.astype(F32), axis=0, keepdims=True)
    out = jnp.where(lane == 0, i1.astype(F32), 0.0)
    out = jnp.where(lane == 1, i2.astype(F32), out)
    out = jnp.where(lane == 2, g1, out)
    out = jnp.where(lane == 3, g2, out)
    out = jnp.where(lane == 4, rank1, out)
    out = jnp.where(lane == 5, rank2, out)
    route_ref[...] = out
    counts_ref[...] = carry[...]


def _router(x2, wr, *, tm):
    n = x2.shape[0]
    return pl.pallas_call(
        _router_kernel,
        out_shape=(jax.ShapeDtypeStruct((n, LANES), F32), jax.ShapeDtypeStruct((1, LANES), F32)),
        grid=(n // tm,),
        in_specs=[pl.BlockSpec((tm, D_MODEL), lambda i: (i, 0)), pl.BlockSpec(wr.shape, lambda i: (0, 0))],
        out_specs=(pl.BlockSpec((tm, LANES), lambda i: (i, 0)), pl.BlockSpec((1, LANES), lambda i: (0, 0))),
        scratch_shapes=[pltpu.VMEM((1, LANES), F32)],
        compiler_params=_cparams(("arbitrary",)),
        name="moe_router",
    )(x2, wr)


POS_ROWS = 8


ROW_UNROLL = 8


def _for_each_row_copy(row_copy, action, pos_rows=(0, POS_ROWS)):
    tokens_per_pos_row = LANES // TOP_K
    for prow in range(*pos_rows):
        def body(c, carry):
            for k in range(TOP_K):
                copy = row_copy(prow * tokens_per_pos_row + c, (prow, TOP_K * c + k), k)
                getattr(copy, action)()
            return carry

        lax.fori_loop(0, tokens_per_pos_row, body, 0, unroll=ROW_UNROLL)


def _dispatch_kernel(pend_ref, padded_ref, pos_hbm, x_ref, xs_hbm, pos_smem, stage, zeros_vmem, sem_pos, sem_rows,
                     *, row_tm, pad_tm):
    i = pl.program_id(0)
    last = pl.num_programs(0) - 1

    def zero_copy(start):
        return pltpu.make_async_copy(zeros_vmem, xs_hbm.at[pl.ds(start, pad_tm)], sem_rows.at[0])

    @pl.when(i == 0)
    def _():
        zeros_vmem[...] = jnp.zeros_like(zeros_vmem)
        used = pend_ref[N_EXPERTS - 1]
        fills = [(padded_ref[e] > 0, pend_ref[e] - pad_tm) for e in range(N_EXPERTS)]
        fills += [(used + e * pad_tm < xs_hbm.shape[0], used + e * pad_tm) for e in range(N_EXPERTS)]
        for cond, start in fills:
            @pl.when(cond)
            def _():
                zero_copy(start).start()
        for cond, start in fills:
            @pl.when(cond)
            def _():
                zero_copy(start).wait()

    def run(slot):
        def row_copies(s):
            def row_copy(r, entry, k):
                return pltpu.make_async_copy(stage.at[s, r], xs_hbm.at[pos_smem[(s,) + entry]], sem_rows.at[s])
            return row_copy

        pos_copy = pltpu.make_async_copy(pos_hbm.at[i], pos_smem.at[slot], sem_pos)
        pos_copy.start()
        stage[slot] = x_ref[...].reshape(stage.shape[1:])
        pos_copy.wait()
        _for_each_row_copy(row_copies(slot), "start")

        @pl.when(i > 0)
        def _():
            _for_each_row_copy(row_copies(1 - slot), "wait")

        @pl.when(i == last)
        def _():
            _for_each_row_copy(row_copies(slot), "wait")

    for slot in range(2):
        @pl.when(i % 2 == slot)
        def _():
            run(slot)


def _dispatch(pend, padded, pos3, x2, *, cap, row_tm, pad_tm):
    n = x2.shape[0]
    assert TOP_K * row_tm == POS_ROWS * LANES
    return pl.pallas_call(
        functools.partial(_dispatch_kernel, row_tm=row_tm, pad_tm=pad_tm),
        out_shape=jax.ShapeDtypeStruct((cap, ROW_SLABS, LANES), F32),
        grid_spec=pltpu.PrefetchScalarGridSpec(
            num_scalar_prefetch=2,
            grid=(n // row_tm,),
            in_specs=[pl.BlockSpec(memory_space=pl.ANY),
                      pl.BlockSpec((row_tm, D_MODEL), lambda i, pe, pa: (i, 0))],
            out_specs=pl.BlockSpec(memory_space=pl.ANY),
            scratch_shapes=[pltpu.SMEM((2, POS_ROWS, LANES), jnp.int32),
                            pltpu.VMEM((2, row_tm, ROW_SLABS, LANES), F32),
                            pltpu.VMEM((pad_tm, ROW_SLABS, LANES), F32),
                            pltpu.SemaphoreType.DMA(()),
                            pltpu.SemaphoreType.DMA((2,))]),
        compiler_params=_cparams(("arbitrary",)),
        name="moe_dispatch",
    )(pend, padded, pos3, x2)


def _combine_kernel(pos_hbm, x_ref, route_ref, ys_hbm, g_ref, b_ref, o_ref, pos_smem, rows0, rows1,
                    sem_pos, sem_rows, *, row_tm):
    i = pl.program_id(0)
    pos_copy = pltpu.make_async_copy(pos_hbm.at[i], pos_smem, sem_pos)
    pos_copy.start()
    pos_copy.wait()
    rows = (rows0, rows1)

    halves = ((0, POS_ROWS // 2), (POS_ROWS // 2, POS_ROWS))
    half_tm = row_tm // 2

    def row_copies(half):
        def row_copy(r, entry, k):
            return pltpu.make_async_copy(ys_hbm.at[pos_smem[entry]], rows[k].at[r], sem_rows.at[half])
        return row_copy

    for half, pos_rows in enumerate(halves):
        _for_each_row_copy(row_copies(half), "start", pos_rows)
    for half, pos_rows in enumerate(halves):
        _for_each_row_copy(row_copies(half), "wait", pos_rows)
        tok = pl.ds(half * half_tm, half_tm)
        flat = (half_tm, D_MODEL)
        ff = (rows0[tok].reshape(flat) * route_ref[tok, 2:3] + rows1[tok].reshape(flat) * route_ref[tok, 3:4])
        o_ref[tok, :] = _layer_norm(DN_ALPHA * x_ref[tok, :] + ff, g_ref[...], b_ref[...])


def _combine(pos3, x2, route, ys, g, b, *, row_tm):
    n = x2.shape[0]
    assert TOP_K * row_tm == POS_ROWS * LANES
    rowblk = lambda w: pl.BlockSpec((row_tm, w), lambda i: (i, 0))
    vec = pl.BlockSpec((1, D_MODEL), lambda i: (0, 0))
    return pl.pallas_call(
        functools.partial(_combine_kernel, row_tm=row_tm),
        out_shape=jax.ShapeDtypeStruct((n, D_MODEL), F32),
        grid=(n // row_tm,),
        in_specs=[pl.BlockSpec(memory_space=pl.ANY), rowblk(D_MODEL), rowblk(LANES),
                  pl.BlockSpec(memory_space=pl.ANY), vec, vec],
        out_specs=rowblk(D_MODEL),
        scratch_shapes=[pltpu.SMEM((POS_ROWS, LANES), jnp.int32),
                        pltpu.VMEM((row_tm, ROW_SLABS, LANES), F32),
                        pltpu.VMEM((row_tm, ROW_SLABS, LANES), F32),
                        pltpu.SemaphoreType.DMA(()),
                        pltpu.SemaphoreType.DMA((2,))],
        compiler_params=_cparams(("arbitrary",)),
        name="moe_combine_ln",
    )(pos3, x2, route, ys, g, b)


def _fox_constants():
    place = np.zeros((FOX_BIAS_TERMS, LANES, FOX_EXT), np.float32)
    qones = np.zeros((FOX_EXT, 1), np.float32)
    vones = np.zeros((FOX_EXT, 1), np.float32)
    for h in range(FOX_HEADS):
        base = h * FOX_SLOT + FOX_HEAD_DIM
        for term in range(FOX_BIAS_TERMS):
            place[term, SSD_HEADS + h, base + term] = 1.0
        qones[base:base + FOX_BIAS_TERMS] = 1.0
        vones[base:base + FOX_ONES_ROWS] = 1.0
    return jnp.asarray(place, BF16), jnp.asarray(qones), jnp.asarray(vones)


def _prepare(p):
    depth = p["w_in"].shape[0]
    offs = np.concatenate([[0], np.cumsum(IN_SIZES)])
    w_in = p["w_in"].astype(F32)
    wz, wxbc, wdt, wq, wk, wv, wf, wp = (w_in[:, :, offs[k]:offs[k + 1]] for k in range(len(IN_SIZES)))

    def slots(w):
        w = w.reshape(depth, D_MODEL, FOX_HEADS, FOX_HEAD_DIM)
        w = jnp.pad(w, ((0, 0), (0, 0), (0, 0), (0, FOX_SLOT - FOX_HEAD_DIM)))
        return w.reshape(depth, D_MODEL, FOX_EXT)

    def vec(v, width):
        return v.astype(F32).reshape(v.shape[0], 1, width)

    def per_head(v):
        return jnp.repeat(v.astype(F32), SSD_HEAD_DIM, axis=1).reshape(depth, 1, SSD_WIDTH)

    def head_rows(v):
        return jnp.pad(v.astype(F32), ((0, 0), (0, 16 - SSD_HEADS))).reshape(depth, 16, 1)

    ws = jnp.concatenate([wdt, wf, jnp.zeros((depth, D_MODEL, LANES - SSD_HEADS - FOX_HEADS), F32)], axis=2)
    eye = jnp.asarray(np.eye(len(POOL_WINDOWS), dtype=np.float32))
    q = dict(
        wm=jnp.concatenate([wxbc, wz, wp, jnp.repeat(wdt, SSD_HEAD_DIM, axis=2)], axis=2).astype(BF16),
        wkx=slots(wk).astype(BF16),
        wqt=jnp.swapaxes(slots(wq * (FOX_HEAD_DIM ** -0.5 * LOG2E)), 1, 2).astype(BF16),
        wvt=jnp.swapaxes(slots(wv), 1, 2).astype(BF16),
        ws=ws.astype(BF16),
        wst=jnp.swapaxes(ws[:, :, 0:16], 1, 2).astype(BF16),
        fbc=jnp.pad(p["fox_f_bias"].astype(F32), ((0, 0), (SSD_HEADS, LANES - SSD_HEADS - FOX_HEADS))
                    ).reshape(depth, 1, LANES),
        conv_w=p["ssm_conv_w"].astype(F32), conv_b=vec(p["ssm_conv_b"], SSD_CONV_CH),
        dtb_e=per_head(p["ssm_dt_bias"]), alog_e=per_head(p["ssm_a_log"]), dsk_e=per_head(p["ssm_d"]),
        norm_w=vec(p["ssm_norm_w"], SSD_WIDTH),
        dtb_r=head_rows(p["ssm_dt_bias"]), alog_r=head_rows(p["ssm_a_log"]),
        pool_wbd=jnp.einsum("lgij,gh->lgihj", p["pool_w"].astype(F32), eye
                            ).reshape(depth, POOL_WIDTH, POOL_WIDTH).astype(BF16),
        pool_b=vec(p["pool_b"].reshape(depth, POOL_WIDTH), POOL_WIDTH), pool_sc=vec(p["pool_scale"], POOL_WIDTH),
        w_out=p["w_out"].astype(BF16),
        xa_wkv=jnp.concatenate([p["xa_wk"], p["xa_wv"]], axis=2).astype(BF16),
        xa_wq=(p["xa_wq"] * (XATTN_HEAD_DIM ** -0.5)).astype(BF16),
        xa_wo=p["xa_wo"].astype(BF16),
        ffn_w1=p["ffn_w1"].astype(BF16), ffn_w3=p["ffn_w3"].astype(BF16), ffn_w2=p["ffn_w2"].astype(BF16),
        moe_w1=p["moe_w1"].reshape(-1, D_MODEL, D_FF),
        moe_w3=p["moe_w3"].reshape(-1, D_MODEL, D_FF),
        moe_w2=p["moe_w2"].reshape(-1, D_FF, D_MODEL),
        router=jnp.pad(p["router_w"].astype(F32), ((0, 0), (0, 0), (0, LANES - N_EXPERTS))),
    )
    for name in ("ln1_g", "ln1_b", "ln2_g", "ln2_b", "ln3_g", "ln3_b"):
        q[name] = vec(p[name], D_MODEL)
    return q


def _token_mixing(x2, q, layer, *, batch, seq_len):
    place, qones, vones = _fox_constants()
    xbc, z, pool_in, dte, kx, qt, vt, rowsp = _in_proj(
        x2, q["wm"][layer], q["wkx"][layer], q["wqt"][layer], q["wvt"][layer], q["ws"][layer], q["wst"][layer],
        q["fbc"][layer], place, qones, vones, seq_len=seq_len, tm=min(IN_TM, seq_len))

    y_ssd = _ssd(xbc, z, dte, rowsp, q["conv_w"][layer], q["conv_b"][layer], q["dtb_e"][layer],
                 q["alog_e"][layer], q["dsk_e"][layer], q["norm_w"][layer], q["dtb_r"][layer], q["alog_r"][layer],
                 batch=batch, seq_len=seq_len, tb=min(SSD_TB, seq_len), chunk=SSD_L)

    y_fox = _fox(kx, qt, vt, batch=batch, seq_len=seq_len, tq=min(FOX_TQ, seq_len), tk=min(FOX_TK, seq_len))

    y_pool = _pool(pool_in, q["pool_wbd"][layer], q["pool_b"][layer], q["pool_sc"][layer],
                   batch=batch, seq_len=seq_len, tb=min(POOL_TB, seq_len))

    return _out_proj(x2, y_ssd, y_fox, y_pool, q["w_out"][layer], q["ln1_g"][layer], q["ln1_b"][layer],
                     tm=min(OUT_TM, seq_len))


def _cross_attention(x2, mem2, q, layer, *, batch, seq_len, mem_len):
    kv = _kv_proj(mem2, q["xa_wkv"][layer])
    return _xattn(x2, kv, q["xa_wq"][layer], q["xa_wo"][layer], q["ln2_g"][layer], q["ln2_b"][layer],
                  batch=batch, seq_len=seq_len, mem_len=mem_len, tm=min(XA_TM, seq_len))


def _dense_ffn(x2, q, layer):
    j = layer // 2
    n = x2.shape[0]
    tm = min(FFN_TM, n)
    tile_expert = jnp.full((n // tm,), j, jnp.int32)
    nvalid = jnp.full((1,), n // tm, jnp.int32)
    return _ffn(tile_expert, nvalid, x2, q["ffn_w1"], q["ffn_w3"], q["ffn_w2"],
                q["ln3_g"][layer], q["ln3_b"][layer], tm=tm, tf=FFN_TF, grouped=False)


def _moe_ffn(x2, q, layer):
    j = layer // 2
    n = x2.shape[0]
    tm = min(MOE_TM, n)
    row_tm = min(ROW_TM, n)
    route, counts = _router(x2, q["router"][j], tm=min(ROUTE_TM, n))

    counts = counts[0, :N_EXPERTS].astype(jnp.int32)
    padded = (counts + tm - 1) // tm * tm
    pend = jnp.cumsum(padded)
    pstart = pend - padded
    experts = route[:, 0:TOP_K].astype(jnp.int32)
    ranks = route[:, 4:4 + TOP_K].astype(jnp.int32)
    pos = (pstart[experts] + ranks).astype(jnp.int32)
    pos3 = pos.reshape(n // row_tm, POS_ROWS, LANES)
    cap = n * TOP_K + N_EXPERTS * tm
    ntiles = cap // tm
    tile_start = jnp.arange(ntiles, dtype=jnp.int32) * tm
    tile_expert = jnp.minimum(jnp.sum((pend[None, :] <= tile_start[:, None]).astype(jnp.int32), axis=1),
                              N_EXPERTS - 1).astype(jnp.int32)
    nvalid = (pend[-1:] // tm).astype(jnp.int32)

    xs = _dispatch(pend.astype(jnp.int32), padded.astype(jnp.int32), pos3, x2, cap=cap, row_tm=row_tm, pad_tm=tm)
    ys = _ffn(tile_expert + j * N_EXPERTS, nvalid, xs, q["moe_w1"], q["moe_w3"], q["moe_w2"],
              q["ln3_g"][layer], q["ln3_b"][layer], tm=tm, tf=FFN_TF, grouped=True)
    return _combine(pos3, x2, route, ys, q["ln3_g"][layer], q["ln3_b"][layer], row_tm=row_tm)


def _forward(x, mem, p):
    batch, seq_len, _ = x.shape
    mem_len = mem.shape[1]
    x2 = x.reshape(batch * seq_len, D_MODEL).astype(F32)
    mem2 = mem.reshape(batch * mem_len, D_MODEL).astype(F32)
    q = _prepare(p)
    for layer in range(DEPTH):
        x2 = _token_mixing(x2, q, layer, batch=batch, seq_len=seq_len)
        x2 = _cross_attention(x2, mem2, q, layer, batch=batch, seq_len=seq_len, mem_len=mem_len)
        x2 = _dense_ffn(x2, q, layer) if layer % 2 == 0 else _moe_ffn(x2, q, layer)
    return x2.reshape(batch, seq_len, D_MODEL)


def kernel(x, mem, w_in, ssm_conv_w, ssm_conv_b, ssm_dt_bias, ssm_a_log, ssm_d, ssm_norm_w, fox_f_bias, pool_w, pool_b, pool_scale, w_out, ln1_g, ln1_b, xa_wq, xa_wk, xa_wv, xa_wo, ln2_g, ln2_b, ffn_w1, ffn_w3, ffn_w2, router_w, moe_w1, moe_w3, moe_w2, ln3_g, ln3_b):
    p = dict(w_in=w_in, ssm_conv_w=ssm_conv_w, ssm_conv_b=ssm_conv_b, ssm_dt_bias=ssm_dt_bias,
             ssm_a_log=ssm_a_log, ssm_d=ssm_d, ssm_norm_w=ssm_norm_w, fox_f_bias=fox_f_bias, pool_w=pool_w,
             pool_b=pool_b, pool_scale=pool_scale, w_out=w_out, ln1_g=ln1_g, ln1_b=ln1_b, xa_wq=xa_wq,
             xa_wk=xa_wk, xa_wv=xa_wv, xa_wo=xa_wo, ln2_g=ln2_g, ln2_b=ln2_b, ffn_w1=ffn_w1, ffn_w3=ffn_w3,
             ffn_w2=ffn_w2, router_w=router_w, moe_w1=moe_w1, moe_w3=moe_w3, moe_w2=moe_w2, ln3_g=ln3_g,
             ln3_b=ln3_b)
    return _forward(x, mem, p)
```

```python
import functools

import jax
import jax.numpy as jnp
import numpy as np
from jax import lax
from jax.experimental import pallas as pl
from jax.experimental.pallas import tpu as pltpu

F32 = jnp.float32
BF16 = jnp.bfloat16

D_MODEL = 1024
DEPTH = 4
SSD_WIDTH = 512
SSD_HEAD_DIM = 64
SSD_HEADS = 8
SSD_GROUPS = 2
SSD_STATE = 128
SSD_CONV = 4
SSD_CONV_CH = 1024
FOX_WIDTH = 256
FOX_HEAD_DIM = 64
FOX_HEADS = 4
POOL_WIDTH = 256
POOL_WINDOWS = (2, 4, 8, 16)
POOL_GROUP_DIM = 64
IN_SIZES = (512, 1024, 8, 256, 256, 256, 4, 256)
XATTN_HEADS = 4
XATTN_HEAD_DIM = 256
D_FF = 3584
N_EXPERTS = 8
TOP_K = 2
DN_ALPHA = (2 * DEPTH) ** 0.25
LN_EPS = 1e-5
RMS_EPS = 1e-5

LANES = 128
MIB = 1024 * 1024

IN_TM = 512
SSD_TB = 512
SSD_L = 128
FOX_TQ = 2048
FOX_TK = 512
FOX_QW = 128
FOX_KSUB = 512
FOX_LOOKAHEAD = 4
POOL_TB = 512
POOL_PIECE = 128
OUT_TM = 1024
XA_TM = 1024
FFN_TM = 1024
FFN_TF = 512
MOE_TM = 1024
ROUTE_TM = 512
ROW_TM = 512


def _cparams(sem, vmem_mib=48):
    return pltpu.CompilerParams(dimension_semantics=sem, vmem_limit_bytes=vmem_mib * MIB)


def _dot(a, b):
    return jnp.dot(a, b, preferred_element_type=F32)


def _dot_nt(a, b):
    return lax.dot_general(a, b, (((1,), (1,)), ((), ())), preferred_element_type=F32)


def _dot_tn(a, b):
    return lax.dot_general(a, b, (((0,), (0,)), ((), ())), preferred_element_type=F32)


def _split3(x):
    hi = x.astype(BF16)
    r1 = x - hi.astype(F32)
    mid = r1.astype(BF16)
    lo = (r1 - mid.astype(F32)).astype(BF16)
    return hi, mid, lo


def _silu(x):
    return x / (1.0 + jnp.exp(-x))


def _softplus(x):
    return jnp.maximum(x, 0.0) + jnp.log1p(jnp.exp(-jnp.abs(x)))


def _log_sigmoid(x):
    return jnp.minimum(x, 0.0) - jnp.log1p(jnp.exp(-jnp.abs(x)))


def _layer_norm(v, g, b):
    mu = jnp.mean(v, axis=-1, keepdims=True)
    d = v - mu
    var = jnp.mean(d * d, axis=-1, keepdims=True)
    return d * lax.rsqrt(var + LN_EPS) * g + b


def _tri_ones(n, lower):
    r = lax.broadcasted_iota(jnp.int32, (n, n), 0)
    c = lax.broadcasted_iota(jnp.int32, (n, n), 1)
    m = (c <= r) if lower else (r <= c)
    return jnp.where(m, 1.0, 0.0).astype(BF16)


FOX_SLOT = LANES
FOX_EXT = FOX_HEADS * FOX_SLOT
FOX_BIAS_TERMS = 3
FOX_ONES_ROWS = 16
LOG2E = 1.4426950408889634


def _in_proj_kernel(x_ref, wm_ref, wkx_ref, wqt_ref, wvt_ref, ws_ref, wst_ref, fbc_ref, place_ref, qones_ref,
                    vones_ref, xbc_ref, z_ref, pool_ref, dte_ref, kx_ref, qt_ref, vt_ref, rows_ref,
                    carry_c, *, tiles_per_batch):
    i = pl.program_id(0)
    tm = x_ref.shape[0]
    xb = x_ref[...].astype(BF16)

    xbc_ref[...] = _dot(xb, wm_ref[:, 0:1024])
    z_ref[...] = _dot(xb, wm_ref[:, 1024:1536])
    pool_ref[...] = _dot(xb, wm_ref[:, 1536:1792])
    dte_ref[...] = _dot(xb, wm_ref[:, 1792:2304])
    qt_ref[...] = (_dot_nt(wqt_ref[...], xb) + qones_ref[...]).astype(BF16)
    vt_ref[...] = (_dot_nt(wvt_ref[...], xb) + vones_ref[...]).astype(BF16)
    rows_ref[...] = _dot_nt(wst_ref[...], xb)

    @pl.when(i % tiles_per_batch == 0)
    def _():
        carry_c[...] = jnp.zeros_like(carry_c)

    small_c = _dot(xb, ws_ref[...])
    c3 = _split3(_log_sigmoid(small_c + fbc_ref[...]))
    tri = _tri_ones(tm, lower=True)
    cs_c = _dot(tri, c3[0]) + _dot(tri, c3[1]) + _dot(tri, c3[2]) + carry_c[...]
    carry_c[...] = cs_c[tm - 1:tm, :]
    n3 = _split3(cs_c * (-LOG2E))
    kx = _dot(xb, wkx_ref[...])
    for term in range(FOX_BIAS_TERMS):
        kx = kx + _dot(n3[term], place_ref[term])
    kx_ref[...] = kx.astype(BF16)


def _in_proj(x2, wm, wkx, wqt, wvt, ws, wst, fbc, place, qones, vones, *, seq_len, tm):
    n = x2.shape[0]
    grid = (n // tm,)
    full = lambda a: pl.BlockSpec(a.shape, lambda i: (0,) * a.ndim)
    rowblk = lambda w: pl.BlockSpec((tm, w), lambda i: (i, 0))
    colblk = lambda h: pl.BlockSpec((h, tm), lambda i: (0, i))
    out_shape = (
        jax.ShapeDtypeStruct((n, 1024), F32),
        jax.ShapeDtypeStruct((n, 512), F32),
        jax.ShapeDtypeStruct((n, 256), F32),
        jax.ShapeDtypeStruct((n, 512), F32),
        jax.ShapeDtypeStruct((n, FOX_EXT), BF16),
        jax.ShapeDtypeStruct((FOX_EXT, n), BF16),
        jax.ShapeDtypeStruct((FOX_EXT, n), BF16),
        jax.ShapeDtypeStruct((16, n), F32),
    )
    out_specs = (rowblk(1024), rowblk(512), rowblk(256), rowblk(512), rowblk(FOX_EXT),
                 colblk(FOX_EXT), colblk(FOX_EXT), colblk(16))
    args = (x2, wm, wkx, wqt, wvt, ws, wst, fbc, place, qones, vones)
    return pl.pallas_call(
        functools.partial(_in_proj_kernel, tiles_per_batch=seq_len // tm),
        out_shape=out_shape,
        grid=grid,
        in_specs=[rowblk(D_MODEL)] + [full(a) for a in args[1:]],
        out_specs=out_specs,
        scratch_shapes=[pltpu.VMEM((1, LANES), F32)],
        compiler_params=_cparams(("arbitrary",)),
        name="in_proj",
    )(*args)


def _ssd_kernel(xbc_ref, z_ref, dte_ref, rows_ref, cw_ref, cb_ref, dtb_ref, alog_ref, dsk_ref, nw_ref,
                dtbr_ref, alogr_ref, y_ref, tail, ubuf, xc, state, *, chunk):
    t = pl.program_id(1)
    tb = xbc_ref.shape[0]
    L = chunk
    CONV_PIECE = 64
    HALO = 8

    @pl.when(t == 0)
    def _():
        tail[...] = jnp.zeros_like(tail)
        state[...] = jnp.zeros_like(state)

    ubuf[0:HALO, :] = tail[...]
    ubuf[HALO:HALO + tb, :] = xbc_ref[...]
    tail[...] = xbc_ref[tb - HALO:tb, :]
    for r0 in range(0, tb, CONV_PIECE):
        window = ubuf[r0:r0 + HALO + CONV_PIECE, :]
        acc = jnp.broadcast_to(cb_ref[...], (CONV_PIECE, SSD_CONV_CH))
        for k in range(SSD_CONV):
            delay = SSD_CONV - 1 - k
            tap = window if delay == 0 else pltpu.roll(window, shift=delay, axis=0)
            acc = acc + cw_ref[k:k + 1, :] * tap[HALO:HALO + CONV_PIECE, :]
        xc[r0:r0 + CONV_PIECE, :] = _silu(acc)

    a_e = -jnp.exp(alog_ref[...])
    a_r = -jnp.exp(alogr_ref[...])
    tri = _tri_ones(L, lower=True)
    upp = _tri_ones(L, lower=False)
    rr = lax.broadcasted_iota(jnp.int32, (L, L), 0)
    cc = lax.broadcasted_iota(jnp.int32, (L, L), 1)
    causal = cc <= rr
    lane = lax.broadcasted_iota(jnp.int32, (L, LANES), 1)
    left = lane < SSD_HEAD_DIM
    HG = SSD_HEADS // SSD_GROUPS
    GW = SSD_WIDTH // SSD_GROUPS

    def body(c, carry):
        r0 = pl.multiple_of(c * L, L)
        rows = pl.ds(r0, L)
        xs = xc[rows, 0:SSD_WIDTH]
        dt_e = _softplus(dte_ref[rows, :] + dtb_ref[...])
        a3 = _split3(dt_e * a_e)
        acs_e = _dot(tri, a3[0]) + _dot(tri, a3[1]) + _dot(tri, a3[2])
        dt_r = _softplus(rows_ref[:, rows] + dtbr_ref[...])
        ar3 = _split3(dt_r * a_r)
        acs_r = _dot(ar3[0], upp) + _dot(ar3[1], upp) + _dot(ar3[2], upp)
        total = acs_e[L - 1:L, :]
        x_dt = xs * dt_e
        xb = x_dt.astype(BF16)
        xd = (x_dt * jnp.exp(total - acs_e)).astype(BF16)
        eacs = jnp.exp(acs_e)
        y_parts = []
        for g in range(SSD_GROUPS):
            bm = xc[rows, SSD_WIDTH + g * SSD_STATE:SSD_WIDTH + (g + 1) * SSD_STATE].astype(BF16)
            cm = xc[rows, SSD_WIDTH + (SSD_GROUPS + g) * SSD_STATE:
                    SSD_WIDTH + (SSD_GROUPS + g + 1) * SSD_STATE].astype(BF16)
            cb = _dot_nt(cm, bm)
            for pair in range(HG // 2):
                lo = g * GW + pair * LANES
                x_pair = xb[:, lo:lo + LANES]
                outs = []
                for sub in range(2):
                    h = g * HG + pair * 2 + sub
                    seg = acs_e[:, h * SSD_HEAD_DIM:h * SSD_HEAD_DIM + 1] - acs_r[h:h + 1, :]
                    decay = jnp.exp(jnp.where(causal, seg, -jnp.inf))
                    outs.append(_dot((cb * decay).astype(BF16), x_pair))
                y_parts.append(jnp.where(left, outs[0], outs[1]))
            s_prev = state[g]
            y_off = _dot(cm, s_prev.astype(BF16)) * eacs[:, g * GW:(g + 1) * GW]
            y_parts[-2] = y_parts[-2] + y_off[:, 0:LANES]
            y_parts[-1] = y_parts[-1] + y_off[:, LANES:2 * LANES]
            state[g] = s_prev * jnp.exp(total[:, g * GW:(g + 1) * GW]) + _dot_tn(bm, xd[:, g * GW:(g + 1) * GW])
        y = jnp.concatenate(y_parts, axis=1) + xs * dsk_ref[...]
        y = y * _silu(z_ref[rows, :])
        outs = []
        for g in range(SSD_GROUPS):
            yg = y[:, g * GW:(g + 1) * GW]
            ms = jnp.mean(yg * yg, axis=-1, keepdims=True)
            outs.append(yg * lax.rsqrt(ms + RMS_EPS))
        y_ref[rows, :] = (jnp.concatenate(outs, axis=1) * nw_ref[...]).astype(y_ref.dtype)
        return carry

    lax.fori_loop(0, tb // L, body, 0, unroll=True)


def _ssd(xbc, z, dte, rowsp, cw, cb, dtb_e, alog_e, dsk_e, nw, dtb_r, alog_r, *, batch, seq_len, tb, chunk):
    n = xbc.shape[0]
    tpb = seq_len // tb
    full = lambda a: pl.BlockSpec(a.shape, lambda b, t: (0,) * a.ndim)
    rowblk = lambda w: pl.BlockSpec((tb, w), lambda b, t: (b * tpb + t, 0))
    return pl.pallas_call(
        functools.partial(_ssd_kernel, chunk=chunk),
        out_shape=jax.ShapeDtypeStruct((n, SSD_WIDTH), BF16),
        grid=(batch, tpb),
        in_specs=[rowblk(SSD_CONV_CH), rowblk(SSD_WIDTH), rowblk(SSD_WIDTH),
                  pl.BlockSpec((16, tb), lambda b, t: (0, b * tpb + t)),
                  full(cw), full(cb), full(dtb_e), full(alog_e), full(dsk_e), full(nw),
                  full(dtb_r), full(alog_r)],
        out_specs=rowblk(SSD_WIDTH),
        scratch_shapes=[pltpu.VMEM((8, SSD_CONV_CH), F32),
                        pltpu.VMEM((tb + 8, SSD_CONV_CH), F32),
                        pltpu.VMEM((tb, SSD_CONV_CH), F32),
                        pltpu.VMEM((SSD_GROUPS, SSD_STATE, SSD_WIDTH // SSD_GROUPS), F32)],
        compiler_params=_cparams(("parallel", "arbitrary")),
        name="ssd_mixer",
    )(xbc, z, dte, rowsp, cw, cb, dtb_e, alog_e, dsk_e, nw, dtb_r, alog_r)


def _fox_kernel(qi_ref, kj_ref, kx_ref, qt_ref, vt_ref, o_ref, m_sc, acc_sc, *, ratio, qw):
    step_id = pl.program_id(1)
    i = qi_ref[step_id]
    j = kj_ref[step_id]
    tk = kx_ref.shape[0]
    tq = qt_ref.shape[1]
    vrows = FOX_HEAD_DIM + FOX_ONES_ROWS

    @pl.when(j == 0)
    def _():
        m_sc[...] = jnp.full_like(m_sc, -jnp.inf)
        acc_sc[...] = jnp.zeros_like(acc_sc)

    def step(diag):
        ksub = min(FOX_KSUB, tk)
        units = [(k0, h, c0) for k0 in range(0, tk, ksub) for h in range(FOX_HEADS) for c0 in range(0, tq, qw)]
        if diag is not None:
            units = [(k0, h, c0) for k0, h, c0 in units if c0 + qw > diag * tk + k0]

        def score(unit):
            k0, h, c0 = unit
            return _dot(kx_ref[k0:k0 + ksub, h * FOX_SLOT:(h + 1) * FOX_SLOT],
                        qt_ref[h * FOX_SLOT:(h + 1) * FOX_SLOT, c0:c0 + qw])

        scores = [score(u) for u in units[:FOX_LOOKAHEAD]]
        for n_unit, (k0, h, c0) in enumerate(units):
            if n_unit + FOX_LOOKAHEAD < len(units):
                scores.append(score(units[n_unit + FOX_LOOKAHEAD]))
            st = scores[n_unit]
            lo = h * FOX_SLOT
            if diag is not None and c0 < diag * tk + k0 + ksub - 1:
                key = diag * tk + k0 + lax.broadcasted_iota(jnp.int32, (ksub, qw), 0)
                qry = c0 + lax.broadcasted_iota(jnp.int32, (ksub, qw), 1)
                st = jnp.where(key <= qry, st, -jnp.inf)
            m_prev = m_sc[h, :, c0:c0 + qw]
            m_new = jnp.maximum(m_prev, jnp.max(st, axis=0, keepdims=True))
            alpha = jnp.exp2(m_prev - m_new)
            p = jnp.exp2(st - m_new).astype(BF16)
            acc_sc[h, :, c0:c0 + qw] = (alpha * acc_sc[h, :, c0:c0 + qw]
                                        + _dot(vt_ref[lo:lo + vrows, k0:k0 + ksub], p))
            m_sc[h, :, c0:c0 + qw] = m_new

    @pl.when(j < i * ratio)
    def _():
        step(None)

    for r in range(ratio):
        @pl.when(j == i * ratio + r)
        def _():
            step(r)

    @pl.when(j == (i + 1) * ratio - 1)
    def _():
        parts = [acc_sc[h, 0:FOX_HEAD_DIM, :] / acc_sc[h, FOX_HEAD_DIM:FOX_HEAD_DIM + 1, :]
                 for h in range(FOX_HEADS)]
        o_ref[...] = jnp.concatenate(parts, axis=0).T.astype(o_ref.dtype)


def _fox(kx, qt, vt, *, batch, seq_len, tq, tk):
    n = kx.shape[0]
    assert tq % tk == 0
    ratio = tq // tk
    nq = seq_len // tq
    nk = seq_len // tk
    pairs = [(i, j) for i in range(nq) for j in range((i + 1) * ratio)]
    qi = jnp.asarray([pq for pq, _ in pairs], jnp.int32)
    kj = jnp.asarray([pk for _, pk in pairs], jnp.int32)
    return pl.pallas_call(
        functools.partial(_fox_kernel, ratio=ratio, qw=min(FOX_QW, tq)),
        out_shape=jax.ShapeDtypeStruct((n, FOX_WIDTH), BF16),
        grid_spec=pltpu.PrefetchScalarGridSpec(
            num_scalar_prefetch=2,
            grid=(batch, len(pairs)),
            in_specs=[pl.BlockSpec((tk, FOX_EXT), lambda b, s, qi, kj: (b * nk + kj[s], 0)),
                      pl.BlockSpec((FOX_EXT, tq), lambda b, s, qi, kj: (0, b * nq + qi[s])),
                      pl.BlockSpec((FOX_EXT, tk), lambda b, s, qi, kj: (0, b * nk + kj[s]))],
            out_specs=pl.BlockSpec((tq, FOX_WIDTH), lambda b, s, qi, kj: (b * nq + qi[s], 0)),
            scratch_shapes=[pltpu.VMEM((FOX_HEADS, 1, tq), F32),
                            pltpu.VMEM((FOX_HEADS, FOX_HEAD_DIM + FOX_ONES_ROWS, tq), F32)]),
        compiler_params=_cparams(("parallel", "arbitrary")),
        name="fox_attention",
    )(qi, kj, kx, qt, vt)


POOL_HALO = 16


def _pool_kernel(u_ref, w_ref, b_ref, sc_ref, y_ref, tail, ubuf):
    t = pl.program_id(1)
    tb = u_ref.shape[0]
    P = POOL_PIECE

    @pl.when(t == 0)
    def _():
        tail[...] = jnp.zeros_like(tail)

    ubuf[0:POOL_HALO, :] = tail[...]
    ubuf[POOL_HALO:POOL_HALO + tb, :] = u_ref[...]
    tail[...] = u_ref[tb - POOL_HALO:tb, :]
    group = lax.broadcasted_iota(jnp.int32, (P, POOL_WIDTH), 1) // POOL_GROUP_DIM
    for p0 in range(0, tb, P):
        w1 = ubuf[p0:p0 + P + POOL_HALO, :]
        n1 = P + POOL_HALO
        a2 = w1[1:n1] + w1[0:n1 - 1]
        a4 = a2[2:n1 - 1] + a2[0:n1 - 3]
        a8 = a4[4:n1 - 3] + a4[0:n1 - 7]
        a16 = a8[8:n1 - 7] + a8[0:n1 - 15]
        sums = (a2[15:15 + P], a4[13:13 + P], a8[9:9 + P], a16[1:1 + P])
        tpos = t * tb + p0 + lax.broadcasted_iota(jnp.int32, (P, 1), 0) + 1
        pooled = jnp.zeros((P, POOL_WIDTH), F32)
        for g, win in enumerate(POOL_WINDOWS):
            cnt = jnp.minimum(tpos, win).astype(F32)
            pooled = jnp.where(group == g, sums[g] / cnt, pooled)
        pooled = pooled - w1[POOL_HALO:POOL_HALO + P]
        y = _dot(pooled.astype(BF16), w_ref[...]) + b_ref[...]
        y_ref[p0:p0 + P, :] = (y * sc_ref[...]).astype(y_ref.dtype)


def _pool(u, wbd, b, sc, *, batch, seq_len, tb):
    n = u.shape[0]
    tpb = seq_len // tb
    full = lambda a: pl.BlockSpec(a.shape, lambda bb, t: (0,) * a.ndim)
    rowblk = pl.BlockSpec((tb, POOL_WIDTH), lambda bb, t: (bb * tpb + t, 0))
    return pl.pallas_call(
        _pool_kernel,
        out_shape=jax.ShapeDtypeStruct((n, POOL_WIDTH), BF16),
        grid=(batch, tpb),
        in_specs=[rowblk, full(wbd), full(b), full(sc)],
        out_specs=rowblk,
        scratch_shapes=[pltpu.VMEM((POOL_HALO, POOL_WIDTH), F32),
                        pltpu.VMEM((tb + POOL_HALO, POOL_WIDTH), F32)],
        compiler_params=_cparams(("parallel", "arbitrary")),
        name="pool_mixer",
    )(u, wbd, b, sc)


def _out_proj_kernel(x_ref, ys_ref, yf_ref, yp_ref, w_ref, g_ref, b_ref, o_ref):
    mix = (_dot(ys_ref[...], w_ref[0:512, :]) + _dot(yf_ref[...], w_ref[512:768, :])
           + _dot(yp_ref[...], w_ref[768:1024, :]))
    o_ref[...] = _layer_norm(DN_ALPHA * x_ref[...] + mix, g_ref[...], b_ref[...])


def _out_proj(x2, ys, yf, yp, w, g, b, *, tm):
    n = x2.shape[0]
    full = lambda a: pl.BlockSpec(a.shape, lambda i: (0,) * a.ndim)
    rowblk = lambda wd: pl.BlockSpec((tm, wd), lambda i: (i, 0))
    return pl.pallas_call(
        _out_proj_kernel,
        out_shape=jax.ShapeDtypeStruct((n, D_MODEL), F32),
        grid=(n // tm,),
        in_specs=[rowblk(D_MODEL), rowblk(512), rowblk(256), rowblk(256), full(w), full(g), full(b)],
        out_specs=rowblk(D_MODEL),
        compiler_params=_cparams(("parallel",)),
        name="out_proj_ln",
    )(x2, ys, yf, yp, w, g, b)


def _kv_proj_kernel(m_ref, w_ref, o_ref):
    o_ref[...] = _dot(m_ref[...].astype(BF16), w_ref[...]).astype(o_ref.dtype)


def _kv_proj(mem2, wkv):
    m = mem2.shape[0]
    return pl.pallas_call(
        _kv_proj_kernel,
        out_shape=jax.ShapeDtypeStruct((m, wkv.shape[1]), BF16),
        grid=(1,),
        in_specs=[pl.BlockSpec(mem2.shape, lambda i: (0, 0)), pl.BlockSpec(wkv.shape, lambda i: (0, 0))],
        out_specs=pl.BlockSpec((m, wkv.shape[1]), lambda i: (0, 0)),
        compiler_params=_cparams(("arbitrary",)),
        name="xattn_kv_proj",
    )(mem2, wkv)


def _xattn_kernel(x_ref, kv_ref, wq_ref, wo_ref, g_ref, b_ref, o_ref):
    x = x_ref[...]
    q = _dot(x.astype(BF16), wq_ref[...]).astype(BF16)
    heads = []
    for h in range(XATTN_HEADS):
        lo = h * XATTN_HEAD_DIM
        k_h = kv_ref[:, lo:lo + XATTN_HEAD_DIM]
        v_h = kv_ref[:, D_MODEL + lo:D_MODEL + lo + XATTN_HEAD_DIM]
        s = _dot_nt(q[:, lo:lo + XATTN_HEAD_DIM], k_h)
        p = jnp.exp(s - jnp.max(s, axis=-1, keepdims=True))
        l = jnp.sum(p, axis=-1, keepdims=True)
        heads.append((_dot(p.astype(BF16), v_h) / l).astype(BF16))
    o = jnp.concatenate(heads, axis=1)
    xa = _dot(o, wo_ref[...])
    o_ref[...] = _layer_norm(DN_ALPHA * x + xa, g_ref[...], b_ref[...])


def _xattn(x2, kv, wq, wo, g, b, *, batch, seq_len, mem_len, tm):
    n = x2.shape[0]
    tpb = seq_len // tm
    full = lambda a: pl.BlockSpec(a.shape, lambda bb, t: (0,) * a.ndim)
    rowblk = pl.BlockSpec((tm, D_MODEL), lambda bb, t: (bb * tpb + t, 0))
    return pl.pallas_call(
        _xattn_kernel,
        out_shape=jax.ShapeDtypeStruct((n, D_MODEL), F32),
        grid=(batch, tpb),
        in_specs=[rowblk, pl.BlockSpec((mem_len, 2 * D_MODEL), lambda bb, t: (bb, 0)),
                  full(wq), full(wo), full(g), full(b)],
        out_specs=rowblk,
        compiler_params=_cparams(("parallel", "parallel")),
        name="xattn_ln",
    )(x2, kv, wq, wo, g, b)


ROW_SLABS = D_MODEL // LANES


def _ffn_kernel(texp_ref, nvalid_ref, x_ref, w1_ref, w3_ref, w2_ref, g_ref, b_ref, o_ref, xb_sc, acc_sc,
                *, grouped):
    t = pl.program_id(0)
    f = pl.program_id(1)
    nf = pl.num_programs(1)

    @pl.when(t < nvalid_ref[0])
    def _():
        @pl.when(f == 0)
        def _():
            xb_sc[...] = x_ref[...].reshape(xb_sc.shape).astype(BF16)
            acc_sc[...] = jnp.zeros_like(acc_sc)

        xb = xb_sc[...]
        h1 = _dot(xb, w1_ref[0].astype(BF16))
        h3 = _dot(xb, w3_ref[0].astype(BF16))
        acc_sc[...] += _dot((_silu(h1) * h3).astype(BF16), w2_ref[0].astype(BF16))

        @pl.when(f == nf - 1)
        def _():
            if grouped:
                o_ref[...] = acc_sc[...].reshape(o_ref.shape)
            else:
                o_ref[...] = _layer_norm(DN_ALPHA * x_ref[...] + acc_sc[...], g_ref[...], b_ref[...])

    @pl.when((t >= nvalid_ref[0]) & (f == 0))
    def _():
        o_ref[...] = jnp.zeros_like(o_ref)


def _ffn(tile_expert, nvalid, x, w1, w3, w2, g, b, *, tm, tf, grouped):
    n = x.shape[0]
    nt = n // tm
    nf = D_FF // tf

    def tile_of(t, nv):
        return jnp.minimum(t, nv[0] - 1)

    def f_of(t, f, nv):
        return jnp.where(t < nv[0], f, nf - 1)

    if grouped:
        rowblk = pl.BlockSpec((tm, ROW_SLABS, LANES), lambda t, f, te, nv: (tile_of(t, nv), 0, 0))
        outblk = pl.BlockSpec((tm, ROW_SLABS, LANES), lambda t, f, te, nv: (t, 0, 0))
        out_shape = jax.ShapeDtypeStruct((n, ROW_SLABS, LANES), F32)
    else:
        rowblk = pl.BlockSpec((tm, D_MODEL), lambda t, f, te, nv: (tile_of(t, nv), 0))
        outblk = pl.BlockSpec((tm, D_MODEL), lambda t, f, te, nv: (t, 0))
        out_shape = jax.ShapeDtypeStruct((n, D_MODEL), F32)
    w13 = pl.BlockSpec((1, D_MODEL, tf), lambda t, f, te, nv: (te[tile_of(t, nv)], 0, f_of(t, f, nv)))
    w2s = pl.BlockSpec((1, tf, D_MODEL), lambda t, f, te, nv: (te[tile_of(t, nv)], f_of(t, f, nv), 0))
    vec = pl.BlockSpec((1, D_MODEL), lambda t, f, te, nv: (0, 0))
    return pl.pallas_call(
        functools.partial(_ffn_kernel, grouped=grouped),
        out_shape=out_shape,
        grid_spec=pltpu.PrefetchScalarGridSpec(
            num_scalar_prefetch=2,
            grid=(nt, nf),
            in_specs=[rowblk, w13, w13, w2s, vec, vec],
            out_specs=outblk,
            scratch_shapes=[pltpu.VMEM((tm, D_MODEL), BF16), pltpu.VMEM((tm, D_MODEL), F32)]),
        compiler_params=_cparams(("arbitrary", "arbitrary"), vmem_mib=56),
        name="swiglu_grouped" if grouped else "swiglu_ln",
    )(tile_expert, nvalid, x, w1, w3, w2, g, b)


def _router_kernel(x_ref, w_ref, route_ref, counts_ref, carry):
    i = pl.program_id(0)
    tm = x_ref.shape[0]

    @pl.when(i == 0)
    def _():
        carry[...] = jnp.zeros_like(carry)

    xh, xm, _ = _split3(x_ref[...])
    wh, wm, _ = _split3(w_ref[...])
    logits = _dot(xh, wh) + (_dot(xh, wm) + _dot(xm, wh))
    lane = lax.broadcasted_iota(jnp.int32, (tm, LANES), 1)
    logits = jnp.where(lane < N_EXPERTS, logits, -jnp.inf)
    m1 = jnp.max(logits, axis=-1, keepdims=True)
    i1 = jnp.min(jnp.where(logits == m1, lane, LANES), axis=-1, keepdims=True)
    rest = jnp.where(lane == i1, -jnp.inf, logits)
    m2 = jnp.max(rest, axis=-1, keepdims=True)
    i2 = jnp.min(jnp.where(rest == m2, lane, LANES), axis=-1, keepdims=True)
    e21 = jnp.exp(m2 - m1)
    g1 = 1.0 / (1.0 + e21)
    g2 = e21 / (1.0 + e21)
    hit1 = lane == i1
    hit2 = lane == i2
    onehot = jnp.where(hit1 | hit2, 1.0, 0.0).astype(BF16)
    r = lax.broadcasted_iota(jnp.int32, (tm, tm), 0)
    c = lax.broadcasted_iota(jnp.int32, (tm, tm), 1)
    strict = jnp.where(c < r, 1.0, 0.0).astype(BF16)
    before = _dot(strict, onehot) + carry[...]
    rank1 = jnp.sum(jnp.where(hit1, before, 0.0), axis=-1, keepdims=True)
    rank2 = jnp.sum(jnp.where(hit2, before, 0.0), axis=-1, keepdims=True)
    carry[...] = carry[...] + jnp.sum(onehot.astype(F32), axis=0, keepdims=True)
    out = jnp.where(lane == 0, i1.astype(F32), 0.0)
    out = jnp.where(lane == 1, i2.astype(F32), out)
    out = jnp.where(lane == 2, g1, out)
    out = jnp.where(lane == 3, g2, out)
    out = jnp.where(lane == 4, rank1, out)
    out = jnp.where(lane == 5, rank2, out)
    route_ref[...] = out
    counts_ref[...] = carry[...]


def _router(x2, wr, *, tm):
    n = x2.shape[0]
    return pl.pallas_call(
        _router_kernel,
        out_shape=(jax.ShapeDtypeStruct((n, LANES), F32), jax.ShapeDtypeStruct((1, LANES), F32)),
        grid=(n // tm,),
        in_specs=[pl.BlockSpec((tm, D_MODEL), lambda i: (i, 0)), pl.BlockSpec(wr.shape, lambda i: (0, 0))],
        out_specs=(pl.BlockSpec((tm, LANES), lambda i: (i, 0)), pl.BlockSpec((1, LANES), lambda i: (0, 0))),
        scratch_shapes=[pltpu.VMEM((1, LANES), F32)],
        compiler_params=_cparams(("arbitrary",)),
        name="moe_router",
    )(x2, wr)


POS_ROWS = 8


ROW_UNROLL = 8


def _for_each_row_copy(row_copy, action, pos_rows=(0, POS_ROWS)):
    tokens_per_pos_row = LANES // TOP_K
    for prow in range(*pos_rows):
        def body(c, carry):
            for k in range(TOP_K):
                copy = row_copy(prow * tokens_per_pos_row + c, (prow, TOP_K * c + k), k)
                getattr(copy, action)()
            return carry

        lax.fori_loop(0, tokens_per_pos_row, body, 0, unroll=ROW_UNROLL)


def _dispatch_kernel(pend_ref, padded_ref, pos_hbm, x_ref, xs_hbm, pos_smem, stage, zeros_vmem, sem_pos, sem_rows,
                     *, row_tm, pad_tm):
    i = pl.program_id(0)
    last = pl.num_programs(0) - 1

    def zero_copy(start):
        return pltpu.make_async_copy(zeros_vmem, xs_hbm.at[pl.ds(start, pad_tm)], sem_rows.at[0])

    @pl.when(i == 0)
    def _():
        zeros_vmem[...] = jnp.zeros_like(zeros_vmem)
        used = pend_ref[N_EXPERTS - 1]
        fills = [(padded_ref[e] > 0, pend_ref[e] - pad_tm) for e in range(N_EXPERTS)]
        fills += [(used + e * pad_tm < xs_hbm.shape[0], used + e * pad_tm) for e in range(N_EXPERTS)]
        for cond, start in fills:
            @pl.when(cond)
            def _():
                zero_copy(start).start()
        for cond, start in fills:
            @pl.when(cond)
            def _():
                zero_copy(start).wait()

    def run(slot):
        def row_copies(s):
            def row_copy(r, entry, k):
                return pltpu.make_async_copy(stage.at[s, r], xs_hbm.at[pos_smem[(s,) + entry]], sem_rows.at[s])
            return row_copy

        pos_copy = pltpu.make_async_copy(pos_hbm.at[i], pos_smem.at[slot], sem_pos)
        pos_copy.start()
        stage[slot] = x_ref[...].reshape(stage.shape[1:])
        pos_copy.wait()
        _for_each_row_copy(row_copies(slot), "start")

        @pl.when(i > 0)
        def _():
            _for_each_row_copy(row_copies(1 - slot), "wait")

        @pl.when(i == last)
        def _():
            _for_each_row_copy(row_copies(slot), "wait")

    for slot in range(2):
        @pl.when(i % 2 == slot)
        def _():
            run(slot)


def _dispatch(pend, padded, pos3, x2, *, cap, row_tm, pad_tm):
    n = x2.shape[0]
    assert TOP_K * row_tm == POS_ROWS * LANES
    return pl.pallas_call(
        functools.partial(_dispatch_kernel, row_tm=row_tm, pad_tm=pad_tm),
        out_shape=jax.ShapeDtypeStruct((cap, ROW_SLABS, LANES), F32),
        grid_spec=pltpu.PrefetchScalarGridSpec(
            num_scalar_prefetch=2,
            grid=(n // row_tm,),
            in_specs=[pl.BlockSpec(memory_space=pl.ANY),
                      pl.BlockSpec((row_tm, D_MODEL), lambda i, pe, pa: (i, 0))],
            out_specs=pl.BlockSpec(memory_space=pl.ANY),
            scratch_shapes=[pltpu.SMEM((2, POS_ROWS, LANES), jnp.int32),
                            pltpu.VMEM((2, row_tm, ROW_SLABS, LANES), F32),
                            pltpu.VMEM((pad_tm, ROW_SLABS, LANES), F32),
                            pltpu.SemaphoreType.DMA(()),
                            pltpu.SemaphoreType.DMA((2,))]),
        compiler_params=_cparams(("arbitrary",)),
        name="moe_dispatch",
    )(pend, padded, pos3, x2)


def _combine_kernel(pos_hbm, x_ref, route_ref, ys_hbm, g_ref, b_ref, o_ref, pos_smem, rows0, rows1,
                    sem_pos, sem_rows, *, row_tm):
    i = pl.program_id(0)
    pos_copy = pltpu.make_async_copy(pos_hbm.at[i], pos_smem, sem_pos)
    pos_copy.start()
    pos_copy.wait()
    rows = (rows0, rows1)

    halves = ((0, POS_ROWS // 2), (POS_ROWS // 2, POS_ROWS))
    half_tm = row_tm // 2

    def row_copies(half):
        def row_copy(r, entry, k):
            return pltpu.make_async_copy(ys_hbm.at[pos_smem[entry]], rows[k].at[r], sem_rows.at[half])
        return row_copy

    for half, pos_rows in enumerate(halves):
        _for_each_row_copy(row_copies(half), "start", pos_rows)
    for half, pos_rows in enumerate(halves):
        _for_each_row_copy(row_copies(half), "wait", pos_rows)
        tok = pl.ds(half * half_tm, half_tm)
        flat = (half_tm, D_MODEL)
        ff = (rows0[tok].reshape(flat) * route_ref[tok, 2:3] + rows1[tok].reshape(flat) * route_ref[tok, 3:4])
        o_ref[tok, :] = _layer_norm(DN_ALPHA * x_ref[tok, :] + ff, g_ref[...], b_ref[...])


def _combine(pos3, x2, route, ys, g, b, *, row_tm):
    n = x2.shape[0]
    assert TOP_K * row_tm == POS_ROWS * LANES
    rowblk = lambda w: pl.BlockSpec((row_tm, w), lambda i: (i, 0))
    vec = pl.BlockSpec((1, D_MODEL), lambda i: (0, 0))
    return pl.pallas_call(
        functools.partial(_combine_kernel, row_tm=row_tm),
        out_shape=jax.ShapeDtypeStruct((n, D_MODEL), F32),
        grid=(n // row_tm,),
        in_specs=[pl.BlockSpec(memory_space=pl.ANY), rowblk(D_MODEL), rowblk(LANES),
                  pl.BlockSpec(memory_space=pl.ANY), vec, vec],
        out_specs=rowblk(D_MODEL),
        scratch_shapes=[pltpu.SMEM((POS_ROWS, LANES), jnp.int32),
                        pltpu.VMEM((row_tm, ROW_SLABS, LANES), F32),
                        pltpu.VMEM((row_tm, ROW_SLABS, LANES), F32),
                        pltpu.SemaphoreType.DMA(()),
                        pltpu.SemaphoreType.DMA((2,))],
        compiler_params=_cparams(("arbitrary",)),
        name="moe_combine_ln",
    )(pos3, x2, route, ys, g, b)


def _fox_constants():
    place = np.zeros((FOX_BIAS_TERMS, LANES, FOX_EXT), np.float32)
    qones = np.zeros((FOX_EXT, 1), np.float32)
    vones = np.zeros((FOX_EXT, 1), np.float32)
    for h in range(FOX_HEADS):
        base = h * FOX_SLOT + FOX_HEAD_DIM
        for term in range(FOX_BIAS_TERMS):
            place[term, SSD_HEADS + h, base + term] = 1.0
        qones[base:base + FOX_BIAS_TERMS] = 1.0
        vones[base:base + FOX_ONES_ROWS] = 1.0
    return jnp.asarray(place, BF16), jnp.asarray(qones), jnp.asarray(vones)


def _prepare(p):
    depth = p["w_in"].shape[0]
    offs = np.concatenate([[0], np.cumsum(IN_SIZES)])
    w_in = p["w_in"].astype(F32)
    wz, wxbc, wdt, wq, wk, wv, wf, wp = (w_in[:, :, offs[k]:offs[k + 1]] for k in range(len(IN_SIZES)))

    def slots(w):
        w = w.reshape(depth, D_MODEL, FOX_HEADS, FOX_HEAD_DIM)
        w = jnp.pad(w, ((0, 0), (0, 0), (0, 0), (0, FOX_SLOT - FOX_HEAD_DIM)))
        return w.reshape(depth, D_MODEL, FOX_EXT)

    def vec(v, width):
        return v.astype(F32).reshape(v.shape[0], 1, width)

    def per_head(v):
        return jnp.repeat(v.astype(F32), SSD_HEAD_DIM, axis=1).reshape(depth, 1, SSD_WIDTH)

    def head_rows(v):
        return jnp.pad(v.astype(F32), ((0, 0), (0, 16 - SSD_HEADS))).reshape(depth, 16, 1)

    ws = jnp.concatenate([wdt, wf, jnp.zeros((depth, D_MODEL, LANES - SSD_HEADS - FOX_HEADS), F32)], axis=2)
    eye = jnp.asarray(np.eye(len(POOL_WINDOWS), dtype=np.float32))
    q = dict(
        wm=jnp.concatenate([wxbc, wz, wp, jnp.repeat(wdt, SSD_HEAD_DIM, axis=2)], axis=2).astype(BF16),
        wkx=slots(wk).astype(BF16),
        wqt=jnp.swapaxes(slots(wq * (FOX_HEAD_DIM ** -0.5 * LOG2E)), 1, 2).astype(BF16),
        wvt=jnp.swapaxes(slots(wv), 1, 2).astype(BF16),
        ws=ws.astype(BF16),
        wst=jnp.swapaxes(ws[:, :, 0:16], 1, 2).astype(BF16),
        fbc=jnp.pad(p["fox_f_bias"].astype(F32), ((0, 0), (SSD_HEADS, LANES - SSD_HEADS - FOX_HEADS))
                    ).reshape(depth, 1, LANES),
        conv_w=p["ssm_conv_w"].astype(F32), conv_b=vec(p["ssm_conv_b"], SSD_CONV_CH),
        dtb_e=per_head(p["ssm_dt_bias"]), alog_e=per_head(p["ssm_a_log"]), dsk_e=per_head(p["ssm_d"]),
        norm_w=vec(p["ssm_norm_w"], SSD_WIDTH),
        dtb_r=head_rows(p["ssm_dt_bias"]), alog_r=head_rows(p["ssm_a_log"]),
        pool_wbd=jnp.einsum("lgij,gh->lgihj", p["pool_w"].astype(F32), eye
                            ).reshape(depth, POOL_WIDTH, POOL_WIDTH).astype(BF16),
        pool_b=vec(p["pool_b"].reshape(depth, POOL_WIDTH), POOL_WIDTH), pool_sc=vec(p["pool_scale"], POOL_WIDTH),
        w_out=p["w_out"].astype(BF16),
        xa_wkv=jnp.concatenate([p["xa_wk"], p["xa_wv"]], axis=2).astype(BF16),
        xa_wq=(p["xa_wq"] * (XATTN_HEAD_DIM ** -0.5)).astype(BF16),
        xa_wo=p["xa_wo"].astype(BF16),
        ffn_w1=p["ffn_w1"].astype(BF16), ffn_w3=p["ffn_w3"].astype(BF16), ffn_w2=p["ffn_w2"].astype(BF16),
        moe_w1=p["moe_w1"].reshape(-1, D_MODEL, D_FF),
        moe_w3=p["moe_w3"].reshape(-1, D_MODEL, D_FF),
        moe_w2=p["moe_w2"].reshape(-1, D_FF, D_MODEL),
        router=jnp.pad(p["router_w"].astype(F32), ((0, 0), (0, 0), (0, LANES - N_EXPERTS))),
    )
    for name in ("ln1_g", "ln1_b", "ln2_g", "ln2_b", "ln3_g", "ln3_b"):
        q[name] = vec(p[name], D_MODEL)
    return q


def _token_mixing(x2, q, layer, *, batch, seq_len):
    place, qones, vones = _fox_constants()
    xbc, z, pool_in, dte, kx, qt, vt, rowsp = _in_proj(
        x2, q["wm"][layer], q["wkx"][layer], q["wqt"][layer], q["wvt"][layer], q["ws"][layer], q["wst"][layer],
        q["fbc"][layer], place, qones, vones, seq_len=seq_len, tm=min(IN_TM, seq_len))

    y_ssd = _ssd(xbc, z, dte, rowsp, q["conv_w"][layer], q["conv_b"][layer], q["dtb_e"][layer],
                 q["alog_e"][layer], q["dsk_e"][layer], q["norm_w"][layer], q["dtb_r"][layer], q["alog_r"][layer],
                 batch=batch, seq_len=seq_len, tb=min(SSD_TB, seq_len), chunk=SSD_L)

    y_fox = _fox(kx, qt, vt, batch=batch, seq_len=seq_len, tq=min(FOX_TQ, seq_len), tk=min(FOX_TK, seq_len))

    y_pool = _pool(pool_in, q["pool_wbd"][layer], q["pool_b"][layer], q["pool_sc"][layer],
                   batch=batch, seq_len=seq_len, tb=min(POOL_TB, seq_len))

    return _out_proj(x2, y_ssd, y_fox, y_pool, q["w_out"][layer], q["ln1_g"][layer], q["ln1_b"][layer],
                     tm=min(OUT_TM, seq_len))


def _cross_attention(x2, mem2, q, layer, *, batch, seq_len, mem_len):
    kv = _kv_proj(mem2, q["xa_wkv"][layer])
    return _xattn(x2, kv, q["xa_wq"][layer], q["xa_wo"][layer], q["ln2_g"][layer], q["ln2_b"][layer],
                  batch=batch, seq_len=seq_len, mem_len=mem_len, tm=min(XA_TM, seq_len))


def _dense_ffn(x2, q, layer):
    j = layer // 2
    n = x2.shape[0]
    tm = min(FFN_TM, n)
    tile_expert = jnp.full((n // tm,), j, jnp.int32)
    nvalid = jnp.full((1,), n // tm, jnp.int32)
    return _ffn(tile_expert, nvalid, x2, q["ffn_w1"], q["ffn_w3"], q["ffn_w2"],
                q["ln3_g"][layer], q["ln3_b"][layer], tm=tm, tf=FFN_TF, grouped=False)


def _moe_ffn(x2, q, layer):
    j = layer // 2
    n = x2.shape[0]
    tm = min(MOE_TM, n)
    row_tm = min(ROW_TM, n)
    route, counts = _router(x2, q["router"][j], tm=min(ROUTE_TM, n))

    counts = counts[0, :N_EXPERTS].astype(jnp.int32)
    padded = (counts + tm - 1) // tm * tm
    pend = jnp.cumsum(padded)
    pstart = pend - padded
    experts = route[:, 0:TOP_K].astype(jnp.int32)
    ranks = route[:, 4:4 + TOP_K].astype(jnp.int32)
    pos = (pstart[experts] + ranks).astype(jnp.int32)
    pos3 = pos.reshape(n // row_tm, POS_ROWS, LANES)
    cap = n * TOP_K + N_EXPERTS * tm
    ntiles = cap // tm
    tile_start = jnp.arange(ntiles, dtype=jnp.int32) * tm
    tile_expert = jnp.minimum(jnp.sum((pend[None, :] <= tile_start[:, None]).astype(jnp.int32), axis=1),
                              N_EXPERTS - 1).astype(jnp.int32)
    nvalid = (pend[-1:] // tm).astype(jnp.int32)

    xs = _dispatch(pend.astype(jnp.int32), padded.astype(jnp.int32), pos3, x2, cap=cap, row_tm=row_tm, pad_tm=tm)
    ys = _ffn(tile_expert + j * N_EXPERTS, nvalid, xs, q["moe_w1"], q["moe_w3"], q["moe_w2"],
              q["ln3_g"][layer], q["ln3_b"][layer], tm=tm, tf=FFN_TF, grouped=True)
    return _combine(pos3, x2, route, ys, q["ln3_g"][layer], q["ln3_b"][layer], row_tm=row_tm)


def _forward(x, mem, p):
    batch, seq_len, _ = x.shape
    mem_len = mem.shape[1]
    x2 = x.reshape(batch * seq_len, D_MODEL).astype(F32)
    mem2 = mem.reshape(batch * mem_len, D_MODEL).astype(F32)
    q = _prepare(p)
    for layer in range(DEPTH):
        x2 = _token_mixing(x2, q, layer, batch=batch, seq_len=seq_len)
        x2 = _cross_attention(x2, mem2, q, layer, batch=batch, seq_len=seq_len, mem_len=mem_len)
        x2 = _dense_ffn(x2, q, layer) if layer % 2 == 0 else _moe_ffn(x2, q, layer)
    return x2.reshape(batch, seq_len, D_MODEL)


def kernel(x, mem, w_in, ssm_conv_w, ssm_conv_b, ssm_dt_bias, ssm_a_log, ssm_d, ssm_norm_w, fox_f_bias, pool_w, pool_b, pool_scale, w_out, ln1_g, ln1_b, xa_wq, xa_wk, xa_wv, xa_wo, ln2_g, ln2_b, ffn_w1, ffn_w3, ffn_w2, router_w, moe_w1, moe_w3, moe_w2, ln3_g, ln3_b):
    p = dict(w_in=w_in, ssm_conv_w=ssm_conv_w, ssm_conv_b=ssm_conv_b, ssm_dt_bias=ssm_dt_bias,
             ssm_a_log=ssm_a_log, ssm_d=ssm_d, ssm_norm_w=ssm_norm_w, fox_f_bias=fox_f_bias, pool_w=pool_w,
             pool_b=pool_b, pool_scale=pool_scale, w_out=w_out, ln1_g=ln1_g, ln1_b=ln1_b, xa_wq=xa_wq,
             xa_wk=xa_wk, xa_wv=xa_wv, xa_wo=xa_wo, ln2_g=ln2_g, ln2_b=ln2_b, ffn_w1=ffn_w1, ffn_w3=ffn_w3,
             ffn_w2=ffn_w2, router_w=router_w, moe_w1=moe_w1, moe_w3=moe_w3, moe_w2=moe_w2, ln3_g=ln3_g,
             ln3_b=ln3_b)
    return _forward(x, mem, p)
```

```python
import functools

import jax
import jax.numpy as jnp
import numpy as np
from jax import lax
from jax.experimental import pallas as pl
from jax.experimental.pallas import tpu as pltpu

F32 = jnp.float32
BF16 = jnp.bfloat16

D_MODEL = 1024
DEPTH = 4
SSD_WIDTH = 512
SSD_HEAD_DIM = 64
SSD_HEADS = 8
SSD_GROUPS = 2
SSD_STATE = 128
SSD_CONV = 4
SSD_CONV_CH = 1024
FOX_WIDTH = 256
FOX_HEAD_DIM = 64
FOX_HEADS = 4
POOL_WIDTH = 256
POOL_WINDOWS = (2, 4, 8, 16)
POOL_GROUP_DIM = 64
IN_SIZES = (512, 1024, 8, 256, 256, 256, 4, 256)
XATTN_HEADS = 4
XATTN_HEAD_DIM = 256
D_FF = 3584
N_EXPERTS = 8
TOP_K = 2
DN_ALPHA = (2 * DEPTH) ** 0.25
LN_EPS = 1e-5
RMS_EPS = 1e-5

LANES = 128
MIB = 1024 * 1024

IN_TM = 512
SSD_TB = 512
SSD_L = 128
FOX_TQ = 2048
FOX_TK = 512
FOX_QW = 128
FOX_KSUB = 512
FOX_LOOKAHEAD = 4
POOL_TB = 512
POOL_PIECE = 128
OUT_TM = 1024
XA_TM = 1024
FFN_TM = 1024
FFN_TF = 512
MOE_TM = 1024
ROUTE_TM = 512
ROW_TM = 512


def _cparams(sem, vmem_mib=48):
    return pltpu.CompilerParams(dimension_semantics=sem, vmem_limit_bytes=vmem_mib * MIB)


def _dot(a, b):
    return jnp.dot(a, b, preferred_element_type=F32)


def _dot_nt(a, b):
    return lax.dot_general(a, b, (((1,), (1,)), ((), ())), preferred_element_type=F32)


def _dot_tn(a, b):
    return lax.dot_general(a, b, (((0,), (0,)), ((), ())), preferred_element_type=F32)


def _split3(x):
    hi = x.astype(BF16)
    r1 = x - hi.astype(F32)
    mid = r1.astype(BF16)
    lo = (r1 - mid.astype(F32)).astype(BF16)
    return hi, mid, lo


def _silu(x):
    return x / (1.0 + jnp.exp(-x))


def _softplus(x):
    return jnp.maximum(x, 0.0) + jnp.log1p(jnp.exp(-jnp.abs(x)))


def _log_sigmoid(x):
    return jnp.minimum(x, 0.0) - jnp.log1p(jnp.exp(-jnp.abs(x)))


def _layer_norm(v, g, b):
    mu = jnp.mean(v, axis=-1, keepdims=True)
    d = v - mu
    var = jnp.mean(d * d, axis=-1, keepdims=True)
    return d * lax.rsqrt(var + LN_EPS) * g + b


def _tri_ones(n, lower):
    r = lax.broadcasted_iota(jnp.int32, (n, n), 0)
    c = lax.broadcasted_iota(jnp.int32, (n, n), 1)
    m = (c <= r) if lower else (r <= c)
    return jnp.where(m, 1.0, 0.0).astype(BF16)


FOX_SLOT = LANES
FOX_EXT = FOX_HEADS * FOX_SLOT
FOX_BIAS_TERMS = 3
FOX_ONES_ROWS = 16
LOG2E = 1.4426950408889634


def _in_proj_kernel(x_ref, wm_ref, wkx_ref, wqt_ref, wvt_ref, ws_ref, wst_ref, fbc_ref, place_ref, qones_ref,
                    vones_ref, xbc_ref, z_ref, pool_ref, dte_ref, kx_ref, qt_ref, vt_ref, rows_ref,
                    carry_c, *, tiles_per_batch):
    i = pl.program_id(0)
    tm = x_ref.shape[0]
    xb = x_ref[...].astype(BF16)

    xbc_ref[...] = _dot(xb, wm_ref[:, 0:1024])
    z_ref[...] = _dot(xb, wm_ref[:, 1024:1536])
    pool_ref[...] = _dot(xb, wm_ref[:, 1536:1792])
    dte_ref[...] = _dot(xb, wm_ref[:, 1792:2304])
    qt_ref[...] = (_dot_nt(wqt_ref[...], xb) + qones_ref[...]).astype(BF16)
    vt_ref[...] = (_dot_nt(wvt_ref[...], xb) + vones_ref[...]).astype(BF16)
    rows_ref[...] = _dot_nt(wst_ref[...], xb)

    @pl.when(i % tiles_per_batch == 0)
    def _():
        carry_c[...] = jnp.zeros_like(carry_c)

    small_c = _dot(xb, ws_ref[...])
    c3 = _split3(_log_sigmoid(small_c + fbc_ref[...]))
    tri = _tri_ones(tm, lower=True)
    cs_c = _dot(tri, c3[0]) + _dot(tri, c3[1]) + _dot(tri, c3[2]) + carry_c[...]
    carry_c[...] = cs_c[tm - 1:tm, :]
    n3 = _split3(cs_c * (-LOG2E))
    kx = _dot(xb, wkx_ref[...])
    for term in range(FOX_BIAS_TERMS):
        kx = kx + _dot(n3[term], place_ref[term])
    kx_ref[...] = kx.astype(BF16)


def _in_proj(x2, wm, wkx, wqt, wvt, ws, wst, fbc, place, qones, vones, *, seq_len, tm):
    n = x2.shape[0]
    grid = (n // tm,)
    full = lambda a: pl.BlockSpec(a.shape, lambda i: (0,) * a.ndim)
    rowblk = lambda w: pl.BlockSpec((tm, w), lambda i: (i, 0))
    colblk = lambda h: pl.BlockSpec((h, tm), lambda i: (0, i))
    out_shape = (
        jax.ShapeDtypeStruct((n, 1024), F32),
        jax.ShapeDtypeStruct((n, 512), F32),
        jax.ShapeDtypeStruct((n, 256), F32),
        jax.ShapeDtypeStruct((n, 512), F32),
        jax.ShapeDtypeStruct((n, FOX_EXT), BF16),
        jax.ShapeDtypeStruct((FOX_EXT, n), BF16),
        jax.ShapeDtypeStruct((FOX_EXT, n), BF16),
        jax.ShapeDtypeStruct((16, n), F32),
    )
    out_specs = (rowblk(1024), rowblk(512), rowblk(256), rowblk(512), rowblk(FOX_EXT),
                 colblk(FOX_EXT), colblk(FOX_EXT), colblk(16))
    args = (x2, wm, wkx, wqt, wvt, ws, wst, fbc, place, qones, vones)
    return pl.pallas_call(
        functools.partial(_in_proj_kernel, tiles_per_batch=seq_len // tm),
        out_shape=out_shape,
        grid=grid,
        in_specs=[rowblk(D_MODEL)] + [full(a) for a in args[1:]],
        out_specs=out_specs,
        scratch_shapes=[pltpu.VMEM((1, LANES), F32)],
        compiler_params=_cparams(("arbitrary",)),
        name="in_proj",
    )(*args)


def _ssd_kernel(xbc_ref, z_ref, dte_ref, rows_ref, cw_ref, cb_ref, dtb_ref, alog_ref, dsk_ref, nw_ref,
                dtbr_ref, alogr_ref, y_ref, tail, ubuf, xc, state, *, chunk):
    t = pl.program_id(1)
    tb = xbc_ref.shape[0]
    L = chunk
    CONV_PIECE = 64
    HALO = 8

    @pl.when(t == 0)
    def _():
        tail[...] = jnp.zeros_like(tail)
        state[...] = jnp.zeros_like(state)

    ubuf[0:HALO, :] = tail[...]
    ubuf[HALO:HALO + tb, :] = xbc_ref[...]
    tail[...] = xbc_ref[tb - HALO:tb, :]
    for r0 in range(0, tb, CONV_PIECE):
        window = ubuf[r0:r0 + HALO + CONV_PIECE, :]
        acc = jnp.broadcast_to(cb_ref[...], (CONV_PIECE, SSD_CONV_CH))
        for k in range(SSD_CONV):
            delay = SSD_CONV - 1 - k
            tap = window if delay == 0 else pltpu.roll(window, shift=delay, axis=0)
            acc = acc + cw_ref[k:k + 1, :] * tap[HALO:HALO + CONV_PIECE, :]
        xc[r0:r0 + CONV_PIECE, :] = _silu(acc)

    a_e = -jnp.exp(alog_ref[...])
    a_r = -jnp.exp(alogr_ref[...])
    tri = _tri_ones(L, lower=True)
    upp = _tri_ones(L, lower=False)
    rr = lax.broadcasted_iota(jnp.int32, (L, L), 0)
    cc = lax.broadcasted_iota(jnp.int32, (L, L), 1)
    causal = cc <= rr
    lane = lax.broadcasted_iota(jnp.int32, (L, LANES), 1)
    left = lane < SSD_HEAD_DIM
    HG = SSD_HEADS // SSD_GROUPS
    GW = SSD_WIDTH // SSD_GROUPS

    def body(c, carry):
        r0 = pl.multiple_of(c * L, L)
        rows = pl.ds(r0, L)
        xs = xc[rows, 0:SSD_WIDTH]
        dt_e = _softplus(dte_ref[rows, :] + dtb_ref[...])
        a3 = _split3(dt_e * a_e)
        acs_e = _dot(tri, a3[0]) + _dot(tri, a3[1]) + _dot(tri, a3[2])
        dt_r = _softplus(rows_ref[:, rows] + dtbr_ref[...])
        ar3 = _split3(dt_r * a_r)
        acs_r = _dot(ar3[0], upp) + _dot(ar3[1], upp) + _dot(ar3[2], upp)
        total = acs_e[L - 1:L, :]
        x_dt = xs * dt_e
        xb = x_dt.astype(BF16)
        xd = (x_dt * jnp.exp(total - acs_e)).astype(BF16)
        eacs = jnp.exp(acs_e)
        y_parts = []
        for g in range(SSD_GROUPS):
            bm = xc[rows, SSD_WIDTH + g * SSD_STATE:SSD_WIDTH + (g + 1) * SSD_STATE].astype(BF16)
            cm = xc[rows, SSD_WIDTH + (SSD_GROUPS + g) * SSD_STATE:
                    SSD_WIDTH + (SSD_GROUPS + g + 1) * SSD_STATE].astype(BF16)
            cb = _dot_nt(cm, bm)
            for pair in range(HG // 2):
                lo = g * GW + pair * LANES
                x_pair = xb[:, lo:lo + LANES]
                outs = []
                for sub in range(2):
                    h = g * HG + pair * 2 + sub
                    seg = acs_e[:, h * SSD_HEAD_DIM:h * SSD_HEAD_DIM + 1] - acs_r[h:h + 1, :]
                    decay = jnp.exp(jnp.where(causal, seg, -jnp.inf))
                    outs.append(_dot((cb * decay).astype(BF16), x_pair))
                y_parts.append(jnp.where(left, outs[0], outs[1]))
            s_prev = state[g]
            y_off = _dot(cm, s_prev.astype(BF16)) * eacs[:, g * GW:(g + 1) * GW]
            y_parts[-2] = y_parts[-2] + y_off[:, 0:LANES]
            y_parts[-1] = y_parts[-1] + y_off[:, LANES:2 * LANES]
            state[g] = s_prev * jnp.exp(total[:, g * GW:(g + 1) * GW]) + _dot_tn(bm, xd[:, g * GW:(g + 1) * GW])
        y = jnp.concatenate(y_parts, axis=1) + xs * dsk_ref[...]
        y = y * _silu(z_ref[rows, :])
        outs = []
        for g in range(SSD_GROUPS):
            yg = y[:, g * GW:(g + 1) * GW]
            ms = jnp.mean(yg * yg, axis=-1, keepdims=True)
            outs.append(yg * lax.rsqrt(ms + RMS_EPS))
        y_ref[rows, :] = (jnp.concatenate(outs, axis=1) * nw_ref[...]).astype(y_ref.dtype)
        return carry

    lax.fori_loop(0, tb // L, body, 0, unroll=True)


def _ssd(xbc, z, dte, rowsp, cw, cb, dtb_e, alog_e, dsk_e, nw, dtb_r, alog_r, *, batch, seq_len, tb, chunk):
    n = xbc.shape[0]
    tpb = seq_len // tb
    full = lambda a: pl.BlockSpec(a.shape, lambda b, t: (0,) * a.ndim)
    rowblk = lambda w: pl.BlockSpec((tb, w), lambda b, t: (b * tpb + t, 0))
    return pl.pallas_call(
        functools.partial(_ssd_kernel, chunk=chunk),
        out_shape=jax.ShapeDtypeStruct((n, SSD_WIDTH), BF16),
        grid=(batch, tpb),
        in_specs=[rowblk(SSD_CONV_CH), rowblk(SSD_WIDTH), rowblk(SSD_WIDTH),
                  pl.BlockSpec((16, tb), lambda b, t: (0, b * tpb + t)),
                  full(cw), full(cb), full(dtb_e), full(alog_e), full(dsk_e), full(nw),
                  full(dtb_r), full(alog_r)],
        out_specs=rowblk(SSD_WIDTH),
        scratch_shapes=[pltpu.VMEM((8, SSD_CONV_CH), F32),
                        pltpu.VMEM((tb + 8, SSD_CONV_CH), F32),
                        pltpu.VMEM((tb, SSD_CONV_CH), F32),
                        pltpu.VMEM((SSD_GROUPS, SSD_STATE, SSD_WIDTH // SSD_GROUPS), F32)],
        compiler_params=_cparams(("parallel", "arbitrary")),
        name="ssd_mixer",
    )(xbc, z, dte, rowsp, cw, cb, dtb_e, alog_e, dsk_e, nw, dtb_r, alog_r)


def _fox_kernel(qi_ref, kj_ref, kx_ref, qt_ref, vt_ref, o_ref, m_sc, acc_sc, *, ratio, qw):
    step_id = pl.program_id(1)
    i = qi_ref[step_id]
    j = kj_ref[step_id]
    tk = kx_ref.shape[0]
    tq = qt_ref.shape[1]
    vrows = FOX_HEAD_DIM + FOX_ONES_ROWS

    @pl.when(j == 0)
    def _():
        m_sc[...] = jnp.full_like(m_sc, -jnp.inf)
        acc_sc[...] = jnp.zeros_like(acc_sc)

    def step(diag):
        ksub = min(FOX_KSUB, tk)
        units = [(k0, h, c0) for k0 in range(0, tk, ksub) for h in range(FOX_HEADS) for c0 in range(0, tq, qw)]
        if diag is not None:
            units = [(k0, h, c0) for k0, h, c0 in units if c0 + qw > diag * tk + k0]

        def score(unit):
            k0, h, c0 = unit
            return _dot(kx_ref[k0:k0 + ksub, h * FOX_SLOT:(h + 1) * FOX_SLOT],
                        qt_ref[h * FOX_SLOT:(h + 1) * FOX_SLOT, c0:c0 + qw])

        scores = [score(u) for u in units[:FOX_LOOKAHEAD]]
        for n_unit, (k0, h, c0) in enumerate(units):
            if n_unit + FOX_LOOKAHEAD < len(units):
                scores.append(score(units[n_unit + FOX_LOOKAHEAD]))
            st = scores[n_unit]
            lo = h * FOX_SLOT
            if diag is not None and c0 < diag * tk + k0 + ksub - 1:
                key = diag * tk + k0 + lax.broadcasted_iota(jnp.int32, (ksub, qw), 0)
                qry = c0 + lax.broadcasted_iota(jnp.int32, (ksub, qw), 1)
                st = jnp.where(key <= qry, st, -jnp.inf)
            m_prev = m_sc[h, :, c0:c0 + qw]
            m_new = jnp.maximum(m_prev, jnp.max(st, axis=0, keepdims=True))
            alpha = jnp.exp2(m_prev - m_new)
            p = jnp.exp2(st - m_new).astype(BF16)
            acc_sc[h, :, c0:c0 + qw] = (alpha * acc_sc[h, :, c0:c0 + qw]
                                        + _dot(vt_ref[lo:lo + vrows, k0:k0 + ksub], p))
            m_sc[h, :, c0:c0 + qw] = m_new

    @pl.when(j < i * ratio)
    def _():
        step(None)

    for r in range(ratio):
        @pl.when(j == i * ratio + r)
        def _():
            step(r)

    @pl.when(j == (i + 1) * ratio - 1)
    def _():
        parts = [acc_sc[h, 0:FOX_HEAD_DIM, :] / acc_sc[h, FOX_HEAD_DIM:FOX_HEAD_DIM + 1, :]
                 for h in range(FOX_HEADS)]
        o_ref[...] = jnp.concatenate(parts, axis=0).T.astype(o_ref.dtype)


def _fox(kx, qt, vt, *, batch, seq_len, tq, tk):
    n = kx.shape[0]
    assert tq % tk == 0
    ratio = tq // tk
    nq = seq_len // tq
    nk = seq_len // tk
    pairs = [(i, j) for i in range(nq) for j in range((i + 1) * ratio)]
    qi = jnp.asarray([pq for pq, _ in pairs], jnp.int32)
    kj = jnp.asarray([pk for _, pk in pairs], jnp.int32)
    return pl.pallas_call(
        functools.partial(_fox_kernel, ratio=ratio, qw=min(FOX_QW, tq)),
        out_shape=jax.ShapeDtypeStruct((n, FOX_WIDTH), BF16),
        grid_spec=pltpu.PrefetchScalarGridSpec(
            num_scalar_prefetch=2,
            grid=(batch, len(pairs)),
            in_specs=[pl.BlockSpec((tk, FOX_EXT), lambda b, s, qi, kj: (b * nk + kj[s], 0)),
                      pl.BlockSpec((FOX_EXT, tq), lambda b, s, qi, kj: (0, b * nq + qi[s])),
                      pl.BlockSpec((FOX_EXT, tk), lambda b, s, qi, kj: (0, b * nk + kj[s]))],
            out_specs=pl.BlockSpec((tq, FOX_WIDTH), lambda b, s, qi, kj: (b * nq + qi[s], 0)),
            scratch_shapes=[pltpu.VMEM((FOX_HEADS, 1, tq), F32),
                            pltpu.VMEM((FOX_HEADS, FOX_HEAD_DIM + FOX_ONES_ROWS, tq), F32)]),
        compiler_params=_cparams(("parallel", "arbitrary")),
        name="fox_attention",
    )(qi, kj, kx, qt, vt)


POOL_HALO = 16


def _pool_kernel(u_ref, w_ref, b_ref, sc_ref, y_ref, tail, ubuf):
    t = pl.program_id(1)
    tb = u_ref.shape[0]
    P = POOL_PIECE

    @pl.when(t == 0)
    def _():
        tail[...] = jnp.zeros_like(tail)

    ubuf[0:POOL_HALO, :] = tail[...]
    ubuf[POOL_HALO:POOL_HALO + tb, :] = u_ref[...]
    tail[...] = u_ref[tb - POOL_HALO:tb, :]
    group = lax.broadcasted_iota(jnp.int32, (P, POOL_WIDTH), 1) // POOL_GROUP_DIM
    for p0 in range(0, tb, P):
        w1 = ubuf[p0:p0 + P + POOL_HALO, :]
        n1 = P + POOL_HALO
        a2 = w1[1:n1] + w1[0:n1 - 1]
        a4 = a2[2:n1 - 1] + a2[0:n1 - 3]
        a8 = a4[4:n1 - 3] + a4[0:n1 - 7]
        a16 = a8[8:n1 - 7] + a8[0:n1 - 15]
        sums = (a2[15:15 + P], a4[13:13 + P], a8[9:9 + P], a16[1:1 + P])
        tpos = t * tb + p0 + lax.broadcasted_iota(jnp.int32, (P, 1), 0) + 1
        pooled = jnp.zeros((P, POOL_WIDTH), F32)
        for g, win in enumerate(POOL_WINDOWS):
            cnt = jnp.minimum(tpos, win).astype(F32)
            pooled = jnp.where(group == g, sums[g] / cnt, pooled)
        pooled = pooled - w1[POOL_HALO:POOL_HALO + P]
        y = _dot(pooled.astype(BF16), w_ref[...]) + b_ref[...]
        y_ref[p0:p0 + P, :] = (y * sc_ref[...]).astype(y_ref.dtype)


def _pool(u, wbd, b, sc, *, batch, seq_len, tb):
    n = u.shape[0]
    tpb = seq_len // tb
    full = lambda a: pl.BlockSpec(a.shape, lambda bb, t: (0,) * a.ndim)
    rowblk = pl.BlockSpec((tb, POOL_WIDTH), lambda bb, t: (bb * tpb + t, 0))
    return pl.pallas_call(
        _pool_kernel,
        out_shape=jax.ShapeDtypeStruct((n, POOL_WIDTH), BF16),
        grid=(batch, tpb),
        in_specs=[rowblk, full(wbd), full(b), full(sc)],
        out_specs=rowblk,
        scratch_shapes=[pltpu.VMEM((POOL_HALO, POOL_WIDTH), F32),
                        pltpu.VMEM((tb + POOL_HALO, POOL_WIDTH), F32)],
        compiler_params=_cparams(("parallel", "arbitrary")),
        name="pool_mixer",
    )(u, wbd, b, sc)


def _out_proj_kernel(x_ref, ys_ref, yf_ref, yp_ref, w_ref, g_ref, b_ref, o_ref):
    mix = (_dot(ys_ref[...], w_ref[0:512, :]) + _dot(yf_ref[...], w_ref[512:768, :])
           + _dot(yp_ref[...], w_ref[768:1024, :]))
    o_ref[...] = _layer_norm(DN_ALPHA * x_ref[...] + mix, g_ref[...], b_ref[...])


def _out_proj(x2, ys, yf, yp, w, g, b, *, tm):
    n = x2.shape[0]
    full = lambda a: pl.BlockSpec(a.shape, lambda i: (0,) * a.ndim)
    rowblk = lambda wd: pl.BlockSpec((tm, wd), lambda i: (i, 0))
    return pl.pallas_call(
        _out_proj_kernel,
        out_shape=jax.ShapeDtypeStruct((n, D_MODEL), F32),
        grid=(n // tm,),
        in_specs=[rowblk(D_MODEL), rowblk(512), rowblk(256), rowblk(256), full(w), full(g), full(b)],
        out_specs=rowblk(D_MODEL),
        compiler_params=_cparams(("parallel",)),
        name="out_proj_ln",
    )(x2, ys, yf, yp, w, g, b)


def _kv_proj_kernel(m_ref, w_ref, o_ref):
    o_ref[...] = _dot(m_ref[...].astype(BF16), w_ref[...]).astype(o_ref.dtype)


def _kv_proj(mem2, wkv):
    m = mem2.shape[0]
    return pl.pallas_call(
        _kv_proj_kernel,
        out_shape=jax.ShapeDtypeStruct((m, wkv.shape[1]), BF16),
        grid=(1,),
        in_specs=[pl.BlockSpec(mem2.shape, lambda i: (0, 0)), pl.BlockSpec(wkv.shape, lambda i: (0, 0))],
        out_specs=pl.BlockSpec((m, wkv.shape[1]), lambda i: (0, 0)),
        compiler_params=_cparams(("arbitrary",)),
        name="xattn_kv_proj",
    )(mem2, wkv)


def _xattn_kernel(x_ref, kv_ref, wq_ref, wo_ref, g_ref, b_ref, o_ref):
    x = x_ref[...]
    q = _dot(x.astype(BF16), wq_ref[...]).astype(BF16)
    heads = []
    for h in range(XATTN_HEADS):
        lo = h * XATTN_HEAD_DIM
        k_h = kv_ref[:, lo:lo + XATTN_HEAD_DIM]
        v_h = kv_ref[:, D_MODEL + lo:D_MODEL + lo + XATTN_HEAD_DIM]
        s = _dot_nt(q[:, lo:lo + XATTN_HEAD_DIM], k_h)
        p = jnp.exp(s - jnp.max(s, axis=-1, keepdims=True))
        l = jnp.sum(p, axis=-1, keepdims=True)
        heads.append((_dot(p.astype(BF16), v_h) / l).astype(BF16))
    o = jnp.concatenate(heads, axis=1)
    xa = _dot(o, wo_ref[...])
    o_ref[...] = _layer_norm(DN_ALPHA * x + xa, g_ref[...], b_ref[...])


def _xattn(x2, kv, wq, wo, g, b, *, batch, seq_len, mem_len, tm):
    n = x2.shape[0]
    tpb = seq_len // tm
    full = lambda a: pl.BlockSpec(a.shape, lambda bb, t: (0,) * a.ndim)
    rowblk = pl.BlockSpec((tm, D_MODEL), lambda bb, t: (bb * tpb + t, 0))
    return pl.pallas_call(
        _xattn_kernel,
        out_shape=jax.ShapeDtypeStruct((n, D_MODEL), F32),
        grid=(batch, tpb),
        in_specs=[rowblk, pl.BlockSpec((mem_len, 2 * D_MODEL), lambda bb, t: (bb, 0)),
                  full(wq), full(wo), full(g), full(b)],
        out_specs=rowblk,
        compiler_params=_cparams(("parallel", "parallel")),
        name="xattn_ln",
    )(x2, kv, wq, wo, g, b)


ROW_SLABS = D_MODEL // LANES


def _ffn_kernel(texp_ref, nvalid_ref, x_ref, w1_ref, w3_ref, w2_ref, g_ref, b_ref, o_ref, xb_sc, acc_sc,
                *, grouped):
    t = pl.program_id(0)
    f = pl.program_id(1)
    nf = pl.num_programs(1)

    @pl.when(t < nvalid_ref[0])
    def _():
        @pl.when(f == 0)
        def _():
            xb_sc[...] = x_ref[...].reshape(xb_sc.shape).astype(BF16)
            acc_sc[...] = jnp.zeros_like(acc_sc)

        xb = xb_sc[...]
        h1 = _dot(xb, w1_ref[0].astype(BF16))
        h3 = _dot(xb, w3_ref[0].astype(BF16))
        acc_sc[...] += _dot((_silu(h1) * h3).astype(BF16), w2_ref[0].astype(BF16))

        @pl.when(f == nf - 1)
        def _():
            if grouped:
                o_ref[...] = acc_sc[...].reshape(o_ref.shape)
            else:
                o_ref[...] = _layer_norm(DN_ALPHA * x_ref[...] + acc_sc[...], g_ref[...], b_ref[...])

    @pl.when((t >= nvalid_ref[0]) & (f == 0))
    def _():
        o_ref[...] = jnp.zeros_like(o_ref)


def _ffn(tile_expert, nvalid, x, w1, w3, w2, g, b, *, tm, tf, grouped):
    n = x.shape[0]
    nt = n // tm
    nf = D_FF // tf

    def tile_of(t, nv):
        return jnp.minimum(t, nv[0] - 1)

    def f_of(t, f, nv):
        return jnp.where(t < nv[0], f, nf - 1)

    if grouped:
        rowblk = pl.BlockSpec((tm, ROW_SLABS, LANES), lambda t, f, te, nv: (tile_of(t, nv), 0, 0))
        outblk = pl.BlockSpec((tm, ROW_SLABS, LANES), lambda t, f, te, nv: (t, 0, 0))
        out_shape = jax.ShapeDtypeStruct((n, ROW_SLABS, LANES), F32)
    else:
        rowblk = pl.BlockSpec((tm, D_MODEL), lambda t, f, te, nv: (tile_of(t, nv), 0))
        outblk = pl.BlockSpec((tm, D_MODEL), lambda t, f, te, nv: (t, 0))
        out_shape = jax.ShapeDtypeStruct((n, D_MODEL), F32)
    w13 = pl.BlockSpec((1, D_MODEL, tf), lambda t, f, te, nv: (te[tile_of(t, nv)], 0, f_of(t, f, nv)))
    w2s = pl.BlockSpec((1, tf, D_MODEL), lambda t, f, te, nv: (te[tile_of(t, nv)], f_of(t, f, nv), 0))
    vec = pl.BlockSpec((1, D_MODEL), lambda t, f, te, nv: (0, 0))
    return pl.pallas_call(
        functools.partial(_ffn_kernel, grouped=grouped),
        out_shape=out_shape,
        grid_spec=pltpu.PrefetchScalarGridSpec(
            num_scalar_prefetch=2,
            grid=(nt, nf),
            in_specs=[rowblk, w13, w13, w2s, vec, vec],
            out_specs=outblk,
            scratch_shapes=[pltpu.VMEM((tm, D_MODEL), BF16), pltpu.VMEM((tm, D_MODEL), F32)]),
        compiler_params=_cparams(("arbitrary", "arbitrary"), vmem_mib=56),
        name="swiglu_grouped" if grouped else "swiglu_ln",
    )(tile_expert, nvalid, x, w1, w3, w2, g, b)


def _router_kernel(x_ref, w_ref, route_ref, counts_ref, carry):
    i = pl.program_id(0)
    tm = x_ref.shape[0]

    @pl.when(i == 0)
    def _():
        carry[...] = jnp.zeros_like(carry)

    xh, xm, _ = _split3(x_ref[...])
    wh, wm, _ = _split3(w_ref[...])
    logits = _dot(xh, wh) + (_dot(xh, wm) + _dot(xm, wh))
    lane = lax.broadcasted_iota(jnp.int32, (tm, LANES), 1)
    logits = jnp.where(lane < N_EXPERTS, logits, -jnp.inf)
    m1 = jnp.max(logits, axis=-1, keepdims=True)
    i1 = jnp.min(jnp.where(logits == m1, lane, LANES), axis=-1, keepdims=True)
    rest = jnp.where(lane == i1, -jnp.inf, logits)
    m2 = jnp.max(rest, axis=-1, keepdims=True)
    i2 = jnp.min(jnp.where(rest == m2, lane, LANES), axis=-1, keepdims=True)
    e21 = jnp.exp(m2 - m1)
    g1 = 1.0 / (1.0 + e21)
    g2 = e21 / (1.0 + e21)
    hit1 = lane == i1
    hit2 = lane == i2
    onehot = jnp.where(hit1 | hit2, 1.0, 0.0).astype(BF16)
    r = lax.broadcasted_iota(jnp.int32, (tm, tm), 0)
    c = lax.broadcasted_iota(jnp.int32, (tm, tm), 1)
    strict = jnp.where(c < r, 1.0, 0.0).astype(BF16)
    before = _dot(strict, onehot) + carry[...]
    rank1 = jnp.sum(jnp.where(hit1, before, 0.0), axis=-1, keepdims=True)
    rank2 = jnp.sum(jnp.where(hit2, before, 0.0), axis=-1, keepdims=True)
    carry[...] = carry[...] + jnp.sum(onehot.astype(F32), axis=0, keepdims=True)
    out = jnp.where(lane == 0, i1.astype(F32), 0.0)
    out = jnp.where(lane == 1, i2.astype(F32), out)
    out = jnp.where(lane == 2, g1, out)
    out = jnp.where(lane == 3, g2, out)
    out = jnp.where(lane == 4, rank1, out)
    out = jnp.where(lane == 5, rank2, out)
    route_ref[...] = out
    counts_ref[...] = carry[...]


def _router(x2, wr, *, tm):
    n = x2.shape[0]
    return pl.pallas_call(
        _router_kernel,
        out_shape=(jax.ShapeDtypeStruct((n, LANES), F32), jax.ShapeDtypeStruct((1, LANES), F32)),
        grid=(n // tm,),
        in_specs=[pl.BlockSpec((tm, D_MODEL), lambda i: (i, 0)), pl.BlockSpec(wr.shape, lambda i: (0, 0))],
        out_specs=(pl.BlockSpec((tm, LANES), lambda i: (i, 0)), pl.BlockSpec((1, LANES), lambda i: (0, 0))),
        scratch_shapes=[pltpu.VMEM((1, LANES), F32)],
        compiler_params=_cparams(("arbitrary",)),
        name="moe_router",
    )(x2, wr)


POS_ROWS = 8


ROW_UNROLL = 8


def _for_each_row_copy(row_copy, action, pos_rows=(0, POS_ROWS)):
    tokens_per_pos_row = LANES // TOP_K
    for prow in range(*pos_rows):
        def body(c, carry):
            for k in range(TOP_K):
                copy = row_copy(prow * tokens_per_pos_row + c, (prow, TOP_K * c + k), k)
                if action == "start":
                    copy.start(priority=k % 2)
                else:
                    copy.wait()
            return carry

        lax.fori_loop(0, tokens_per_pos_row, body, 0, unroll=ROW_UNROLL)


def _dispatch_kernel(pend_ref, padded_ref, pos_hbm, x_ref, xs_hbm, pos_smem, stage, zeros_vmem, sem_pos, sem_rows,
                     *, row_tm, pad_tm):
    i = pl.program_id(0)
    last = pl.num_programs(0) - 1

    def zero_copy(start):
        return pltpu.make_async_copy(zeros_vmem, xs_hbm.at[pl.ds(start, pad_tm)], sem_rows.at[0])

    @pl.when(i == 0)
    def _():
        zeros_vmem[...] = jnp.zeros_like(zeros_vmem)
        used = pend_ref[N_EXPERTS - 1]
        fills = [(padded_ref[e] > 0, pend_ref[e] - pad_tm) for e in range(N_EXPERTS)]
        fills += [(used + e * pad_tm < xs_hbm.shape[0], used + e * pad_tm) for e in range(N_EXPERTS)]
        for cond, start in fills:
            @pl.when(cond)
            def _():
                zero_copy(start).start()
        for cond, start in fills:
            @pl.when(cond)
            def _():
                zero_copy(start).wait()

    def run(slot):
        def row_copies(s):
            def row_copy(r, entry, k):
                return pltpu.make_async_copy(stage.at[s, r], xs_hbm.at[pos_smem[(s,) + entry]], sem_rows.at[s])
            return row_copy

        pos_copy = pltpu.make_async_copy(pos_hbm.at[i], pos_smem.at[slot], sem_pos)
        pos_copy.start()
        stage[slot] = x_ref[...].reshape(stage.shape[1:])
        pos_copy.wait()
        _for_each_row_copy(row_copies(slot), "start")

        @pl.when(i > 0)
        def _():
            _for_each_row_copy(row_copies(1 - slot), "wait")

        @pl.when(i == last)
        def _():
            _for_each_row_copy(row_copies(slot), "wait")

    for slot in range(2):
        @pl.when(i % 2 == slot)
        def _():
            run(slot)


def _dispatch(pend, padded, pos3, x2, *, cap, row_tm, pad_tm):
    n = x2.shape[0]
    assert TOP_K * row_tm == POS_ROWS * LANES
    return pl.pallas_call(
        functools.partial(_dispatch_kernel, row_tm=row_tm, pad_tm=pad_tm),
        out_shape=jax.ShapeDtypeStruct((cap, ROW_SLABS, LANES), F32),
        grid_spec=pltpu.PrefetchScalarGridSpec(
            num_scalar_prefetch=2,
            grid=(n // row_tm,),
            in_specs=[pl.BlockSpec(memory_space=pl.ANY),
                      pl.BlockSpec((row_tm, D_MODEL), lambda i, pe, pa: (i, 0))],
            out_specs=pl.BlockSpec(memory_space=pl.ANY),
            scratch_shapes=[pltpu.SMEM((2, POS_ROWS, LANES), jnp.int32),
                            pltpu.VMEM((2, row_tm, ROW_SLABS, LANES), F32),
                            pltpu.VMEM((pad_tm, ROW_SLABS, LANES), F32),
                            pltpu.SemaphoreType.DMA(()),
                            pltpu.SemaphoreType.DMA((2,))]),
        compiler_params=_cparams(("arbitrary",)),
        name="moe_dispatch",
    )(pend, padded, pos3, x2)


def _combine_kernel(pos_hbm, x_ref, route_ref, ys_hbm, g_ref, b_ref, o_ref, pos_smem, rows0, rows1,
                    sem_pos, sem_rows, *, row_tm):
    i = pl.program_id(0)
    pos_copy = pltpu.make_async_copy(pos_hbm.at[i], pos_smem, sem_pos)
    pos_copy.start()
    pos_copy.wait()
    rows = (rows0, rows1)

    halves = ((0, POS_ROWS // 2), (POS_ROWS // 2, POS_ROWS))
    half_tm = row_tm // 2

    def row_copies(half):
        def row_copy(r, entry, k):
            return pltpu.make_async_copy(ys_hbm.at[pos_smem[entry]], rows[k].at[r], sem_rows.at[half])
        return row_copy

    for half, pos_rows in enumerate(halves):
        _for_each_row_copy(row_copies(half), "start", pos_rows)
    for half, pos_rows in enumerate(halves):
        _for_each_row_copy(row_copies(half), "wait", pos_rows)
        tok = pl.ds(half * half_tm, half_tm)
        flat = (half_tm, D_MODEL)
        ff = (rows0[tok].reshape(flat) * route_ref[tok, 2:3] + rows1[tok].reshape(flat) * route_ref[tok, 3:4])
        o_ref[tok, :] = _layer_norm(DN_ALPHA * x_ref[tok, :] + ff, g_ref[...], b_ref[...])


def _combine(pos3, x2, route, ys, g, b, *, row_tm):
    n = x2.shape[0]
    assert TOP_K * row_tm == POS_ROWS * LANES
    rowblk = lambda w: pl.BlockSpec((row_tm, w), lambda i: (i, 0))
    vec = pl.BlockSpec((1, D_MODEL), lambda i: (0, 0))
    return pl.pallas_call(
        functools.partial(_combine_kernel, row_tm=row_tm),
        out_shape=jax.ShapeDtypeStruct((n, D_MODEL), F32),
        grid=(n // row_tm,),
        in_specs=[pl.BlockSpec(memory_space=pl.ANY), rowblk(D_MODEL), rowblk(LANES),
                  pl.BlockSpec(memory_space=pl.ANY), vec, vec],
        out_specs=rowblk(D_MODEL),
        scratch_shapes=[pltpu.SMEM((POS_ROWS, LANES), jnp.int32),
                        pltpu.VMEM((row_tm, ROW_SLABS, LANES), F32),
                        pltpu.VMEM((row_tm, ROW_SLABS, LANES), F32),
                        pltpu.SemaphoreType.DMA(()),
                        pltpu.SemaphoreType.DMA((2,))],
        compiler_params=_cparams(("arbitrary",)),
        name="moe_combine_ln",
    )(pos3, x2, route, ys, g, b)


def _fox_constants():
    place = np.zeros((FOX_BIAS_TERMS, LANES, FOX_EXT), np.float32)
    qones = np.zeros((FOX_EXT, 1), np.float32)
    vones = np.zeros((FOX_EXT, 1), np.float32)
    for h in range(FOX_HEADS):
        base = h * FOX_SLOT + FOX_HEAD_DIM
        for term in range(FOX_BIAS_TERMS):
            place[term, SSD_HEADS + h, base + term] = 1.0
        qones[base:base + FOX_BIAS_TERMS] = 1.0
        vones[base:base + FOX_ONES_ROWS] = 1.0
    return jnp.asarray(place, BF16), jnp.asarray(qones), jnp.asarray(vones)


def _prepare(p):
    depth = p["w_in"].shape[0]
    offs = np.concatenate([[0], np.cumsum(IN_SIZES)])
    w_in = p["w_in"].astype(F32)
    wz, wxbc, wdt, wq, wk, wv, wf, wp = (w_in[:, :, offs[k]:offs[k + 1]] for k in range(len(IN_SIZES)))

    def slots(w):
        w = w.reshape(depth, D_MODEL, FOX_HEADS, FOX_HEAD_DIM)
        w = jnp.pad(w, ((0, 0), (0, 0), (0, 0), (0, FOX_SLOT - FOX_HEAD_DIM)))
        return w.reshape(depth, D_MODEL, FOX_EXT)

    def vec(v, width):
        return v.astype(F32).reshape(v.shape[0], 1, width)

    def per_head(v):
        return jnp.repeat(v.astype(F32), SSD_HEAD_DIM, axis=1).reshape(depth, 1, SSD_WIDTH)

    def head_rows(v):
        return jnp.pad(v.astype(F32), ((0, 0), (0, 16 - SSD_HEADS))).reshape(depth, 16, 1)

    ws = jnp.concatenate([wdt, wf, jnp.zeros((depth, D_MODEL, LANES - SSD_HEADS - FOX_HEADS), F32)], axis=2)
    eye = jnp.asarray(np.eye(len(POOL_WINDOWS), dtype=np.float32))
    q = dict(
        wm=jnp.concatenate([wxbc, wz, wp, jnp.repeat(wdt, SSD_HEAD_DIM, axis=2)], axis=2).astype(BF16),
        wkx=slots(wk).astype(BF16),
        wqt=jnp.swapaxes(slots(wq * (FOX_HEAD_DIM ** -0.5 * LOG2E)), 1, 2).astype(BF16),
        wvt=jnp.swapaxes(slots(wv), 1, 2).astype(BF16),
        ws=ws.astype(BF16),
        wst=jnp.swapaxes(ws[:, :, 0:16], 1, 2).astype(BF16),
        fbc=jnp.pad(p["fox_f_bias"].astype(F32), ((0, 0), (SSD_HEADS, LANES - SSD_HEADS - FOX_HEADS))
                    ).reshape(depth, 1, LANES),
        conv_w=p["ssm_conv_w"].astype(F32), conv_b=vec(p["ssm_conv_b"], SSD_CONV_CH),
        dtb_e=per_head(p["ssm_dt_bias"]), alog_e=per_head(p["ssm_a_log"]), dsk_e=per_head(p["ssm_d"]),
        norm_w=vec(p["ssm_norm_w"], SSD_WIDTH),
        dtb_r=head_rows(p["ssm_dt_bias"]), alog_r=head_rows(p["ssm_a_log"]),
        pool_wbd=jnp.einsum("lgij,gh->lgihj", p["pool_w"].astype(F32), eye
                            ).reshape(depth, POOL_WIDTH, POOL_WIDTH).astype(BF16),
        pool_b=vec(p["pool_b"].reshape(depth, POOL_WIDTH), POOL_WIDTH), pool_sc=vec(p["pool_scale"], POOL_WIDTH),
        w_out=p["w_out"].astype(BF16),
        xa_wkv=jnp.concatenate([p["xa_wk"], p["xa_wv"]], axis=2).astype(BF16),
        xa_wq=(p["xa_wq"] * (XATTN_HEAD_DIM ** -0.5)).astype(BF16),
        xa_wo=p["xa_wo"].astype(BF16),
        ffn_w1=p["ffn_w1"].astype(BF16), ffn_w3=p["ffn_w3"].astype(BF16), ffn_w2=p["ffn_w2"].astype(BF16),
        moe_w1=p["moe_w1"].reshape(-1, D_MODEL, D_FF),
        moe_w3=p["moe_w3"].reshape(-1, D_MODEL, D_FF),
        moe_w2=p["moe_w2"].reshape(-1, D_FF, D_MODEL),
        router=jnp.pad(p["router_w"].astype(F32), ((0, 0), (0, 0), (0, LANES - N_EXPERTS))),
    )
    for name in ("ln1_g", "ln1_b", "ln2_g", "ln2_b", "ln3_g", "ln3_b"):
        q[name] = vec(p[name], D_MODEL)
    return q


def _token_mixing(x2, q, layer, *, batch, seq_len):
    place, qones, vones = _fox_constants()
    xbc, z, pool_in, dte, kx, qt, vt, rowsp = _in_proj(
        x2, q["wm"][layer], q["wkx"][layer], q["wqt"][layer], q["wvt"][layer], q["ws"][layer], q["wst"][layer],
        q["fbc"][layer], place, qones, vones, seq_len=seq_len, tm=min(IN_TM, seq_len))

    y_ssd = _ssd(xbc, z, dte, rowsp, q["conv_w"][layer], q["conv_b"][layer], q["dtb_e"][layer],
                 q["alog_e"][layer], q["dsk_e"][layer], q["norm_w"][layer], q["dtb_r"][layer], q["alog_r"][layer],
                 batch=batch, seq_len=seq_len, tb=min(SSD_TB, seq_len), chunk=SSD_L)

    y_fox = _fox(kx, qt, vt, batch=batch, seq_len=seq_len, tq=min(FOX_TQ, seq_len), tk=min(FOX_TK, seq_len))

    y_pool = _pool(pool_in, q["pool_wbd"][layer], q["pool_b"][layer], q["pool_sc"][layer],
                   batch=batch, seq_len=seq_len, tb=min(POOL_TB, seq_len))

    return _out_proj(x2, y_ssd, y_fox, y_pool, q["w_out"][layer], q["ln1_g"][layer], q["ln1_b"][layer],
                     tm=min(OUT_TM, seq_len))


def _cross_attention(x2, mem2, q, layer, *, batch, seq_len, mem_len):
    kv = _kv_proj(mem2, q["xa_wkv"][layer])
    return _xattn(x2, kv, q["xa_wq"][layer], q["xa_wo"][layer], q["ln2_g"][layer], q["ln2_b"][layer],
                  batch=batch, seq_len=seq_len, mem_len=mem_len, tm=min(XA_TM, seq_len))


def _dense_ffn(x2, q, layer):
    j = layer // 2
    n = x2.shape[0]
    tm = min(FFN_TM, n)
    tile_expert = jnp.full((n // tm,), j, jnp.int32)
    nvalid = jnp.full((1,), n // tm, jnp.int32)
    return _ffn(tile_expert, nvalid, x2, q["ffn_w1"], q["ffn_w3"], q["ffn_w2"],
                q["ln3_g"][layer], q["ln3_b"][layer], tm=tm, tf=FFN_TF, grouped=False)


def _moe_ffn(x2, q, layer):
    j = layer // 2
    n = x2.shape[0]
    tm = min(MOE_TM, n)
    row_tm = min(ROW_TM, n)
    route, counts = _router(x2, q["router"][j], tm=min(ROUTE_TM, n))

    counts = counts[0, :N_EXPERTS].astype(jnp.int32)
    padded = (counts + tm - 1) // tm * tm
    pend = jnp.cumsum(padded)
    pstart = pend - padded
    experts = route[:, 0:TOP_K].astype(jnp.int32)
    ranks = route[:, 4:4 + TOP_K].astype(jnp.int32)
    pos = (pstart[experts] + ranks).astype(jnp.int32)
    pos3 = pos.reshape(n // row_tm, POS_ROWS, LANES)
    cap = n * TOP_K + N_EXPERTS * tm
    ntiles = cap // tm
    tile_start = jnp.arange(ntiles, dtype=jnp.int32) * tm
    tile_expert = jnp.minimum(jnp.sum((pend[None, :] <= tile_start[:, None]).astype(jnp.int32), axis=1),
                              N_EXPERTS - 1).astype(jnp.int32)
    nvalid = (pend[-1:] // tm).astype(jnp.int32)

    xs = _dispatch(pend.astype(jnp.int32), padded.astype(jnp.int32), pos3, x2, cap=cap, row_tm=row_tm, pad_tm=tm)
    ys = _ffn(tile_expert + j * N_EXPERTS, nvalid, xs, q["moe_w1"], q["moe_w3"], q["moe_w2"],
              q["ln3_g"][layer], q["ln3_b"][layer], tm=tm, tf=FFN_TF, grouped=True)
    return _combine(pos3, x2, route, ys, q["ln3_g"][layer], q["ln3_b"][layer], row_tm=row_tm)


def _forward(x, mem, p):
    batch, seq_len, _ = x.shape
    mem_len = mem.shape[1]
    x2 = x.reshape(batch * seq_len, D_MODEL).astype(F32)
    mem2 = mem.reshape(batch * mem_len, D_MODEL).astype(F32)
    q = _prepare(p)
    for layer in range(DEPTH):
        x2 = _token_mixing(x2, q, layer, batch=batch, seq_len=seq_len)
        x2 = _cross_attention(x2, mem2, q, layer, batch=batch, seq_len=seq_len, mem_len=mem_len)
        x2 = _dense_ffn(x2, q, layer) if layer % 2 == 0 else _moe_ffn(x2, q, layer)
    return x2.reshape(batch, seq_len, D_MODEL)


def kernel(x, mem, w_in, ssm_conv_w, ssm_conv_b, ssm_dt_bias, ssm_a_log, ssm_d, ssm_norm_w, fox_f_bias, pool_w, pool_b, pool_scale, w_out, ln1_g, ln1_b, xa_wq, xa_wk, xa_wv, xa_wo, ln2_g, ln2_b, ffn_w1, ffn_w3, ffn_w2, router_w, moe_w1, moe_w3, moe_w2, ln3_g, ln3_b):
    p = dict(w_in=w_in, ssm_conv_w=ssm_conv_w, ssm_conv_b=ssm_conv_b, ssm_dt_bias=ssm_dt_bias,
             ssm_a_log=ssm_a_log, ssm_d=ssm_d, ssm_norm_w=ssm_norm_w, fox_f_bias=fox_f_bias, pool_w=pool_w,
             pool_b=pool_b, pool_scale=pool_scale, w_out=w_out, ln1_g=ln1_g, ln1_b=ln1_b, xa_wq=xa_wq,
             xa_wk=xa_wk, xa_wv=xa_wv, xa_wo=xa_wo, ln2_g=ln2_g, ln2_b=ln2_b, ffn_w1=ffn_w1, ffn_w3=ffn_w3,
             ffn_w2=ffn_w2, router_w=router_w, moe_w1=moe_w1, moe_w3=moe_w3, moe_w2=moe_w2, ln3_g=ln3_g,
             ln3_b=ln3_b)
    return _forward(x, mem, p)
```
